```python
import jax, jax.numpy as jnp
from jax import lax
import numpy as np

D_MODEL = 1024
BATCH = 8
SEQ = 4096
DEPTH = 2
DEC_BATCH = 8
DEC_SEQ = 2048
PAST_LEN = 128

D_MIX = D_MODEL
HEAD_DIM = 64
CONV_W = 3 * D_MIX // 8
FOURIER_W = D_MIX // 4
SGU_W = D_MIX - CONV_W - FOURIER_W
CONV_HEADS = CONV_W // HEAD_DIM
FOURIER_GROUPS = FOURIER_W // HEAD_DIM
SGU_HEADS = SGU_W // HEAD_DIM
CONV_K = 3
CHUNK = 128
IN_W = 3 * CONV_W + FOURIER_W + 2 * SGU_W
SPLITS = (CONV_W, 2 * CONV_W, 3 * CONV_W, 3 * CONV_W + FOURIER_W, 3 * CONV_W + FOURIER_W + SGU_W)
N_EXPERTS = 16
CAPACITY_FACTOR = 2
D_EXPERT = D_MODEL
N_MOD = 6
EPS = 1e-6

kernel_name = 'hybrid_parallel_conv_fourier_sgu_ec_moe_encoder'


def rms_norm(x, g):
    xf = x.astype(jnp.float32)
    y = xf * lax.rsqrt(jnp.mean(xf * xf, axis=-1, keepdims=True) + EPS)
    return y.astype(x.dtype) * g


def modulate(x, g, shift, scale):
    return rms_norm(x, g) * (1 + scale[:, None, :]) + shift[:, None, :]


def short_conv_mixer(zb, zc, za, conv_w):
    a = zc * za
    a_prev = jnp.pad(a[:, :-1], ((0, 0), (1, 0), (0, 0)))
    a_next = jnp.pad(a[:, 1:], ((0, 0), (0, 1), (0, 0)))
    return zb * (conv_w[0] * a_prev + conv_w[1] * a + conv_w[2] * a_next)


def fourier_mixer(zf):
    b, s, _ = zf.shape
    f = zf.reshape(b, s, FOURIER_GROUPS, HEAD_DIM).astype(jnp.float32)
    y = jnp.real(jnp.fft.fft2(f, axes=(1, 3), norm='ortho'))
    return y.reshape(b, s, FOURIER_W).astype(zf.dtype)


def spatial_gating_mixer(zu, zv, g_sgu, w_sp, b_sp):
    b, s, _ = zu.shape
    u = jax.nn.gelu(zu)
    v = rms_norm(jax.nn.gelu(zv), g_sgu)
    v = v.reshape(b, s // CHUNK, CHUNK, SGU_HEADS, HEAD_DIM)
    sv = jnp.einsum('hpq,bnqhd->bnphd', w_sp, v) + b_sp.T[None, None, :, :, None]
    return u * sv.reshape(b, s, SGU_W)


def token_mixer(h, w_in, conv_w, g_sgu, w_sp, b_sp, g_grp, w_out):
    z = h @ w_in
    zb, zc, za, zf, zu, zv = jnp.split(z, SPLITS, axis=-1)
    ya = rms_norm(short_conv_mixer(zb, zc, za, conv_w), g_grp[:CONV_W])
    yf = rms_norm(fourier_mixer(zf), g_grp[CONV_W:CONV_W + FOURIER_W])
    yc = rms_norm(spatial_gating_mixer(zu, zv, g_sgu, w_sp, b_sp), g_grp[CONV_W + FOURIER_W:])
    return jnp.concatenate([ya, yf, yc], axis=-1) @ w_out


def expert_choice_moe(h, w_router, w_gate, w_up, w_down):
    b, s, d = h.shape
    n = b * s
    cap = CAPACITY_FACTOR * n // N_EXPERTS
    hf = h.reshape(n, d)
    aff = jax.nn.softmax(hf.astype(jnp.float32) @ w_router.astype(jnp.float32), axis=-1)
    gates, idx = lax.top_k(aff.T, cap)
    xe = hf[idx]
    hid = jax.nn.silu(jnp.einsum('ecd,edf->ecf', xe, w_gate)) * jnp.einsum('ecd,edf->ecf', xe, w_up)
    ye = jnp.einsum('ecf,efd->ecd', hid, w_down) * gates[..., None].astype(h.dtype)
    out = jnp.zeros_like(hf).at[idx.reshape(-1)].add(ye.reshape(-1, d))
    return out.reshape(b, s, d)


def encoder(x, c, w_ada, b_ada, g_mix, w_in, conv_w, g_sgu, w_spatial, b_spatial, g_grp, w_out,
            g_ffn, w_router, w_gate, w_up, w_down, g_final):
    for l in range(DEPTH):
        mod = jax.nn.silu(c) @ w_ada[l] + b_ada[l]
        sh_m, sc_m, gt_m, sh_f, sc_f, gt_f = jnp.split(mod, N_MOD, axis=-1)
        h = modulate(x, g_mix[l], sh_m, sc_m)
        x = x + gt_m[:, None, :] * token_mixer(h, w_in[l], conv_w[l], g_sgu[l], w_spatial[l],
                                               b_spatial[l], g_grp[l], w_out[l])
        h = modulate(x, g_ffn[l], sh_f, sc_f)
        x = x + gt_f[:, None, :] * expert_choice_moe(h, w_router[l], w_gate[l], w_up[l], w_down[l])
    return rms_norm(x, g_final)


def setup_inputs(seed: int = 0) -> dict:
    key = jax.random.key(seed)
    ks = jax.random.split(key, 24)
    nrm = jax.random.normal
    f32 = jnp.float32
    return {
        'x_prompt': nrm(ks[0], (BATCH, SEQ, D_MODEL), f32),
        'x_sample': nrm(ks[1], (DEC_BATCH, DEC_SEQ, D_MODEL), f32),
        'c_prompt': nrm(ks[2], (BATCH, D_MODEL), f32),
        'c_sample': nrm(ks[3], (DEC_BATCH, D_MODEL), f32),
        'w_ada': nrm(ks[4], (DEPTH, D_MODEL, N_MOD * D_MODEL), f32) * (0.5 * D_MODEL ** -0.5),
        'b_ada': nrm(ks[5], (DEPTH, N_MOD * D_MODEL), f32) * 0.02,
        'g_mix': 1.0 + 0.02 * nrm(ks[6], (DEPTH, D_MODEL), f32),
        'w_in': nrm(ks[7], (DEPTH, D_MODEL, IN_W), f32) * D_MODEL ** -0.5,
        'conv_w': nrm(ks[8], (DEPTH, CONV_K, CONV_W), f32) * CONV_K ** -0.5,
        'g_sgu': 1.0 + 0.02 * nrm(ks[9], (DEPTH, SGU_W), f32),
        'w_spatial': nrm(ks[10], (DEPTH, SGU_HEADS, CHUNK, CHUNK), f32) * CHUNK ** -0.5,
        'b_spatial': 1.0 + 0.02 * nrm(ks[11], (DEPTH, SGU_HEADS, CHUNK), f32),
        'g_grp': 1.0 + 0.02 * nrm(ks[12], (DEPTH, D_MIX), f32),
        'w_out': nrm(ks[13], (DEPTH, D_MIX, D_MODEL), f32) * D_MIX ** -0.5,
        'g_ffn': 1.0 + 0.02 * nrm(ks[14], (DEPTH, D_MODEL), f32),
        'w_router': nrm(ks[15], (DEPTH, D_MODEL, N_EXPERTS), f32) * D_MODEL ** -0.5,
        'w_gate': nrm(ks[16], (DEPTH, N_EXPERTS, D_MODEL, D_EXPERT), f32) * D_MODEL ** -0.5,
        'w_up': nrm(ks[17], (DEPTH, N_EXPERTS, D_MODEL, D_EXPERT), f32) * D_MODEL ** -0.5,
        'w_down': nrm(ks[18], (DEPTH, N_EXPERTS, D_EXPERT, D_MODEL), f32) * D_EXPERT ** -0.5,
        'g_final': 1.0 + 0.02 * nrm(ks[19], (D_MODEL,), f32),
    }


def reference(x_prompt, x_sample, c_prompt, c_sample, w_ada, b_ada, g_mix, w_in, conv_w, g_sgu,
              w_spatial, b_spatial, g_grp, w_out, g_ffn, w_router, w_gate, w_up, w_down, g_final):
    y_prompt = encoder(x_prompt, c_prompt, w_ada, b_ada, g_mix, w_in, conv_w, g_sgu, w_spatial,
                       b_spatial, g_grp, w_out, g_ffn, w_router, w_gate, w_up, w_down, g_final)
    y_sample = encoder(x_sample, c_sample, w_ada, b_ada, g_mix, w_in, conv_w, g_sgu, w_spatial,
                       b_spatial, g_grp, w_out, g_ffn, w_router, w_gate, w_up, w_down, g_final)
    return (y_prompt, y_sample)
```

```python
import functools
import math

import jax
import jax.numpy as jnp
from jax import lax
from jax.experimental import pallas as pl
from jax.experimental.pallas import tpu as pltpu

D_MODEL = 1024
HEAD_DIM = 64
CONV_W = 384
FOURIER_W = 256
SGU_W = 384
SGU_HEADS = SGU_W // HEAD_DIM
CHUNK = 128
IN_W = 3 * CONV_W + FOURIER_W + 2 * SGU_W
N_EXPERTS = 16
CAPACITY_FACTOR = 2
N_MOD = 6
EPS = 1e-6

LANES = 128
SUBLANES = 8
VMEM_LIMIT_BYTES = 52 * 1024 * 1024

_HI = lax.Precision.HIGHEST
_BF16 = jnp.bfloat16
_F32 = jnp.float32


def _rms(x):
    return x * lax.rsqrt(jnp.mean(x * x, axis=-1, keepdims=True) + EPS)


def _params(sem, vmem=VMEM_LIMIT_BYTES):
    return pltpu.CompilerParams(dimension_semantics=sem, vmem_limit_bytes=vmem)


def _ada_kernel(c_ref, w_ref, b_ref, o_ref):
    c = c_ref[...]
    a = c * jax.nn.sigmoid(c)
    o_ref[0] = jnp.dot(a, w_ref[0], precision=_HI, preferred_element_type=_F32) + b_ref[0]


def _ada(c_all, w_ada, b_ada):
    depth, d, m = w_ada.shape
    nb = c_all.shape[0]
    tn = 1536
    return pl.pallas_call(
        _ada_kernel,
        grid=(depth, m // tn),
        in_specs=[
            pl.BlockSpec((nb, d), lambda l, j: (0, 0)),
            pl.BlockSpec((1, d, tn), lambda l, j: (l, 0, j)),
            pl.BlockSpec((1, 1, tn), lambda l, j: (l, 0, j)),
        ],
        out_specs=pl.BlockSpec((1, nb, tn), lambda l, j: (l, 0, j)),
        out_shape=jax.ShapeDtypeStruct((depth, nb, m), _F32),
        compiler_params=_params(("arbitrary", "arbitrary")),
        name="ada",
    )(c_all, w_ada, b_ada.reshape(depth, 1, m))


def _mix_in_kernel(x_ref, xp_ref, xn_ref, mod_ref, gmix_ref, win_ref, convw_ref, gsgu_ref,
                   wsp_ref, bsp_ref, ggrp_ref, dft_ref, yac_ref, fc_ref, fs_ref, *, tile):
    t = pl.program_id(1)
    nt = pl.num_programs(1)
    shift = mod_ref[0, :, 0:D_MODEL]
    scale = mod_ref[0, :, D_MODEL:2 * D_MODEL]
    gmix = gmix_ref[...]

    def modulated(x):
        return ((_rms(x) * gmix) * (1.0 + scale) + shift).astype(_BF16)

    h = modulated(x_ref[0])
    z = jnp.dot(h, win_ref[...], preferred_element_type=_F32)

    def halo(xh_ref):
        zh = jnp.dot(modulated(xh_ref[0]), win_ref[:, CONV_W:3 * CONV_W],
                     preferred_element_type=_F32)
        return zh[:, :CONV_W] * zh[:, CONV_W:]

    a_before = jnp.where(t > 0, halo(xp_ref)[SUBLANES - 1:SUBLANES, :], 0.0)
    a_after = jnp.where(t < nt - 1, halo(xn_ref)[0:1, :], 0.0)

    zb = z[:, 0:CONV_W]
    a = z[:, CONV_W:2 * CONV_W] * z[:, 2 * CONV_W:3 * CONV_W]
    row = lax.broadcasted_iota(jnp.int32, (tile, CONV_W), 0)
    a_prev = jnp.where(row == 0, a_before, pltpu.roll(a, 1, axis=0))
    a_next = jnp.where(row == tile - 1, a_after, pltpu.roll(a, tile - 1, axis=0))
    cw = convw_ref[...]
    conv = zb * (cw[0:1, :] * a_prev + cw[1:2, :] * a + cw[2:3, :] * a_next)
    ya = (_rms(conv) * ggrp_ref[:, 0:CONV_W]).astype(_BF16)

    zf = z[:, 3 * CONV_W:3 * CONV_W + FOURIER_W].astype(_BF16)
    f2 = jnp.dot(zf, dft_ref[...], preferred_element_type=_F32)
    fc_ref[...] = f2[:, :FOURIER_W].astype(_BF16)
    fs_ref[...] = f2[:, FOURIER_W:].astype(_BF16)

    off = 3 * CONV_W + FOURIER_W
    u = jax.nn.gelu(z[:, off:off + SGU_W])
    v = (_rms(jax.nn.gelu(z[:, off + SGU_W:off + 2 * SGU_W])) * gsgu_ref[...]).astype(_BF16)
    lane = lax.broadcasted_iota(jnp.int32, (CHUNK, LANES), 1)
    first_head = lane < HEAD_DIM
    chunks = []
    for c in range(tile // CHUNK):
        pairs = []
        for k in range(SGU_W // LANES):
            vk = v[c * CHUNK:(c + 1) * CHUNK, k * LANES:(k + 1) * LANES]
            r0 = jnp.dot(wsp_ref[2 * k], vk, preferred_element_type=_F32)
            r1 = jnp.dot(wsp_ref[2 * k + 1], vk, preferred_element_type=_F32)
            pairs.append(jnp.where(first_head, r0, r1))
        chunks.append(jnp.concatenate(pairs, axis=1) + bsp_ref[...])
    sv = jnp.concatenate(chunks, axis=0)
    yc = (_rms(u * sv) * ggrp_ref[:, CONV_W + FOURIER_W:]).astype(_BF16)
    yac_ref[0] = jnp.concatenate([ya, yc], axis=1)


def _mix_in(x, mod_l, gmix, win_bf, convw, gsgu, wsp_bf, bsp_rows, ggrp, dft64, tile):
    b, s, d = x.shape
    nt = s // tile
    hb = tile // SUBLANES
    last_hb = s // SUBLANES - 1
    const2 = lambda i, t: (0, 0)
    return pl.pallas_call(
        functools.partial(_mix_in_kernel, tile=tile),
        grid=(b, nt),
        in_specs=[
            pl.BlockSpec((1, tile, d), lambda i, t: (i, t, 0)),
            pl.BlockSpec((1, SUBLANES, d), lambda i, t: (i, jnp.maximum(t * hb - 1, 0), 0)),
            pl.BlockSpec((1, SUBLANES, d), lambda i, t: (i, jnp.minimum((t + 1) * hb, last_hb), 0)),
            pl.BlockSpec((1, 1, N_MOD * d), lambda i, t: (i, 0, 0)),
            pl.BlockSpec((1, d), const2),
            pl.BlockSpec((d, IN_W), const2),
            pl.BlockSpec((3, CONV_W), const2),
            pl.BlockSpec((1, SGU_W), const2),
            pl.BlockSpec((SGU_HEADS, CHUNK, CHUNK), lambda i, t: (0, 0, 0)),
            pl.BlockSpec((CHUNK, SGU_W), const2),
            pl.BlockSpec((1, d), const2),
            pl.BlockSpec((FOURIER_W, 2 * FOURIER_W), const2),
        ],
        out_specs=[
            pl.BlockSpec((1, tile, CONV_W + SGU_W), lambda i, t: (i, t, 0)),
            pl.BlockSpec((tile, FOURIER_W), lambda i, t: (t, i)),
            pl.BlockSpec((tile, FOURIER_W), lambda i, t: (t, i)),
        ],
        out_shape=[
            jax.ShapeDtypeStruct((b, s, CONV_W + SGU_W), _BF16),
            jax.ShapeDtypeStruct((s, b * FOURIER_W), _BF16),
            jax.ShapeDtypeStruct((s, b * FOURIER_W), _BF16),
        ],
        compiler_params=_params(("arbitrary", "arbitrary")),
        name="mix_in",
    )(x, x, x, mod_l, gmix, win_bf, convw, gsgu, wsp_bf, bsp_rows, ggrp, dft64)


def _seq_dft_kernel(wc_ref, ws_ref, fc_ref, fs_ref, o_ref):
    part = (jnp.dot(wc_ref[...], fc_ref[...], preferred_element_type=_F32)
            + jnp.dot(ws_ref[...], fs_ref[...], preferred_element_type=_F32))

    @pl.when(pl.program_id(2) == 0)
    def _():
        o_ref[...] = part

    @pl.when(pl.program_id(2) > 0)
    def _():
        o_ref[...] += part


def _seq_dft(wc, wsn, fc, fs):
    s = wc.shape[0]
    n = fc.shape[1]
    tm = min(1024, s)
    tn = min(1024, n)
    tk = min(1024, s)
    return pl.pallas_call(
        _seq_dft_kernel,
        grid=(s // tm, n // tn, s // tk),
        in_specs=[
            pl.BlockSpec((tm, tk), lambda i, j, k: (i, k)),
            pl.BlockSpec((tm, tk), lambda i, j, k: (i, k)),
            pl.BlockSpec((tk, tn), lambda i, j, k: (k, j)),
            pl.BlockSpec((tk, tn), lambda i, j, k: (k, j)),
        ],
        out_specs=pl.BlockSpec((tm, tn), lambda i, j, k: (i, j)),
        out_shape=jax.ShapeDtypeStruct((s, n), _F32),
        compiler_params=_params(("arbitrary", "arbitrary", "arbitrary")),
        name="seq_dft",
    )(wc, wsn, fc, fs)


def _mix_out_kernel(x_ref, yac_ref, yf_ref, mod_ref, ggrp_ref, wout_ref, gffn_ref, wr_ref,
                    x1_ref, h2_ref, aff_ref, *, seq_scale):
    d = D_MODEL
    gate_m = mod_ref[0, :, 2 * d:3 * d]
    shift_f = mod_ref[0, :, 3 * d:4 * d]
    scale_f = mod_ref[0, :, 4 * d:5 * d]
    yf = (_rms(yf_ref[...] * seq_scale) * ggrp_ref[:, CONV_W:CONV_W + FOURIER_W]).astype(_BF16)
    yac = yac_ref[0]
    ycat = jnp.concatenate([yac[:, :CONV_W], yf, yac[:, CONV_W:]], axis=1)
    mix = jnp.dot(ycat, wout_ref[...], preferred_element_type=_F32)
    x1 = x_ref[0] + gate_m * mix
    x1_ref[0] = x1
    h2 = (_rms(x1) * gffn_ref[...]) * (1.0 + scale_f) + shift_f
    h2_ref[0] = h2
    logits = jnp.dot(h2, wr_ref[...], precision=_HI, preferred_element_type=_F32)
    e = jnp.exp(logits - jnp.max(logits, axis=-1, keepdims=True))
    aff_ref[0] = e / jnp.sum(e, axis=-1, keepdims=True)


def _mix_out(x, yac, yf, mod_l, ggrp, wout_bf, gffn, w_router, tile):
    b, s, d = x.shape
    const2 = lambda i, t: (0, 0)
    return pl.pallas_call(
        functools.partial(_mix_out_kernel, seq_scale=1.0 / math.sqrt(s)),
        grid=(b, s // tile),
        in_specs=[
            pl.BlockSpec((1, tile, d), lambda i, t: (i, t, 0)),
            pl.BlockSpec((1, tile, CONV_W + SGU_W), lambda i, t: (i, t, 0)),
            pl.BlockSpec((tile, FOURIER_W), lambda i, t: (t, i)),
            pl.BlockSpec((1, 1, N_MOD * d), lambda i, t: (i, 0, 0)),
            pl.BlockSpec((1, d), const2),
            pl.BlockSpec((d, d), const2),
            pl.BlockSpec((1, d), const2),
            pl.BlockSpec((d, N_EXPERTS), const2),
        ],
        out_specs=[
            pl.BlockSpec((1, tile, d), lambda i, t: (i, t, 0)),
            pl.BlockSpec((1, tile, d), lambda i, t: (i, t, 0)),
            pl.BlockSpec((1, tile, N_EXPERTS), lambda i, t: (i, t, 0)),
        ],
        out_shape=[
            jax.ShapeDtypeStruct((b, s, d), _F32),
            jax.ShapeDtypeStruct((b, s, d), _F32),
            jax.ShapeDtypeStruct((b, s, N_EXPERTS), _F32),
        ],
        compiler_params=_params(("arbitrary", "arbitrary")),
        name="mix_out",
    )(x, yac, yf, mod_l, ggrp, wout_bf, gffn, w_router)


def _cumsum_mats(rows):
    qi = lax.broadcasted_iota(jnp.int32, (LANES, LANES), 0)
    qj = lax.broadcasted_iota(jnp.int32, (LANES, LANES), 1)
    ri = lax.broadcasted_iota(jnp.int32, (rows, rows), 0)
    rj = lax.broadcasted_iota(jnp.int32, (rows, rows), 1)
    def ones_where(mask):
        return jnp.where(mask, 1.0, 0.0).astype(_BF16)

    lane_excl = ones_where(qi < qj)
    lane_incl = ones_where(qi <= qj)
    row_excl = ones_where(rj < ri)
    row_incl_t = ones_where(ri <= rj)
    return lane_excl, lane_incl, row_excl, row_incl_t


def _route_kernel(aff_ref, idx_ref, gate_ref, *, cap, jb):
    a = aff_ref[0]
    rows = a.shape[0]
    lane_excl, lane_incl, row_excl, row_incl_t = _cumsum_mats(rows)

    def count(mask):
        return jnp.sum(jnp.where(mask, 1.0, 0.0)).astype(jnp.int32)

    def bisect(i, thr_bits):
        cand = thr_bits | (jnp.int32(1) << (30 - i))
        cand_f = lax.bitcast_convert_type(cand, _F32)
        return jnp.where(count(a >= cand_f) >= cap, cand, thr_bits)

    thr = lax.bitcast_convert_type(lax.fori_loop(0, 31, bisect, jnp.int32(0)), _F32)
    above = a > thr
    tied = a == thr
    need = cap - count(above)

    tied_b = jnp.where(tied, 1.0, 0.0).astype(_BF16)
    in_row = jnp.dot(tied_b, lane_excl, preferred_element_type=_F32)
    row_tot = jnp.dot(tied_b, jnp.ones((LANES, LANES), _BF16), preferred_element_type=_F32)
    before_row = jnp.dot(row_excl, row_tot.astype(_BF16), preferred_element_type=_F32)
    tie_rank = (in_row + before_row).astype(jnp.int32)
    sel = above | (tied & (tie_rank < need))
    sel_b = jnp.where(sel, 1.0, 0.0).astype(_BF16)

    row_cnt = lax.dot_general(jnp.ones((SUBLANES, LANES), _BF16), sel_b,
                              (((1,), (1,)), ((), ())), preferred_element_type=_F32)
    incl = jnp.dot(row_cnt.astype(_BF16), row_incl_t, preferred_element_type=_F32)[0:1, :]
    excl = incl - row_cnt[0:1, :]
    row_id = lax.broadcasted_iota(jnp.int32, (1, rows), 1).astype(_F32)
    lane_id = lax.broadcasted_iota(jnp.int32, (jb, LANES), 1).astype(_F32)

    for c in range(cap // jb):
        j = (lax.broadcasted_iota(jnp.int32, (jb, 1), 0) + c * jb).astype(_F32)
        hit = (excl <= j) & (j < incl)
        hit_f = jnp.where(hit, 1.0, 0.0)
        r_of_j = jnp.sum(hit_f * row_id, axis=1, keepdims=True)
        target = j + 1.0 - jnp.sum(hit_f * excl, axis=1, keepdims=True)
        row_sel = jnp.dot(hit_f.astype(_BF16), sel_b, preferred_element_type=_F32)
        within = jnp.dot(row_sel.astype(_BF16), lane_incl, preferred_element_type=_F32)
        l_of_j = jnp.sum(jnp.where(within < target, 1.0, 0.0), axis=1, keepdims=True)
        a_row = jnp.dot(hit_f, a, precision=_HI, preferred_element_type=_F32)
        gate = jnp.sum(jnp.where(lane_id == l_of_j, a_row, 0.0), axis=1, keepdims=True)
        token = (r_of_j * LANES + l_of_j).astype(jnp.int32)
        idx_ref[0, c * jb:(c + 1) * jb, :] = jnp.broadcast_to(token, (jb, LANES))
        gate_ref[0, c * jb:(c + 1) * jb, :] = jnp.broadcast_to(gate, (jb, LANES))


def _route(aff_t, cap):
    e, rows, _ = aff_t.shape
    jb = min(512, cap)
    return pl.pallas_call(
        functools.partial(_route_kernel, cap=cap, jb=jb),
        grid=(e,),
        in_specs=[pl.BlockSpec((1, rows, LANES), lambda i: (i, 0, 0))],
        out_specs=[
            pl.BlockSpec((1, cap, LANES), lambda i: (i, 0, 0)),
            pl.BlockSpec((1, cap, LANES), lambda i: (i, 0, 0)),
        ],
        out_shape=[
            jax.ShapeDtypeStruct((e, cap, LANES), jnp.int32),
            jax.ShapeDtypeStruct((e, cap, LANES), _F32),
        ],
        compiler_params=_params(("arbitrary",)),
        name="route",
    )(aff_t)


def _experts_kernel(idx_hbm, h2_hbm, xin_hbm, tok_ref, gate_ref, gf_ref, wg_ref, wu_ref, wd_ref,
                    xout_hbm, idx_a, idx_b, idx_n, h_a, h_b, xo_a, xo_b, sems,
                    *, rows, nsteps, seq_shift, nbatch):
    del xin_hbm
    e = pl.program_id(0)
    j = pl.program_id(1)
    step = e * nsteps + j
    last_step = pl.num_programs(0) * nsteps - 1
    blk_a = 2 * step
    blk_n = jnp.minimum(blk_a + 2, 2 * last_step + 1)
    SEM_IDX, SEM_HA, SEM_HB, SEM_XA, SEM_XB, SEM_SA, SEM_SB = range(7)

    idx_copies = [pltpu.make_async_copy(idx_hbm.at[b], ref, sems.at[SEM_IDX])
                  for b, ref in ((blk_a, idx_a), (blk_a + 1, idx_b), (blk_n, idx_n))]
    for c in idx_copies:
        c.start()
    for c in idx_copies:
        c.wait()

    def issue_rows(idx_ref, jobs):
        for r in range(rows):
            tok = idx_ref[0, r]
            for k, (src, dst, sem, kind) in enumerate(jobs):
                if kind == "gather":
                    cp = pltpu.make_async_copy(src.at[pl.ds(tok, 1)], dst.at[pl.ds(r, 1)], sems.at[sem])
                else:
                    cp = pltpu.make_async_copy(src.at[pl.ds(r, 1)], dst.at[pl.ds(tok, 1)], sems.at[sem])
                cp.start(priority=(r + k) % 2)

    def wait_rows(hbm, buf, sem, to_hbm=False):
        if to_hbm:
            pltpu.make_async_copy(buf, hbm.at[pl.ds(0, rows)], sems.at[sem]).wait()
        else:
            pltpu.make_async_copy(hbm.at[pl.ds(0, rows)], buf, sems.at[sem]).wait()

    @pl.when(step == 0)
    def _():
        def first(r, carry):
            tok = idx_a[0, r]
            pltpu.make_async_copy(h2_hbm.at[pl.ds(tok, 1)], h_a.at[pl.ds(r, 1)], sems.at[SEM_HA]).start()
            return carry
        lax.fori_loop(0, rows, first, 0)

    @pl.when(step > 0)
    def _():
        wait_rows(xout_hbm, xo_a, SEM_SA, to_hbm=True)

    @pl.when((step > 0) & (j == 0))
    def _():
        wait_rows(xout_hbm, xo_b, SEM_SB, to_hbm=True)

    def block(part, idx_ref, idx_next, h_buf, h_next, xo_buf, sem_h, sem_hn, sem_x, sem_s):
        issue_rows(idx_ref, [(xout_hbm, xo_buf, sem_x, "gather")])
        issue_rows(idx_next, [(h2_hbm, h_next, sem_hn, "gather")])
        wait_rows(h2_hbm, h_buf, sem_h)
        x = h_buf[...].astype(_BF16)
        hg = jnp.dot(x, wg_ref[0, 0].astype(_BF16), preferred_element_type=_F32)
        hu = jnp.dot(x, wu_ref[0, 0].astype(_BF16), preferred_element_type=_F32)
        hid = (hg * jax.nn.sigmoid(hg) * hu).astype(_BF16)
        y = jnp.dot(hid, wd_ref[0, 0].astype(_BF16), preferred_element_type=_F32)
        rs = slice(part * rows, (part + 1) * rows)
        seq_of_row = jnp.concatenate([tok_ref[rs, :] >> seq_shift] * (D_MODEL // LANES), axis=1)
        gf = jnp.zeros_like(y)
        for b in range(nbatch):
            gf = jnp.where(seq_of_row == b, gf_ref[b:b + 1, :], gf)
        gate = jnp.concatenate([gate_ref[rs, :]] * (D_MODEL // LANES), axis=1)
        upd = gf * (y * gate)
        wait_rows(xout_hbm, xo_buf, sem_x)
        xo_buf[...] = xo_buf[...] + upd
        issue_rows(idx_ref, [(xo_buf, xout_hbm, sem_s, "scatter")])

    block(0, idx_a, idx_b, h_a, h_b, xo_a, SEM_HA, SEM_HB, SEM_XA, SEM_SA)

    @pl.when((step > 0) & (j > 0))
    def _():
        wait_rows(xout_hbm, xo_b, SEM_SB, to_hbm=True)

    block(1, idx_b, idx_n, h_b, h_a, xo_b, SEM_HB, SEM_HA, SEM_XB, SEM_SB)

    @pl.when(step == last_step)
    def _():
        wait_rows(h2_hbm, h_a, SEM_HA)
        wait_rows(xout_hbm, xo_a, SEM_SA, to_hbm=True)
        wait_rows(xout_hbm, xo_b, SEM_SB, to_hbm=True)


def _experts(idx_blocks, h2, x1, tok_rep, gate_rep, gate_f, w_gate, w_up, w_down, layer, seq, rows):
    n, d = x1.shape
    e = w_gate.shape[1]
    cap = tok_rep.shape[0] // e
    nsteps = cap // (2 * rows)
    nbatch = gate_f.shape[0]
    seq_shift = seq.bit_length() - 1
    assert 1 << seq_shift == seq and nsteps * 2 * rows == cap
    wspec = pl.BlockSpec((1, 1, d, d), lambda i, j: (layer, i, 0, 0))
    any_spec = pl.BlockSpec(memory_space=pl.ANY)
    return pl.pallas_call(
        functools.partial(_experts_kernel, rows=rows, nsteps=nsteps, seq_shift=seq_shift, nbatch=nbatch),
        grid=(e, nsteps),
        in_specs=[
            any_spec, any_spec, any_spec,
            pl.BlockSpec((2 * rows, LANES), lambda i, j: (i * nsteps + j, 0)),
            pl.BlockSpec((2 * rows, LANES), lambda i, j: (i * nsteps + j, 0)),
            pl.BlockSpec((nbatch, d), lambda i, j: (0, 0)),
            wspec, wspec, wspec,
        ],
        out_specs=any_spec,
        out_shape=jax.ShapeDtypeStruct((n, d), _F32),
        scratch_shapes=[
            pltpu.SMEM((1, rows), jnp.int32),
            pltpu.SMEM((1, rows), jnp.int32),
            pltpu.SMEM((1, rows), jnp.int32),
            pltpu.VMEM((rows, d), _F32),
            pltpu.VMEM((rows, d), _F32),
            pltpu.VMEM((rows, d), _F32),
            pltpu.VMEM((rows, d), _F32),
            pltpu.SemaphoreType.DMA((7,)),
        ],
        input_output_aliases={2: 0},
        compiler_params=_params(("arbitrary", "arbitrary")),
        name="experts",
    )(idx_blocks, h2, x1, tok_rep, gate_rep, gate_f, w_gate, w_up, w_down)


def _final_kernel(x_ref, g_ref, o_ref):
    o_ref[0] = _rms(x_ref[0]) * g_ref[...]


def _final_norm(x, g, tile):
    b, s, d = x.shape
    return pl.pallas_call(
        _final_kernel,
        grid=(b, s // tile),
        in_specs=[pl.BlockSpec((1, tile, d), lambda i, t: (i, t, 0)),
                  pl.BlockSpec((1, d), lambda i, t: (0, 0))],
        out_specs=pl.BlockSpec((1, tile, d), lambda i, t: (i, t, 0)),
        out_shape=jax.ShapeDtypeStruct((b, s, d), _F32),
        compiler_params=_params(("arbitrary", "arbitrary")),
        name="final_norm",
    )(x, g)


def _channel_dft_table():
    k = jnp.arange(HEAD_DIM, dtype=jnp.int32)
    ang = ((k[:, None] * k[None, :]) % HEAD_DIM).astype(_F32) * (2.0 * math.pi / HEAD_DIM)
    groups = FOURIER_W // HEAD_DIM
    eye = jnp.eye(groups, dtype=_F32)
    c = jnp.kron(eye, jnp.cos(ang)) / math.sqrt(HEAD_DIM)
    s = jnp.kron(eye, jnp.sin(ang)) / math.sqrt(HEAD_DIM)
    return jnp.concatenate([c, s], axis=1).astype(_BF16)


def _seq_dft_tables(s):
    k = jnp.arange(s, dtype=jnp.int32)
    ang = ((k[:, None] * k[None, :]) % s).astype(_F32) * (2.0 * math.pi / s)
    return jnp.cos(ang).astype(_BF16), (-jnp.sin(ang)).astype(_BF16)


def _encoder(x, mod, p, dft64, tile, erows):
    b, s, d = x.shape
    n = b * s
    cap = CAPACITY_FACTOR * n // N_EXPERTS
    wc, wsn = _seq_dft_tables(s)
    depth = p["w_in"].shape[0]
    for l in range(depth):
        mod_l = mod[l][:, None, :]
        yac, fc, fs = _mix_in(x, mod_l, p["g_mix"][l][None], p["w_in_bf"][l], p["conv_w"][l],
                              p["g_sgu"][l][None], p["w_sp_bf"][l], p["b_sp_rows"][l],
                              p["g_grp"][l][None], dft64, tile)
        yf = _seq_dft(wc, wsn, fc, fs)
        x1, h2, aff = _mix_out(x, yac, yf, mod_l, p["g_grp"][l][None], p["w_out_bf"][l],
                               p["g_ffn"][l][None], p["w_router"][l], tile)
        aff_t = aff.reshape(n, N_EXPERTS).T.reshape(N_EXPERTS, n // LANES, LANES)
        tok_rep, gate_rep = _route(aff_t, cap)
        idx_blocks = tok_rep[:, :, 0].reshape(N_EXPERTS * cap // erows, 1, erows)
        gate_f = mod_l[:, 0, 5 * d:6 * d]
        x = _experts(idx_blocks, h2.reshape(n, d), x1.reshape(n, d),
                     tok_rep.reshape(N_EXPERTS * cap, LANES), gate_rep.reshape(N_EXPERTS * cap, LANES),
                     gate_f, p["w_gate"], p["w_up"], p["w_down"], l, s, erows).reshape(b, s, d)
    return _final_norm(x, p["g_final"][None], tile)


def kernel(x_prompt, x_sample, c_prompt, c_sample, w_ada, b_ada, g_mix, w_in, conv_w, g_sgu, w_spatial,
           b_spatial, g_grp, w_out, g_ffn, w_router, w_gate, w_up, w_down, g_final):
    nb = x_prompt.shape[0]
    mod = _ada(jnp.concatenate([c_prompt, c_sample], axis=0), w_ada, b_ada)
    p = dict(
        g_mix=g_mix, conv_w=conv_w, g_sgu=g_sgu, g_grp=g_grp, g_ffn=g_ffn, w_router=w_router,
        w_gate=w_gate, w_up=w_up, w_down=w_down, g_final=g_final, w_in=w_in,
        w_in_bf=w_in.astype(_BF16), w_out_bf=w_out.astype(_BF16), w_sp_bf=w_spatial.astype(_BF16),
        b_sp_rows=jnp.repeat(jnp.swapaxes(b_spatial, 1, 2), HEAD_DIM, axis=2),
    )
    dft64 = _channel_dft_table()
    tile = min(512, x_sample.shape[1])
    erows = 512
    y_prompt = _encoder(x_prompt, mod[:, :nb], p, dft64, tile, erows)
    y_sample = _encoder(x_sample, mod[:, nb:], p, dft64, tile, erows)
    return (y_prompt, y_sample)
```

```python
import functools
import math

import jax
import jax.numpy as jnp
from jax import lax
from jax.experimental import pallas as pl
from jax.experimental.pallas import tpu as pltpu

D_MODEL = 1024
HEAD_DIM = 64
CONV_W = 384
FOURIER_W = 256
SGU_W = 384
SGU_HEADS = SGU_W // HEAD_DIM
CHUNK = 128
IN_W = 3 * CONV_W + FOURIER_W + 2 * SGU_W
N_EXPERTS = 16
CAPACITY_FACTOR = 2
N_MOD = 6
EPS = 1e-6

LANES = 128
SUBLANES = 8
VMEM_LIMIT_BYTES = 52 * 1024 * 1024
MXU_COLS = 256
EXPERT_COL_CHUNK = MXU_COLS

_HI = lax.Precision.HIGHEST
_BF16 = jnp.bfloat16
_F32 = jnp.float32


def _rms(x):
    return x * lax.rsqrt(jnp.mean(x * x, axis=-1, keepdims=True) + EPS)


def _params(sem, vmem=VMEM_LIMIT_BYTES):
    return pltpu.CompilerParams(dimension_semantics=sem, vmem_limit_bytes=vmem)


def _ada_kernel(c_ref, w_ref, b_ref, o_ref):
    c = c_ref[...]
    a = c * jax.nn.sigmoid(c)
    o_ref[0] = jnp.dot(a, w_ref[0], precision=_HI, preferred_element_type=_F32) + b_ref[0]


def _ada(c_all, w_ada, b_ada):
    depth, d, m = w_ada.shape
    nb = c_all.shape[0]
    tn = 1536
    return pl.pallas_call(
        _ada_kernel,
        grid=(depth, m // tn),
        in_specs=[
            pl.BlockSpec((nb, d), lambda l, j: (0, 0)),
            pl.BlockSpec((1, d, tn), lambda l, j: (l, 0, j)),
            pl.BlockSpec((1, 1, tn), lambda l, j: (l, 0, j)),
        ],
        out_specs=pl.BlockSpec((1, nb, tn), lambda l, j: (l, 0, j)),
        out_shape=jax.ShapeDtypeStruct((depth, nb, m), _F32),
        compiler_params=_params(("arbitrary", "arbitrary")),
        name="ada",
    )(c_all, w_ada, b_ada.reshape(depth, 1, m))


def _mix_in_kernel(x_ref, xp_ref, xn_ref, mod_ref, gmix_ref, win_ref, convw_ref, gsgu_ref,
                   wsp_ref, bsp_ref, ggrp_ref, dft_ref, yac_ref, fc_ref, fs_ref, *, tile):
    t = pl.program_id(1)
    nt = pl.num_programs(1)
    shift = mod_ref[0, :, 0:D_MODEL]
    scale = mod_ref[0, :, D_MODEL:2 * D_MODEL]
    gmix = gmix_ref[...]

    def modulated(x):
        return ((_rms(x) * gmix) * (1.0 + scale) + shift).astype(_BF16)

    h = modulated(x_ref[0])
    z = jnp.dot(h, win_ref[...], preferred_element_type=_F32)

    def halo(xh_ref):
        zh = jnp.dot(modulated(xh_ref[0]), win_ref[:, CONV_W:3 * CONV_W],
                     preferred_element_type=_F32)
        return zh[:, :CONV_W] * zh[:, CONV_W:]

    a_before = jnp.where(t > 0, halo(xp_ref)[SUBLANES - 1:SUBLANES, :], 0.0)
    a_after = jnp.where(t < nt - 1, halo(xn_ref)[0:1, :], 0.0)

    zb = z[:, 0:CONV_W]
    a = z[:, CONV_W:2 * CONV_W] * z[:, 2 * CONV_W:3 * CONV_W]
    row = lax.broadcasted_iota(jnp.int32, (tile, CONV_W), 0)
    a_prev = jnp.where(row == 0, a_before, pltpu.roll(a, 1, axis=0))
    a_next = jnp.where(row == tile - 1, a_after, pltpu.roll(a, tile - 1, axis=0))
    cw = convw_ref[...]
    conv = zb * (cw[0:1, :] * a_prev + cw[1:2, :] * a + cw[2:3, :] * a_next)
    ya = (_rms(conv) * ggrp_ref[:, 0:CONV_W]).astype(_BF16)

    zf = z[:, 3 * CONV_W:3 * CONV_W + FOURIER_W].astype(_BF16)
    f2 = jnp.dot(zf, dft_ref[...], preferred_element_type=_F32)
    fc_ref[...] = f2[:, :FOURIER_W].astype(_BF16)
    fs_ref[...] = f2[:, FOURIER_W:].astype(_BF16)

    off = 3 * CONV_W + FOURIER_W
    u = jax.nn.gelu(z[:, off:off + SGU_W])
    v = (_rms(jax.nn.gelu(z[:, off + SGU_W:off + 2 * SGU_W])) * gsgu_ref[...]).astype(_BF16)
    lane = lax.broadcasted_iota(jnp.int32, (CHUNK, LANES), 1)
    first_head = lane < HEAD_DIM
    chunks = []
    for c in range(tile // CHUNK):
        pairs = []
        for k in range(SGU_W // LANES):
            vk = v[c * CHUNK:(c + 1) * CHUNK, k * LANES:(k + 1) * LANES]
            r0 = jnp.dot(wsp_ref[2 * k], vk, preferred_element_type=_F32)
            r1 = jnp.dot(wsp_ref[2 * k + 1], vk, preferred_element_type=_F32)
            pairs.append(jnp.where(first_head, r0, r1))
        chunks.append(jnp.concatenate(pairs, axis=1) + bsp_ref[...])
    sv = jnp.concatenate(chunks, axis=0)
    yc = (_rms(u * sv) * ggrp_ref[:, CONV_W + FOURIER_W:]).astype(_BF16)
    yac_ref[0] = jnp.concatenate([ya, yc], axis=1)


def _mix_in(x, mod_l, gmix, win_bf, convw, gsgu, wsp_bf, bsp_rows, ggrp, dft64, tile):
    b, s, d = x.shape
    nt = s // tile
    hb = tile // SUBLANES
    last_hb = s // SUBLANES - 1
    const2 = lambda i, t: (0, 0)
    return pl.pallas_call(
        functools.partial(_mix_in_kernel, tile=tile),
        grid=(b, nt),
        in_specs=[
            pl.BlockSpec((1, tile, d), lambda i, t: (i, t, 0)),
            pl.BlockSpec((1, SUBLANES, d), lambda i, t: (i, jnp.maximum(t * hb - 1, 0), 0)),
            pl.BlockSpec((1, SUBLANES, d), lambda i, t: (i, jnp.minimum((t + 1) * hb, last_hb), 0)),
            pl.BlockSpec((1, 1, N_MOD * d), lambda i, t: (i, 0, 0)),
            pl.BlockSpec((1, d), const2),
            pl.BlockSpec((d, IN_W), const2),
            pl.BlockSpec((3, CONV_W), const2),
            pl.BlockSpec((1, SGU_W), const2),
            pl.BlockSpec((SGU_HEADS, CHUNK, CHUNK), lambda i, t: (0, 0, 0)),
            pl.BlockSpec((CHUNK, SGU_W), const2),
            pl.BlockSpec((1, d), const2),
            pl.BlockSpec((FOURIER_W, 2 * FOURIER_W), const2),
        ],
        out_specs=[
            pl.BlockSpec((1, tile, CONV_W + SGU_W), lambda i, t: (i, t, 0)),
            pl.BlockSpec((tile, FOURIER_W), lambda i, t: (t, i)),
            pl.BlockSpec((tile, FOURIER_W), lambda i, t: (t, i)),
        ],
        out_shape=[
            jax.ShapeDtypeStruct((b, s, CONV_W + SGU_W), _BF16),
            jax.ShapeDtypeStruct((s, b * FOURIER_W), _BF16),
            jax.ShapeDtypeStruct((s, b * FOURIER_W), _BF16),
        ],
        compiler_params=_params(("arbitrary", "arbitrary")),
        name="mix_in",
    )(x, x, x, mod_l, gmix, win_bf, convw, gsgu, wsp_bf, bsp_rows, ggrp, dft64)


def _seq_dft_kernel(wc_ref, ws_ref, fc_ref, fs_ref, o_ref):
    part = (jnp.dot(wc_ref[...], fc_ref[...], preferred_element_type=_F32)
            + jnp.dot(ws_ref[...], fs_ref[...], preferred_element_type=_F32))

    @pl.when(pl.program_id(2) == 0)
    def _():
        o_ref[...] = part

    @pl.when(pl.program_id(2) > 0)
    def _():
        o_ref[...] += part


def _seq_dft(wc, wsn, fc, fs):
    s = wc.shape[0]
    n = fc.shape[1]
    tm = min(1024, s)
    tn = min(1024, n)
    tk = min(1024, s)
    return pl.pallas_call(
        _seq_dft_kernel,
        grid=(s // tm, n // tn, s // tk),
        in_specs=[
            pl.BlockSpec((tm, tk), lambda i, j, k: (i, k)),
            pl.BlockSpec((tm, tk), lambda i, j, k: (i, k)),
            pl.BlockSpec((tk, tn), lambda i, j, k: (k, j)),
            pl.BlockSpec((tk, tn), lambda i, j, k: (k, j)),
        ],
        out_specs=pl.BlockSpec((tm, tn), lambda i, j, k: (i, j)),
        out_shape=jax.ShapeDtypeStruct((s, n), _F32),
        compiler_params=_params(("arbitrary", "arbitrary", "arbitrary")),
        name="seq_dft",
    )(wc, wsn, fc, fs)


def _mix_out_kernel(x_ref, yac_ref, yf_ref, mod_ref, ggrp_ref, wout_ref, gffn_ref, wr_ref,
                    x1_ref, h2_ref, aff_ref, *, seq_scale):
    d = D_MODEL
    gate_m = mod_ref[0, :, 2 * d:3 * d]
    shift_f = mod_ref[0, :, 3 * d:4 * d]
    scale_f = mod_ref[0, :, 4 * d:5 * d]
    yf = (_rms(yf_ref[...] * seq_scale) * ggrp_ref[:, CONV_W:CONV_W + FOURIER_W]).astype(_BF16)
    yac = yac_ref[0]
    ycat = jnp.concatenate([yac[:, :CONV_W], yf, yac[:, CONV_W:]], axis=1)
    mix = jnp.dot(ycat, wout_ref[...], preferred_element_type=_F32)
    x1 = x_ref[0] + gate_m * mix
    x1_ref[0] = x1
    h2 = (_rms(x1) * gffn_ref[...]) * (1.0 + scale_f) + shift_f
    h2_ref[0] = h2
    h_hi = h2.astype(_BF16)
    h_lo = (h2 - h_hi.astype(_F32)).astype(_BF16)
    by_hi = jnp.dot(h_hi, wr_ref[...], preferred_element_type=_F32)
    logits = (by_hi[:, :N_EXPERTS] + (by_hi[:, N_EXPERTS:]
              + jnp.dot(h_lo, wr_ref[:, :N_EXPERTS], preferred_element_type=_F32)))
    e = jnp.exp(logits - jnp.max(logits, axis=-1, keepdims=True))
    aff_ref[0] = e / jnp.sum(e, axis=-1, keepdims=True)


def _mix_out(x, yac, yf, mod_l, ggrp, wout_bf, gffn, w_router, tile):
    b, s, d = x.shape
    const2 = lambda i, t: (0, 0)
    return pl.pallas_call(
        functools.partial(_mix_out_kernel, seq_scale=1.0 / math.sqrt(s)),
        grid=(b, s // tile),
        in_specs=[
            pl.BlockSpec((1, tile, d), lambda i, t: (i, t, 0)),
            pl.BlockSpec((1, tile, CONV_W + SGU_W), lambda i, t: (i, t, 0)),
            pl.BlockSpec((tile, FOURIER_W), lambda i, t: (t, i)),
            pl.BlockSpec((1, 1, N_MOD * d), lambda i, t: (i, 0, 0)),
            pl.BlockSpec((1, d), const2),
            pl.BlockSpec((d, d), const2),
            pl.BlockSpec((1, d), const2),
            pl.BlockSpec((d, 2 * N_EXPERTS), const2),
        ],
        out_specs=[
            pl.BlockSpec((1, tile, d), lambda i, t: (i, t, 0)),
            pl.BlockSpec((1, tile, d), lambda i, t: (i, t, 0)),
            pl.BlockSpec((1, tile, N_EXPERTS), lambda i, t: (i, t, 0)),
        ],
        out_shape=[
            jax.ShapeDtypeStruct((b, s, d), _F32),
            jax.ShapeDtypeStruct((b, s, d), _F32),
            jax.ShapeDtypeStruct((b, s, N_EXPERTS), _F32),
        ],
        compiler_params=_params(("arbitrary", "arbitrary")),
        name="mix_out",
    )(x, yac, yf, mod_l, ggrp, wout_bf, gffn, w_router)


def _cumsum_mats(rows):
    qi = lax.broadcasted_iota(jnp.int32, (LANES, LANES), 0)
    qj = lax.broadcasted_iota(jnp.int32, (LANES, LANES), 1)
    ri = lax.broadcasted_iota(jnp.int32, (rows, rows), 0)
    rj = lax.broadcasted_iota(jnp.int32, (rows, rows), 1)
    def ones_where(mask):
        return jnp.where(mask, 1.0, 0.0).astype(_BF16)

    lane_excl = ones_where(qi < qj)
    lane_incl = ones_where(qi <= qj)
    row_excl = ones_where(rj < ri)
    row_incl_t = ones_where(ri <= rj)
    return lane_excl, lane_incl, row_excl, row_incl_t


def _route_kernel(aff_ref, idx_ref, gate_ref, *, cap, jb):
    a = aff_ref[0]
    rows = a.shape[0]
    lane_excl, lane_incl, row_excl, row_incl_t = _cumsum_mats(rows)

    def count(mask):
        return jnp.sum(jnp.where(mask, 1.0, 0.0)).astype(jnp.int32)

    def refine(thr_bits, shift, nbits):
        passing = jnp.int32(0)
        for m in range(1, 1 << nbits):
            cand_f = lax.bitcast_convert_type(thr_bits | (jnp.int32(m) << shift), _F32)
            passing += (count(a >= cand_f) >= cap).astype(jnp.int32)
        return thr_bits | (passing << shift)

    thr_bits = lax.fori_loop(0, 10, lambda i, t: refine(t, 27 - 3 * i, 3), jnp.int32(0))
    thr = lax.bitcast_convert_type(thr_bits, _F32)
    above = a > thr
    tied = a == thr
    need = cap - count(above)

    tied_b = jnp.where(tied, 1.0, 0.0).astype(_BF16)
    in_row = jnp.dot(tied_b, lane_excl, preferred_element_type=_F32)
    row_tot = jnp.dot(tied_b, jnp.ones((LANES, LANES), _BF16), preferred_element_type=_F32)
    before_row = jnp.dot(row_excl, row_tot.astype(_BF16), preferred_element_type=_F32)
    tie_rank = (in_row + before_row).astype(jnp.int32)
    sel = above | (tied & (tie_rank < need))
    sel_b = jnp.where(sel, 1.0, 0.0).astype(_BF16)

    row_cnt = lax.dot_general(jnp.ones((SUBLANES, LANES), _BF16), sel_b,
                              (((1,), (1,)), ((), ())), preferred_element_type=_F32)
    incl = jnp.dot(row_cnt.astype(_BF16), row_incl_t, preferred_element_type=_F32)[0:1, :]
    excl = incl - row_cnt[0:1, :]
    a_hi = a.astype(_BF16)
    a_mid = (a - a_hi.astype(_F32)).astype(_BF16)
    a_lo = (a - a_hi.astype(_F32) - a_mid.astype(_F32)).astype(_BF16)
    row_id = lax.broadcasted_iota(jnp.int32, (1, rows), 1).astype(_F32)
    lane_id = lax.broadcasted_iota(jnp.int32, (jb, LANES), 1).astype(_F32)

    for c in range(cap // jb):
        j = (lax.broadcasted_iota(jnp.int32, (jb, 1), 0) + c * jb).astype(_F32)
        hit = (excl <= j) & (j < incl)
        hit_f = jnp.where(hit, 1.0, 0.0)
        r_of_j = jnp.sum(hit_f * row_id, axis=1, keepdims=True)
        target = j + 1.0 - jnp.sum(hit_f * excl, axis=1, keepdims=True)
        row_sel = jnp.dot(hit_f.astype(_BF16), sel_b, preferred_element_type=_F32)
        within = jnp.dot(row_sel.astype(_BF16), lane_incl, preferred_element_type=_F32)
        l_of_j = jnp.sum(jnp.where(within < target, 1.0, 0.0), axis=1, keepdims=True)
        hit_b = hit_f.astype(_BF16)
        a_row = (jnp.dot(hit_b, a_hi, preferred_element_type=_F32)
                 + jnp.dot(hit_b, a_mid, preferred_element_type=_F32)
                 + jnp.dot(hit_b, a_lo, preferred_element_type=_F32))
        gate = jnp.sum(jnp.where(lane_id == l_of_j, a_row, 0.0), axis=1, keepdims=True)
        token = (r_of_j * LANES + l_of_j).astype(jnp.int32)
        idx_ref[0, c * jb:(c + 1) * jb, :] = jnp.broadcast_to(token, (jb, LANES))
        gate_ref[0, c * jb:(c + 1) * jb, :] = jnp.broadcast_to(gate, (jb, LANES))


def _route(aff_t, cap):
    e, rows, _ = aff_t.shape
    jb = min(512, cap)
    return pl.pallas_call(
        functools.partial(_route_kernel, cap=cap, jb=jb),
        grid=(e,),
        in_specs=[pl.BlockSpec((1, rows, LANES), lambda i: (i, 0, 0))],
        out_specs=[
            pl.BlockSpec((1, cap, LANES), lambda i: (i, 0, 0)),
            pl.BlockSpec((1, cap, LANES), lambda i: (i, 0, 0)),
        ],
        out_shape=[
            jax.ShapeDtypeStruct((e, cap, LANES), jnp.int32),
            jax.ShapeDtypeStruct((e, cap, LANES), _F32),
        ],
        compiler_params=_params(("arbitrary",)),
        name="route",
    )(aff_t)


def _experts_kernel(idx_hbm, h2_hbm, xin_hbm, tok_ref, gate_ref, gf_ref, wg_ref, wu_ref, wd_ref,
                    xout_hbm, idx_p, idx_a, idx_b, idx_n, h_a, h_b, xo_a, xo_b, sems,
                    *, rows, nsteps, seq_shift, nbatch, col_chunk):
    del xin_hbm
    e = pl.program_id(0)
    j = pl.program_id(1)
    step = e * nsteps + j
    last_step = pl.num_programs(0) * nsteps - 1
    blk_a = 2 * step
    blk_p = jnp.maximum(blk_a - 1, 0)
    blk_n = jnp.minimum(blk_a + 2, 2 * last_step + 1)
    SEM_IDX, SEM_HA, SEM_HB, SEM_XA, SEM_XB, SEM_SA, SEM_SB = range(7)

    idx_copies = [pltpu.make_async_copy(idx_hbm.at[b], ref, sems.at[SEM_IDX])
                  for b, ref in ((blk_p, idx_p), (blk_a, idx_a), (blk_a + 1, idx_b), (blk_n, idx_n))]
    for c in idx_copies:
        c.start()
    for c in idx_copies:
        c.wait()

    def row_copy(idx_ref, r, src, dst, sem, kind):
        tok = idx_ref[0, r]
        if kind == "gather":
            return pltpu.make_async_copy(src.at[pl.ds(tok, 1)], dst.at[pl.ds(r, 1)], sems.at[sem])
        return pltpu.make_async_copy(src.at[pl.ds(r, 1)], dst.at[pl.ds(tok, 1)], sems.at[sem])

    def start_all(idx_ref, src, dst, sem, kind):
        def body(r, carry):
            row_copy(idx_ref, r, src, dst, sem, kind).start()
            return carry
        lax.fori_loop(0, rows, body, 0)

    def wait_rows(hbm, buf, sem, to_hbm=False):
        if to_hbm:
            pltpu.make_async_copy(buf, hbm.at[pl.ds(0, rows)], sems.at[sem]).wait()
        else:
            pltpu.make_async_copy(hbm.at[pl.ds(0, rows)], buf, sems.at[sem]).wait()

    def block(part, h_buf, sem_h, xo_buf, sem_x, streams):
        wait_rows(h2_hbm, h_buf, sem_h)
        todo = [functools.partial(row_copy, stream[0], r, *stream[1:])
                for r in range(rows) for stream in streams]
        nch = D_MODEL // col_chunk
        per_call = -(-len(todo) // (3 * nch))

        def start_some():
            for k, make in enumerate(todo[:per_call]):
                make().start(priority=k % 2)
            del todo[:per_call]

        x = h_buf[...].astype(_BF16)
        hid = []
        for c in range(nch):
            cs = slice(c * col_chunk, (c + 1) * col_chunk)
            hg = jnp.dot(x, wg_ref[0, 0, :, cs].astype(_BF16), preferred_element_type=_F32)
            start_some()
            hu = jnp.dot(x, wu_ref[0, 0, :, cs].astype(_BF16), preferred_element_type=_F32)
            start_some()
            hid.append((hg * jax.nn.sigmoid(hg) * hu).astype(_BF16))
        hid = jnp.concatenate(hid, axis=1)
        rs = slice(part * rows, (part + 1) * rows)
        seq_of_row = tok_ref[rs, :] >> seq_shift
        gate = gate_ref[rs, :]
        upd = []
        for c in range(nch):
            cs = slice(c * col_chunk, (c + 1) * col_chunk)
            y = jnp.dot(hid, wd_ref[0, 0, :, cs].astype(_BF16), preferred_element_type=_F32)
            start_some()
            seq_c = jnp.concatenate([seq_of_row] * (col_chunk // LANES), axis=1)
            gf = jnp.zeros_like(y)
            for b in range(nbatch):
                gf = jnp.where(seq_c == b, gf_ref[b:b + 1, cs], gf)
            upd.append(gf * (y * jnp.concatenate([gate] * (col_chunk // LANES), axis=1)))
        assert not todo
        wait_rows(xout_hbm, xo_buf, sem_x)
        xo_buf[...] = xo_buf[...] + jnp.concatenate(upd, axis=1)

    gather_xa = (idx_a, xout_hbm, xo_a, SEM_XA, "gather")
    gather_xb = (idx_b, xout_hbm, xo_b, SEM_XB, "gather")
    gather_hb = (idx_b, h2_hbm, h_b, SEM_HB, "gather")
    gather_hn = (idx_n, h2_hbm, h_a, SEM_HA, "gather")
    scatter_p = (idx_p, xo_b, xout_hbm, SEM_SB, "scatter")
    scatter_a = (idx_a, xo_a, xout_hbm, SEM_SA, "scatter")
    scatter_b = (idx_b, xo_b, xout_hbm, SEM_SB, "scatter")

    @pl.when(step == 0)
    def _():
        start_all(idx_a, h2_hbm, h_a, SEM_HA, "gather")

    @pl.when(step > 0)
    def _():
        wait_rows(xout_hbm, xo_a, SEM_SA, to_hbm=True)

    @pl.when(j > 0)
    def _():
        block(0, h_a, SEM_HA, xo_a, SEM_XA, [gather_xa, scatter_p, gather_hb])
        wait_rows(xout_hbm, xo_b, SEM_SB, to_hbm=True)

    @pl.when(j == 0)
    def _():
        @pl.when(step > 0)
        def _():
            start_all(*scatter_p)
            wait_rows(xout_hbm, xo_b, SEM_SB, to_hbm=True)
        block(0, h_a, SEM_HA, xo_a, SEM_XA, [gather_xa, gather_hb])

    block(1, h_b, SEM_HB, xo_b, SEM_XB, [gather_xb, scatter_a, gather_hn])

    @pl.when(step == last_step)
    def _():
        start_all(*scatter_b)
        wait_rows(h2_hbm, h_a, SEM_HA)
        wait_rows(xout_hbm, xo_a, SEM_SA, to_hbm=True)
        wait_rows(xout_hbm, xo_b, SEM_SB, to_hbm=True)


def _experts(idx_blocks, h2, x1, tok_rep, gate_rep, gate_f, w_gate, w_up, w_down, layer, seq, rows):
    n, d = x1.shape
    e = w_gate.shape[1]
    cap = tok_rep.shape[0] // e
    nsteps = cap // (2 * rows)
    nbatch = gate_f.shape[0]
    seq_shift = seq.bit_length() - 1
    assert 1 << seq_shift == seq and nsteps * 2 * rows == cap
    wspec = pl.BlockSpec((1, 1, d, d), lambda i, j: (layer, i, 0, 0))
    any_spec = pl.BlockSpec(memory_space=pl.ANY)
    return pl.pallas_call(
        functools.partial(_experts_kernel, rows=rows, nsteps=nsteps, seq_shift=seq_shift, nbatch=nbatch,
                          col_chunk=EXPERT_COL_CHUNK),
        grid=(e, nsteps),
        in_specs=[
            any_spec, any_spec, any_spec,
            pl.BlockSpec((2 * rows, LANES), lambda i, j: (i * nsteps + j, 0)),
            pl.BlockSpec((2 * rows, LANES), lambda i, j: (i * nsteps + j, 0)),
            pl.BlockSpec((nbatch, d), lambda i, j: (0, 0)),
            wspec, wspec, wspec,
        ],
        out_specs=any_spec,
        out_shape=jax.ShapeDtypeStruct((n, d), _F32),
        scratch_shapes=[
            pltpu.SMEM((1, rows), jnp.int32),
            pltpu.SMEM((1, rows), jnp.int32),
            pltpu.SMEM((1, rows), jnp.int32),
            pltpu.SMEM((1, rows), jnp.int32),
            pltpu.VMEM((rows, d), _F32),
            pltpu.VMEM((rows, d), _F32),
            pltpu.VMEM((rows, d), _F32),
            pltpu.VMEM((rows, d), _F32),
            pltpu.SemaphoreType.DMA((7,)),
        ],
        input_output_aliases={2: 0},
        compiler_params=_params(("arbitrary", "arbitrary")),
        name="experts",
    )(idx_blocks, h2, x1, tok_rep, gate_rep, gate_f, w_gate, w_up, w_down)


def _final_kernel(x_ref, g_ref, o_ref):
    o_ref[0] = _rms(x_ref[0]) * g_ref[...]


def _final_norm(x, g, tile):
    b, s, d = x.shape
    return pl.pallas_call(
        _final_kernel,
        grid=(b, s // tile),
        in_specs=[pl.BlockSpec((1, tile, d), lambda i, t: (i, t, 0)),
                  pl.BlockSpec((1, d), lambda i, t: (0, 0))],
        out_specs=pl.BlockSpec((1, tile, d), lambda i, t: (i, t, 0)),
        out_shape=jax.ShapeDtypeStruct((b, s, d), _F32),
        compiler_params=_params(("arbitrary", "arbitrary")),
        name="final_norm",
    )(x, g)


def _channel_dft_table():
    k = jnp.arange(HEAD_DIM, dtype=jnp.int32)
    ang = ((k[:, None] * k[None, :]) % HEAD_DIM).astype(_F32) * (2.0 * math.pi / HEAD_DIM)
    groups = FOURIER_W // HEAD_DIM
    eye = jnp.eye(groups, dtype=_F32)
    c = jnp.kron(eye, jnp.cos(ang)) / math.sqrt(HEAD_DIM)
    s = jnp.kron(eye, jnp.sin(ang)) / math.sqrt(HEAD_DIM)
    return jnp.concatenate([c, s], axis=1).astype(_BF16)


def _split_hi_lo(w):
    hi = w.astype(_BF16)
    lo = (w - hi.astype(_F32)).astype(_BF16)
    return jnp.concatenate([hi, lo], axis=-1)


def _seq_dft_tables(s):
    radix = 64
    assert s % radix == 0
    k = jnp.arange(s, dtype=jnp.int32)[:, None]
    t1 = jnp.arange(s // radix, dtype=jnp.int32)[None, :]
    t0 = jnp.arange(radix, dtype=jnp.int32)[None, :]
    ang_a = ((k * t1 * radix) % s).astype(_F32) * (2.0 * math.pi / s)
    ang_b = ((k * t0) % s).astype(_F32) * (2.0 * math.pi / s)
    ca, sa = jnp.cos(ang_a)[:, :, None], jnp.sin(ang_a)[:, :, None]
    cb, sb = jnp.cos(ang_b)[:, None, :], jnp.sin(ang_b)[:, None, :]
    cos = (ca * cb - sa * sb).reshape(s, s)
    nsin = (-(sa * cb + ca * sb)).reshape(s, s)
    return cos.astype(_BF16), nsin.astype(_BF16)


def _encoder(x, mod, p, dft64, tile, erows):
    b, s, d = x.shape
    n = b * s
    cap = CAPACITY_FACTOR * n // N_EXPERTS
    wc, wsn = _seq_dft_tables(s)
    depth = p["w_in"].shape[0]
    for l in range(depth):
        mod_l = mod[l][:, None, :]
        yac, fc, fs = _mix_in(x, mod_l, p["g_mix"][l][None], p["w_in_bf"][l], p["conv_w"][l],
                              p["g_sgu"][l][None], p["w_sp_bf"][l], p["b_sp_rows"][l],
                              p["g_grp"][l][None], dft64, tile)
        yf = _seq_dft(wc, wsn, fc, fs)
        x1, h2, aff = _mix_out(x, yac, yf, mod_l, p["g_grp"][l][None], p["w_out_bf"][l],
                               p["g_ffn"][l][None], p["w_router_hl"][l], tile)
        aff_t = aff.reshape(n, N_EXPERTS).T.reshape(N_EXPERTS, n // LANES, LANES)
        tok_rep, gate_rep = _route(aff_t, cap)
        idx_blocks = tok_rep[:, :, 0].reshape(N_EXPERTS * cap // erows, 1, erows)
        gate_f = mod_l[:, 0, 5 * d:6 * d]
        x = _experts(idx_blocks, h2.reshape(n, d), x1.reshape(n, d),
                     tok_rep.reshape(N_EXPERTS * cap, LANES), gate_rep.reshape(N_EXPERTS * cap, LANES),
                     gate_f, p["w_gate"], p["w_up"], p["w_down"], l, s, erows).reshape(b, s, d)
    return _final_norm(x, p["g_final"][None], tile)


def kernel(x_prompt, x_sample, c_prompt, c_sample, w_ada, b_ada, g_mix, w_in, conv_w, g_sgu, w_spatial,
           b_spatial, g_grp, w_out, g_ffn, w_router, w_gate, w_up, w_down, g_final):
    nb = x_prompt.shape[0]
    mod = _ada(jnp.concatenate([c_prompt, c_sample], axis=0), w_ada, b_ada)
    p = dict(
        g_mix=g_mix, conv_w=conv_w, g_sgu=g_sgu, g_grp=g_grp, g_ffn=g_ffn, w_router_hl=_split_hi_lo(w_router),
        w_gate=w_gate, w_up=w_up, w_down=w_down, g_final=g_final, w_in=w_in,
        w_in_bf=w_in.astype(_BF16), w_out_bf=w_out.astype(_BF16), w_sp_bf=w_spatial.astype(_BF16),
        b_sp_rows=jnp.repeat(jnp.swapaxes(b_spatial, 1, 2), HEAD_DIM, axis=2),
    )
    dft64 = _channel_dft_table()
    tile = min(512, x_sample.shape[1])
    erows = 512
    y_prompt = _encoder(x_prompt, mod[:, :nb], p, dft64, tile, erows)
    y_sample = _encoder(x_sample, mod[:, nb:], p, dft64, tile, erows)
    return (y_prompt, y_sample)
```

```python
import functools
import math

import jax
import jax.numpy as jnp
from jax import lax
from jax.experimental import pallas as pl
from jax.experimental.pallas import tpu as pltpu

D_MODEL = 1024
HEAD_DIM = 64
CONV_W = 384
FOURIER_W = 256
SGU_W = 384
SGU_HEADS = SGU_W // HEAD_DIM
CHUNK = 128
IN_W = 3 * CONV_W + FOURIER_W + 2 * SGU_W
N_EXPERTS = 16
CAPACITY_FACTOR = 2
N_MOD = 6
EPS = 1e-6

LANES = 128
SUBLANES = 8
VMEM_LIMIT_BYTES = 52 * 1024 * 1024
MXU_COLS = 256
EXPERT_COL_CHUNK = MXU_COLS

_HI = lax.Precision.HIGHEST
_BF16 = jnp.bfloat16
_F32 = jnp.float32


def _rms(x):
    return x * lax.rsqrt(jnp.mean(x * x, axis=-1, keepdims=True) + EPS)


def _params(sem, vmem=VMEM_LIMIT_BYTES):
    return pltpu.CompilerParams(dimension_semantics=sem, vmem_limit_bytes=vmem)


def _ada_kernel(c_ref, w_ref, b_ref, o_ref):
    c = c_ref[...]
    a = c * jax.nn.sigmoid(c)
    o_ref[0] = jnp.dot(a, w_ref[0], precision=_HI, preferred_element_type=_F32) + b_ref[0]


def _ada(c_all, w_ada, b_ada):
    depth, d, m = w_ada.shape
    nb = c_all.shape[0]
    tn = 1536
    return pl.pallas_call(
        _ada_kernel,
        grid=(depth, m // tn),
        in_specs=[
            pl.BlockSpec((nb, d), lambda l, j: (0, 0)),
            pl.BlockSpec((1, d, tn), lambda l, j: (l, 0, j)),
            pl.BlockSpec((1, 1, tn), lambda l, j: (l, 0, j)),
        ],
        out_specs=pl.BlockSpec((1, nb, tn), lambda l, j: (l, 0, j)),
        out_shape=jax.ShapeDtypeStruct((depth, nb, m), _F32),
        compiler_params=_params(("arbitrary", "arbitrary")),
        name="ada",
    )(c_all, w_ada, b_ada.reshape(depth, 1, m))


def _mix_in_kernel(x_ref, xp_ref, xn_ref, mod_ref, gmix_ref, win_ref, convw_ref, gsgu_ref,
                   wsp_ref, bsp_ref, ggrp_ref, dft_ref, yac_ref, fc_ref, fs_ref, *, tile):
    t = pl.program_id(1)
    nt = pl.num_programs(1)
    shift = mod_ref[0, :, 0:D_MODEL]
    scale = mod_ref[0, :, D_MODEL:2 * D_MODEL]
    gmix = gmix_ref[...]

    def modulated(x):
        return ((_rms(x) * gmix) * (1.0 + scale) + shift).astype(_BF16)

    h = modulated(x_ref[0])
    z = jnp.dot(h, win_ref[...], preferred_element_type=_F32)

    def halo(xh_ref):
        zh = jnp.dot(modulated(xh_ref[0]), win_ref[:, CONV_W:3 * CONV_W],
                     preferred_element_type=_F32)
        return zh[:, :CONV_W] * zh[:, CONV_W:]

    a_before = jnp.where(t > 0, halo(xp_ref)[SUBLANES - 1:SUBLANES, :], 0.0)
    a_after = jnp.where(t < nt - 1, halo(xn_ref)[0:1, :], 0.0)

    zb = z[:, 0:CONV_W]
    a = z[:, CONV_W:2 * CONV_W] * z[:, 2 * CONV_W:3 * CONV_W]
    row = lax.broadcasted_iota(jnp.int32, (tile, CONV_W), 0)
    a_prev = jnp.where(row == 0, a_before, pltpu.roll(a, 1, axis=0))
    a_next = jnp.where(row == tile - 1, a_after, pltpu.roll(a, tile - 1, axis=0))
    cw = convw_ref[...]
    conv = zb * (cw[0:1, :] * a_prev + cw[1:2, :] * a + cw[2:3, :] * a_next)
    ya = (_rms(conv) * ggrp_ref[:, 0:CONV_W]).astype(_BF16)

    zf = z[:, 3 * CONV_W:3 * CONV_W + FOURIER_W].astype(_BF16)
    f2 = jnp.dot(zf, dft_ref[...], preferred_element_type=_F32)
    fc_ref[...] = f2[:, :FOURIER_W].astype(_BF16)
    fs_ref[...] = f2[:, FOURIER_W:].astype(_BF16)

    off = 3 * CONV_W + FOURIER_W
    u = jax.nn.gelu(z[:, off:off + SGU_W])
    v = (_rms(jax.nn.gelu(z[:, off + SGU_W:off + 2 * SGU_W])) * gsgu_ref[...]).astype(_BF16)
    lane = lax.broadcasted_iota(jnp.int32, (CHUNK, LANES), 1)
    first_head = lane < HEAD_DIM
    chunks = []
    for c in range(tile // CHUNK):
        pairs = []
        for k in range(SGU_W // LANES):
            vk = v[c * CHUNK:(c + 1) * CHUNK, k * LANES:(k + 1) * LANES]
            r0 = jnp.dot(wsp_ref[2 * k], vk, preferred_element_type=_F32)
            r1 = jnp.dot(wsp_ref[2 * k + 1], vk, preferred_element_type=_F32)
            pairs.append(jnp.where(first_head, r0, r1))
        chunks.append(jnp.concatenate(pairs, axis=1) + bsp_ref[...])
    sv = jnp.concatenate(chunks, axis=0)
    yc = (_rms(u * sv) * ggrp_ref[:, CONV_W + FOURIER_W:]).astype(_BF16)
    yac_ref[0] = jnp.concatenate([ya, yc], axis=1)


def _mix_in(x, mod_l, gmix, win_bf, convw, gsgu, wsp_bf, bsp_rows, ggrp, dft64, tile):
    b, s, d = x.shape[0], x.shape[1], D_MODEL
    nt = s // tile
    hb = tile // SUBLANES
    last_hb = s // SUBLANES - 1
    const2 = lambda i, t: (0, 0)
    return pl.pallas_call(
        functools.partial(_mix_in_kernel, tile=tile),
        grid=(b, nt),
        in_specs=[
            pl.BlockSpec((1, tile, d), lambda i, t: (i, t, 0)),
            pl.BlockSpec((1, SUBLANES, d), lambda i, t: (i, jnp.maximum(t * hb - 1, 0), 0)),
            pl.BlockSpec((1, SUBLANES, d), lambda i, t: (i, jnp.minimum((t + 1) * hb, last_hb), 0)),
            pl.BlockSpec((1, 1, N_MOD * d), lambda i, t: (i, 0, 0)),
            pl.BlockSpec((1, d), const2),
            pl.BlockSpec((d, IN_W), const2),
            pl.BlockSpec((3, CONV_W), const2),
            pl.BlockSpec((1, SGU_W), const2),
            pl.BlockSpec((SGU_HEADS, CHUNK, CHUNK), lambda i, t: (0, 0, 0)),
            pl.BlockSpec((CHUNK, SGU_W), const2),
            pl.BlockSpec((1, d), const2),
            pl.BlockSpec((FOURIER_W, 2 * FOURIER_W), const2),
        ],
        out_specs=[
            pl.BlockSpec((1, tile, CONV_W + SGU_W), lambda i, t: (i, t, 0)),
            pl.BlockSpec((tile, FOURIER_W), lambda i, t: (t, i)),
            pl.BlockSpec((tile, FOURIER_W), lambda i, t: (t, i)),
        ],
        out_shape=[
            jax.ShapeDtypeStruct((b, s, CONV_W + SGU_W), _BF16),
            jax.ShapeDtypeStruct((s, b * FOURIER_W), _BF16),
            jax.ShapeDtypeStruct((s, b * FOURIER_W), _BF16),
        ],
        compiler_params=_params(("arbitrary", "arbitrary")),
        name="mix_in",
    )(x, x, x, mod_l, gmix, win_bf, convw, gsgu, wsp_bf, bsp_rows, ggrp, dft64)


def _seq_dft_kernel(wc_ref, ws_ref, fc_ref, fs_ref, o_ref):
    part = (jnp.dot(wc_ref[...], fc_ref[...], preferred_element_type=_F32)
            + jnp.dot(ws_ref[...], fs_ref[...], preferred_element_type=_F32))

    @pl.when(pl.program_id(2) == 0)
    def _():
        o_ref[...] = part

    @pl.when(pl.program_id(2) > 0)
    def _():
        o_ref[...] += part


def _seq_dft(wc, wsn, fc, fs):
    s = wc.shape[0]
    n = fc.shape[1]
    tm = min(1024, s)
    tn = min(1024, n)
    tk = min(1024, s)
    return pl.pallas_call(
        _seq_dft_kernel,
        grid=(s // tm, n // tn, s // tk),
        in_specs=[
            pl.BlockSpec((tm, tk), lambda i, j, k: (i, k)),
            pl.BlockSpec((tm, tk), lambda i, j, k: (i, k)),
            pl.BlockSpec((tk, tn), lambda i, j, k: (k, j)),
            pl.BlockSpec((tk, tn), lambda i, j, k: (k, j)),
        ],
        out_specs=pl.BlockSpec((tm, tn), lambda i, j, k: (i, j)),
        out_shape=jax.ShapeDtypeStruct((s, n), _F32),
        compiler_params=_params(("arbitrary", "arbitrary", "arbitrary")),
        name="seq_dft",
    )(wc, wsn, fc, fs)


def _mix_out_kernel(x_ref, yac_ref, yf_ref, mod_ref, ggrp_ref, wout_ref, gffn_ref, wr_ref,
                    xh_ref, aff_ref, *, seq_scale):
    d = D_MODEL
    gate_m = mod_ref[0, :, 2 * d:3 * d]
    shift_f = mod_ref[0, :, 3 * d:4 * d]
    scale_f = mod_ref[0, :, 4 * d:5 * d]
    yf = (_rms(yf_ref[...] * seq_scale) * ggrp_ref[:, CONV_W:CONV_W + FOURIER_W]).astype(_BF16)
    yac = yac_ref[0]
    ycat = jnp.concatenate([yac[:, :CONV_W], yf, yac[:, CONV_W:]], axis=1)
    mix = jnp.dot(ycat, wout_ref[...], preferred_element_type=_F32)
    x1 = x_ref[0] + gate_m * mix
    h2 = (_rms(x1) * gffn_ref[...]) * (1.0 + scale_f) + shift_f
    xh_ref[0, :, :d] = x1
    xh_ref[0, :, d:] = h2
    h_hi = h2.astype(_BF16)
    h_lo = (h2 - h_hi.astype(_F32)).astype(_BF16)
    by_hi = jnp.dot(h_hi, wr_ref[...], preferred_element_type=_F32)
    logits = (by_hi[:, :N_EXPERTS] + (by_hi[:, N_EXPERTS:]
              + jnp.dot(h_lo, wr_ref[:, :N_EXPERTS], preferred_element_type=_F32)))
    e = jnp.exp(logits - jnp.max(logits, axis=-1, keepdims=True))
    aff_ref[0] = e / jnp.sum(e, axis=-1, keepdims=True)


def _mix_out(x, yac, yf, mod_l, ggrp, wout_bf, gffn, w_router, tile):
    b, s, d = x.shape[0], x.shape[1], D_MODEL
    const2 = lambda i, t: (0, 0)
    return pl.pallas_call(
        functools.partial(_mix_out_kernel, seq_scale=1.0 / math.sqrt(s)),
        grid=(b, s // tile),
        in_specs=[
            pl.BlockSpec((1, tile, d), lambda i, t: (i, t, 0)),
            pl.BlockSpec((1, tile, CONV_W + SGU_W), lambda i, t: (i, t, 0)),
            pl.BlockSpec((tile, FOURIER_W), lambda i, t: (t, i)),
            pl.BlockSpec((1, 1, N_MOD * d), lambda i, t: (i, 0, 0)),
            pl.BlockSpec((1, d), const2),
            pl.BlockSpec((d, d), const2),
            pl.BlockSpec((1, d), const2),
            pl.BlockSpec((d, 2 * N_EXPERTS), const2),
        ],
        out_specs=[
            pl.BlockSpec((1, tile, 2 * d), lambda i, t: (i, t, 0)),
            pl.BlockSpec((1, tile, N_EXPERTS), lambda i, t: (i, t, 0)),
        ],
        out_shape=[
            jax.ShapeDtypeStruct((b, s, 2 * d), _F32),
            jax.ShapeDtypeStruct((b, s, N_EXPERTS), _F32),
        ],
        compiler_params=_params(("arbitrary", "arbitrary")),
        name="mix_out",
    )(x, yac, yf, mod_l, ggrp, wout_bf, gffn, w_router)


def _cumsum_mats(rows):
    qi = lax.broadcasted_iota(jnp.int32, (LANES, LANES), 0)
    qj = lax.broadcasted_iota(jnp.int32, (LANES, LANES), 1)
    ri = lax.broadcasted_iota(jnp.int32, (rows, rows), 0)
    rj = lax.broadcasted_iota(jnp.int32, (rows, rows), 1)
    def ones_where(mask):
        return jnp.where(mask, 1.0, 0.0).astype(_BF16)

    lane_excl = ones_where(qi < qj)
    lane_incl = ones_where(qi <= qj)
    row_excl = ones_where(rj < ri)
    row_incl_t = ones_where(ri <= rj)
    return lane_excl, lane_incl, row_excl, row_incl_t


def _route_kernel(aff_ref, idx_ref, gate_ref, *, cap, jb):
    a = aff_ref[0]
    rows = a.shape[0]
    lane_excl, lane_incl, row_excl, row_incl_t = _cumsum_mats(rows)

    def count(mask):
        return jnp.sum(jnp.where(mask, 1.0, 0.0)).astype(jnp.int32)

    def refine(thr_bits, shift, nbits):
        passing = jnp.int32(0)
        for m in range(1, 1 << nbits):
            cand_f = lax.bitcast_convert_type(thr_bits | (jnp.int32(m) << shift), _F32)
            passing += (count(a >= cand_f) >= cap).astype(jnp.int32)
        return thr_bits | (passing << shift)

    thr_bits = lax.fori_loop(0, 10, lambda i, t: refine(t, 27 - 3 * i, 3), jnp.int32(0))
    thr = lax.bitcast_convert_type(thr_bits, _F32)
    above = a > thr
    tied = a == thr
    need = cap - count(above)

    tied_b = jnp.where(tied, 1.0, 0.0).astype(_BF16)
    in_row = jnp.dot(tied_b, lane_excl, preferred_element_type=_F32)
    row_tot = jnp.dot(tied_b, jnp.ones((LANES, LANES), _BF16), preferred_element_type=_F32)
    before_row = jnp.dot(row_excl, row_tot.astype(_BF16), preferred_element_type=_F32)
    tie_rank = (in_row + before_row).astype(jnp.int32)
    sel = above | (tied & (tie_rank < need))
    sel_b = jnp.where(sel, 1.0, 0.0).astype(_BF16)

    row_cnt = lax.dot_general(jnp.ones((SUBLANES, LANES), _BF16), sel_b,
                              (((1,), (1,)), ((), ())), preferred_element_type=_F32)
    incl = jnp.dot(row_cnt.astype(_BF16), row_incl_t, preferred_element_type=_F32)[0:1, :]
    excl = incl - row_cnt[0:1, :]
    a_hi = a.astype(_BF16)
    a_mid = (a - a_hi.astype(_F32)).astype(_BF16)
    a_lo = (a - a_hi.astype(_F32) - a_mid.astype(_F32)).astype(_BF16)
    row_id = lax.broadcasted_iota(jnp.int32, (1, rows), 1).astype(_F32)
    lane_id = lax.broadcasted_iota(jnp.int32, (jb, LANES), 1).astype(_F32)

    for c in range(cap // jb):
        j = (lax.broadcasted_iota(jnp.int32, (jb, 1), 0) + c * jb).astype(_F32)
        hit = (excl <= j) & (j < incl)
        hit_f = jnp.where(hit, 1.0, 0.0)
        r_of_j = jnp.sum(hit_f * row_id, axis=1, keepdims=True)
        target = j + 1.0 - jnp.sum(hit_f * excl, axis=1, keepdims=True)
        row_sel = jnp.dot(hit_f.astype(_BF16), sel_b, preferred_element_type=_F32)
        within = jnp.dot(row_sel.astype(_BF16), lane_incl, preferred_element_type=_F32)
        l_of_j = jnp.sum(jnp.where(within < target, 1.0, 0.0), axis=1, keepdims=True)
        hit_b = hit_f.astype(_BF16)
        a_row = (jnp.dot(hit_b, a_hi, preferred_element_type=_F32)
                 + jnp.dot(hit_b, a_mid, preferred_element_type=_F32)
                 + jnp.dot(hit_b, a_lo, preferred_element_type=_F32))
        gate = jnp.sum(jnp.where(lane_id == l_of_j, a_row, 0.0), axis=1, keepdims=True)
        token = (r_of_j * LANES + l_of_j).astype(jnp.int32)
        idx_ref[0, c * jb:(c + 1) * jb, :] = jnp.broadcast_to(token, (jb, LANES))
        gate_ref[0, c * jb:(c + 1) * jb, :] = jnp.broadcast_to(gate, (jb, LANES))


def _route(aff_t, cap):
    e, rows, _ = aff_t.shape
    jb = min(512, cap)
    return pl.pallas_call(
        functools.partial(_route_kernel, cap=cap, jb=jb),
        grid=(e,),
        in_specs=[pl.BlockSpec((1, rows, LANES), lambda i: (i, 0, 0))],
        out_specs=[
            pl.BlockSpec((1, cap, LANES), lambda i: (i, 0, 0)),
            pl.BlockSpec((1, cap, LANES), lambda i: (i, 0, 0)),
        ],
        out_shape=[
            jax.ShapeDtypeStruct((e, cap, LANES), jnp.int32),
            jax.ShapeDtypeStruct((e, cap, LANES), _F32),
        ],
        compiler_params=_params(("arbitrary",)),
        name="route",
    )(aff_t)


def _experts_kernel(idx_hbm, xin_hbm, tok_ref, gate_ref, gf_ref, wg_ref, wu_ref, wd_ref,
                    xh_hbm, idx_p, idx_a, idx_b, idx_n, idx_m, prev_first, buf_a, buf_b, wb_a, wb_b, sems,
                    *, rows, nsteps, seq_shift, nbatch, col_chunk):
    del xin_hbm
    d = D_MODEL
    e = pl.program_id(0)
    j = pl.program_id(1)
    step = e * nsteps + j
    last_step = pl.num_programs(0) * nsteps - 1
    blk_a = 2 * step
    blk_p = jnp.maximum(blk_a - 1, 0)
    last_blk = 2 * last_step + 1
    blk_n = jnp.minimum(blk_a + 2, last_blk)
    blk_m = jnp.minimum(blk_a + 3, last_blk)
    SEM_IDX, SEM_GA, SEM_GB, SEM_SA, SEM_SB, SEM_FIX = range(6)

    idx_copies = [pltpu.make_async_copy(idx_hbm.at[b], ref, sems.at[SEM_IDX])
                  for b, ref in ((blk_p, idx_p), (blk_a, idx_a), (blk_a + 1, idx_b), (blk_n, idx_n),
                                 (blk_m, idx_m))]
    for c in idx_copies:
        c.start()
    for c in idx_copies:
        c.wait()

    res_cols = pl.ds(0, d)
    moe_cols = pl.ds(d, d)

    def row_copy(idx_ref, r, buf, sem, kind):
        tok = idx_ref[0, r]
        if kind == "store":
            return pltpu.make_async_copy(buf.at[pl.ds(r, 1)], xh_hbm.at[pl.ds(tok, 1), res_cols], sems.at[sem])
        if kind == "fetch":
            return pltpu.make_async_copy(xh_hbm.at[pl.ds(tok, 1)], buf.at[pl.ds(r, 1)], sems.at[sem])
        cols = moe_cols if kind == "fetch_moe" else res_cols
        return pltpu.make_async_copy(xh_hbm.at[pl.ds(tok, 1), cols], buf.at[pl.ds(r, 1), cols], sems.at[sem])

    def start_all(idx_ref, buf, sem, kind):
        def body(r, carry):
            row_copy(idx_ref, r, buf, sem, kind).start()
            return carry
        lax.fori_loop(0, rows, body, 0)

    def wait_all(buf, sem, kind):
        all_rows = pl.ds(0, rows)
        if kind == "store":
            pltpu.make_async_copy(buf, xh_hbm.at[all_rows, res_cols], sems.at[sem]).wait()
        elif kind == "fetch":
            pltpu.make_async_copy(xh_hbm.at[all_rows], buf, sems.at[sem]).wait()
        else:
            cols = moe_cols if kind == "fetch_moe" else res_cols
            pltpu.make_async_copy(xh_hbm.at[all_rows, cols], buf.at[:, cols], sems.at[sem]).wait()

    def block(part, buf, sem_g, wb, streams, fetched=False):
        if not fetched:
            wait_all(buf, sem_g, "fetch")
        todo = [functools.partial(row_copy, stream[0], r, *stream[1:])
                for r in range(rows) for stream in streams]
        nch = D_MODEL // col_chunk
        per_call = -(-len(todo) // (3 * nch))

        def start_some():
            for k, make in enumerate(todo[:per_call]):
                make().start(priority=k % 2)
            del todo[:per_call]

        x = buf[:, d:].astype(_BF16)
        hid = []
        for c in range(nch):
            cs = slice(c * col_chunk, (c + 1) * col_chunk)
            hg = jnp.dot(x, wg_ref[0, 0, :, cs].astype(_BF16), preferred_element_type=_F32)
            start_some()
            hu = jnp.dot(x, wu_ref[0, 0, :, cs].astype(_BF16), preferred_element_type=_F32)
            start_some()
            hid.append((hg * jax.nn.sigmoid(hg) * hu).astype(_BF16))
        hid = jnp.concatenate(hid, axis=1)
        rs = slice(part * rows, (part + 1) * rows)
        seq_of_row = tok_ref[rs, :] >> seq_shift
        gate = gate_ref[rs, :]
        upd = []
        for c in range(nch):
            cs = slice(c * col_chunk, (c + 1) * col_chunk)
            y = jnp.dot(hid, wd_ref[0, 0, :, cs].astype(_BF16), preferred_element_type=_F32)
            start_some()
            seq_c = jnp.concatenate([seq_of_row] * (col_chunk // LANES), axis=1)
            gf = jnp.zeros_like(y)
            for b in range(nbatch):
                gf = jnp.where(seq_c == b, gf_ref[b:b + 1, cs], gf)
            upd.append(gf * (y * jnp.concatenate([gate] * (col_chunk // LANES), axis=1)))
        assert not todo
        wb[...] = buf[:, :d] + jnp.concatenate(upd, axis=1)

    fetch_b = (idx_b, buf_b, SEM_GB, "fetch")
    fetch_n = (idx_n, buf_a, SEM_GA, "fetch")
    fetch_n_moe = (idx_n, buf_a, SEM_GA, "fetch_moe")
    store_p = (idx_p, wb_b, SEM_SB, "store")
    store_a = (idx_a, wb_a, SEM_SA, "store")
    store_b = (idx_b, wb_b, SEM_SB, "store")

    @pl.when(step == 0)
    def _():
        start_all(idx_a, buf_a, SEM_GA, "fetch")
        prev_first[0] = 0

    @pl.when(step > 0)
    def _():
        wait_all(wb_a, SEM_SA, "store")

    def overlaps(first, last, other_first, other_last):
        return jnp.logical_not((other_last < first) | (last < other_first))

    serial = (step == 0) | overlaps(idx_a[0, 0], idx_b[0, rows - 1], prev_first[0], idx_p[0, rows - 1])
    next_serial = overlaps(idx_n[0, 0], idx_m[0, rows - 1], idx_a[0, 0], idx_b[0, rows - 1])
    prev_first[0] = idx_a[0, 0]

    @pl.when(serial)
    def _():
        @pl.when(step == 0)
        def _():
            wait_all(buf_a, SEM_GA, "fetch")

        @pl.when(step > 0)
        def _():
            start_all(*store_p)
            wait_all(wb_b, SEM_SB, "store")
            wait_all(buf_a, SEM_GA, "fetch_moe")
            start_all(idx_a, buf_a, SEM_FIX, "fetch_res")
            wait_all(buf_a, SEM_FIX, "fetch_res")
        block(0, buf_a, SEM_GA, wb_a, [fetch_b], fetched=True)

    @pl.when(jnp.logical_not(serial))
    def _():
        block(0, buf_a, SEM_GA, wb_a, [store_p, fetch_b])
        wait_all(wb_b, SEM_SB, "store")

    @pl.when(next_serial)
    def _():
        block(1, buf_b, SEM_GB, wb_b, [store_a, fetch_n_moe])

    @pl.when(jnp.logical_not(next_serial))
    def _():
        block(1, buf_b, SEM_GB, wb_b, [store_a, fetch_n])

    @pl.when(step == last_step)
    def _():
        wait_all(buf_a, SEM_GA, "fetch_moe")
        start_all(*store_b)
        wait_all(wb_a, SEM_SA, "store")
        wait_all(wb_b, SEM_SB, "store")


def _experts(idx_blocks, xh, tok_rep, gate_rep, gate_f, w_gate, w_up, w_down, layer, seq, rows):
    n, d = xh.shape[0], D_MODEL
    e = w_gate.shape[1]
    cap = tok_rep.shape[0] // e
    nsteps = cap // (2 * rows)
    nbatch = gate_f.shape[0]
    seq_shift = seq.bit_length() - 1
    assert 1 << seq_shift == seq and nsteps * 2 * rows == cap
    wspec = pl.BlockSpec((1, 1, d, d), lambda i, j: (layer, i, 0, 0))
    any_spec = pl.BlockSpec(memory_space=pl.ANY)
    return pl.pallas_call(
        functools.partial(_experts_kernel, rows=rows, nsteps=nsteps, seq_shift=seq_shift, nbatch=nbatch,
                          col_chunk=EXPERT_COL_CHUNK),
        grid=(e, nsteps),
        in_specs=[
            any_spec, any_spec,
            pl.BlockSpec((2 * rows, LANES), lambda i, j: (i * nsteps + j, 0)),
            pl.BlockSpec((2 * rows, LANES), lambda i, j: (i * nsteps + j, 0)),
            pl.BlockSpec((nbatch, d), lambda i, j: (0, 0)),
            wspec, wspec, wspec,
        ],
        out_specs=any_spec,
        out_shape=jax.ShapeDtypeStruct((n, 2 * d), _F32),
        scratch_shapes=[
            pltpu.SMEM((1, rows), jnp.int32),
            pltpu.SMEM((1, rows), jnp.int32),
            pltpu.SMEM((1, rows), jnp.int32),
            pltpu.SMEM((1, rows), jnp.int32),
            pltpu.SMEM((1, rows), jnp.int32),
            pltpu.SMEM((1,), jnp.int32),
            pltpu.VMEM((rows, 2 * d), _F32),
            pltpu.VMEM((rows, 2 * d), _F32),
            pltpu.VMEM((rows, d), _F32),
            pltpu.VMEM((rows, d), _F32),
            pltpu.SemaphoreType.DMA((6,)),
        ],
        input_output_aliases={1: 0},
        compiler_params=_params(("arbitrary", "arbitrary")),
        name="experts",
    )(idx_blocks, xh, tok_rep, gate_rep, gate_f, w_gate, w_up, w_down)


def _final_kernel(x_ref, g_ref, o_ref):
    o_ref[0] = _rms(x_ref[0]) * g_ref[...]


def _final_norm(x, g, tile):
    b, s, d = x.shape[0], x.shape[1], D_MODEL
    return pl.pallas_call(
        _final_kernel,
        grid=(b, s // tile),
        in_specs=[pl.BlockSpec((1, tile, d), lambda i, t: (i, t, 0)),
                  pl.BlockSpec((1, d), lambda i, t: (0, 0))],
        out_specs=pl.BlockSpec((1, tile, d), lambda i, t: (i, t, 0)),
        out_shape=jax.ShapeDtypeStruct((b, s, d), _F32),
        compiler_params=_params(("arbitrary", "arbitrary")),
        name="final_norm",
    )(x, g)


def _channel_dft_table():
    k = jnp.arange(HEAD_DIM, dtype=jnp.int32)
    ang = ((k[:, None] * k[None, :]) % HEAD_DIM).astype(_F32) * (2.0 * math.pi / HEAD_DIM)
    groups = FOURIER_W // HEAD_DIM
    eye = jnp.eye(groups, dtype=_F32)
    c = jnp.kron(eye, jnp.cos(ang)) / math.sqrt(HEAD_DIM)
    s = jnp.kron(eye, jnp.sin(ang)) / math.sqrt(HEAD_DIM)
    return jnp.concatenate([c, s], axis=1).astype(_BF16)


def _split_hi_lo(w):
    hi = w.astype(_BF16)
    lo = (w - hi.astype(_F32)).astype(_BF16)
    return jnp.concatenate([hi, lo], axis=-1)


def _seq_dft_tables(s):
    radix = 64
    assert s % radix == 0
    k = jnp.arange(s, dtype=jnp.int32)[:, None]
    t1 = jnp.arange(s // radix, dtype=jnp.int32)[None, :]
    t0 = jnp.arange(radix, dtype=jnp.int32)[None, :]
    ang_a = ((k * t1 * radix) % s).astype(_F32) * (2.0 * math.pi / s)
    ang_b = ((k * t0) % s).astype(_F32) * (2.0 * math.pi / s)
    ca, sa = jnp.cos(ang_a)[:, :, None], jnp.sin(ang_a)[:, :, None]
    cb, sb = jnp.cos(ang_b)[:, None, :], jnp.sin(ang_b)[:, None, :]
    cos = (ca * cb - sa * sb).reshape(s, s)
    nsin = (-(sa * cb + ca * sb)).reshape(s, s)
    return cos.astype(_BF16), nsin.astype(_BF16)


def _encoder(x, mod, p, dft64, tile, erows):
    b, s, d = x.shape
    n = b * s
    cap = CAPACITY_FACTOR * n // N_EXPERTS
    assert cap % (2 * erows) == 0 and cap // erows >= 2
    wc, wsn = _seq_dft_tables(s)
    depth = p["w_in"].shape[0]
    for l in range(depth):
        mod_l = mod[l][:, None, :]
        yac, fc, fs = _mix_in(x, mod_l, p["g_mix"][l][None], p["w_in_bf"][l], p["conv_w"][l],
                              p["g_sgu"][l][None], p["w_sp_bf"][l], p["b_sp_rows"][l],
                              p["g_grp"][l][None], dft64, tile)
        yf = _seq_dft(wc, wsn, fc, fs)
        xh, aff = _mix_out(x, yac, yf, mod_l, p["g_grp"][l][None], p["w_out_bf"][l],
                           p["g_ffn"][l][None], p["w_router_hl"][l], tile)
        aff_t = aff.reshape(n, N_EXPERTS).T.reshape(N_EXPERTS, n // LANES, LANES)
        tok_rep, gate_rep = _route(aff_t, cap)
        idx_blocks = tok_rep[:, :, 0].reshape(N_EXPERTS * cap // erows, 1, erows)
        gate_f = mod_l[:, 0, 5 * d:6 * d]
        x = _experts(idx_blocks, xh.reshape(n, 2 * d),
                     tok_rep.reshape(N_EXPERTS * cap, LANES), gate_rep.reshape(N_EXPERTS * cap, LANES),
                     gate_f, p["w_gate"], p["w_up"], p["w_down"], l, s, erows).reshape(b, s, 2 * d)
    return _final_norm(x, p["g_final"][None], tile)


def kernel(x_prompt, x_sample, c_prompt, c_sample, w_ada, b_ada, g_mix, w_in, conv_w, g_sgu, w_spatial,
           b_spatial, g_grp, w_out, g_ffn, w_router, w_gate, w_up, w_down, g_final):
    nb = x_prompt.shape[0]
    mod = _ada(jnp.concatenate([c_prompt, c_sample], axis=0), w_ada, b_ada)
    p = dict(
        g_mix=g_mix, conv_w=conv_w, g_sgu=g_sgu, g_grp=g_grp, g_ffn=g_ffn, w_router_hl=_split_hi_lo(w_router),
        w_gate=w_gate, w_up=w_up, w_down=w_down, g_final=g_final, w_in=w_in,
        w_in_bf=w_in.astype(_BF16), w_out_bf=w_out.astype(_BF16), w_sp_bf=w_spatial.astype(_BF16),
        b_sp_rows=jnp.repeat(jnp.swapaxes(b_spatial, 1, 2), HEAD_DIM, axis=2),
    )
    dft64 = _channel_dft_table()
    tile = min(512, x_sample.shape[1])
    erows = 512
    y_prompt = _encoder(x_prompt, mod[:, :nb], p, dft64, tile, erows)
    y_sample = _encoder(x_sample, mod[:, nb:], p, dft64, tile, erows)
    return (y_prompt, y_sample)
```

```python
import functools
import math

import jax
import jax.numpy as jnp
from jax import lax
from jax.experimental import pallas as pl
from jax.experimental.pallas import tpu as pltpu

D_MODEL = 1024
HEAD_DIM = 64
CONV_W = 384
FOURIER_W = 256
SGU_W = 384
SGU_HEADS = SGU_W // HEAD_DIM
CHUNK = 128
IN_W = 3 * CONV_W + FOURIER_W + 2 * SGU_W
N_EXPERTS = 16
CAPACITY_FACTOR = 2
N_MOD = 6
EPS = 1e-6

LANES = 128
SUBLANES = 8
VMEM_LIMIT_BYTES = 52 * 1024 * 1024
MXU_COLS = 256
EXPERT_COL_CHUNK = MXU_COLS
EXPERT_MAX_ROWS = 512
IDX_RING = 8

_HI = lax.Precision.HIGHEST
_BF16 = jnp.bfloat16
_F32 = jnp.float32


def _rms(x):
    return x * lax.rsqrt(jnp.mean(x * x, axis=-1, keepdims=True) + EPS)


def _params(sem, vmem=VMEM_LIMIT_BYTES):
    return pltpu.CompilerParams(dimension_semantics=sem, vmem_limit_bytes=vmem)


def _ada_kernel(c_ref, w_ref, b_ref, o_ref):
    c = c_ref[...]
    a = c * jax.nn.sigmoid(c)
    o_ref[0] = jnp.dot(a, w_ref[0], precision=_HI, preferred_element_type=_F32) + b_ref[0]


def _ada(c_all, w_ada, b_ada):
    depth, d, m = w_ada.shape
    nb = c_all.shape[0]
    tn = 1536
    return pl.pallas_call(
        _ada_kernel,
        grid=(depth, m // tn),
        in_specs=[
            pl.BlockSpec((nb, d), lambda l, j: (0, 0)),
            pl.BlockSpec((1, d, tn), lambda l, j: (l, 0, j)),
            pl.BlockSpec((1, 1, tn), lambda l, j: (l, 0, j)),
        ],
        out_specs=pl.BlockSpec((1, nb, tn), lambda l, j: (l, 0, j)),
        out_shape=jax.ShapeDtypeStruct((depth, nb, m), _F32),
        compiler_params=_params(("arbitrary", "arbitrary")),
        name="ada",
    )(c_all, w_ada, b_ada.reshape(depth, 1, m))


def _mix_in_kernel(x_ref, xp_ref, xn_ref, mod_ref, gmix_ref, win_ref, convw_ref, gsgu_ref,
                   wsp_ref, bsp_ref, ggrp_ref, dft_ref, yac_ref, fc_ref, fs_ref, *, tile):
    t = pl.program_id(1)
    nt = pl.num_programs(1)
    shift = mod_ref[0, :, 0:D_MODEL]
    scale = mod_ref[0, :, D_MODEL:2 * D_MODEL]
    gain = gmix_ref[...] * (1.0 + scale)

    def modulated(x):
        return (_rms(x) * gain + shift).astype(_BF16)

    h = modulated(x_ref[0])
    z = jnp.dot(h, win_ref[...], preferred_element_type=_F32)

    def halo(xh_ref):
        zh = jnp.dot(modulated(xh_ref[0]), win_ref[:, CONV_W:3 * CONV_W],
                     preferred_element_type=_F32)
        return zh[:, :CONV_W] * zh[:, CONV_W:]

    a_before = jnp.where(t > 0, halo(xp_ref)[SUBLANES - 1:SUBLANES, :], 0.0)
    a_after = jnp.where(t < nt - 1, halo(xn_ref)[0:1, :], 0.0)

    zb = z[:, 0:CONV_W]
    a = z[:, CONV_W:2 * CONV_W] * z[:, 2 * CONV_W:3 * CONV_W]
    row = lax.broadcasted_iota(jnp.int32, (tile, CONV_W), 0)
    a_prev = jnp.where(row == 0, a_before, pltpu.roll(a, 1, axis=0))
    a_next = jnp.where(row == tile - 1, a_after, pltpu.roll(a, tile - 1, axis=0))
    cw = convw_ref[...]
    conv = zb * (cw[0:1, :] * a_prev + cw[1:2, :] * a + cw[2:3, :] * a_next)
    ya = (_rms(conv) * ggrp_ref[:, 0:CONV_W]).astype(_BF16)

    zf = z[:, 3 * CONV_W:3 * CONV_W + FOURIER_W].astype(_BF16)
    f2 = jnp.dot(zf, dft_ref[...], preferred_element_type=_F32)
    fc_ref[...] = f2[:, :FOURIER_W].astype(_BF16)
    fs_ref[...] = f2[:, FOURIER_W:].astype(_BF16)

    off = 3 * CONV_W + FOURIER_W
    u = jax.nn.gelu(z[:, off:off + SGU_W])
    v = (_rms(jax.nn.gelu(z[:, off + SGU_W:off + 2 * SGU_W])) * gsgu_ref[...]).astype(_BF16)
    lane = lax.broadcasted_iota(jnp.int32, (CHUNK, LANES), 1)
    first_head = lane < HEAD_DIM
    chunks = []
    for c in range(tile // CHUNK):
        pairs = []
        for k in range(SGU_W // LANES):
            vk = v[c * CHUNK:(c + 1) * CHUNK, k * LANES:(k + 1) * LANES]
            r0 = jnp.dot(wsp_ref[2 * k], vk, preferred_element_type=_F32)
            r1 = jnp.dot(wsp_ref[2 * k + 1], vk, preferred_element_type=_F32)
            pairs.append(jnp.where(first_head, r0, r1))
        chunks.append(jnp.concatenate(pairs, axis=1) + bsp_ref[...])
    sv = jnp.concatenate(chunks, axis=0)
    yc = (_rms(u * sv) * ggrp_ref[:, CONV_W + FOURIER_W:]).astype(_BF16)
    yac_ref[0] = jnp.concatenate([ya, yc], axis=1)


def _mix_in(x, mod_l, gmix, win_bf, convw, gsgu, wsp_bf, bsp_rows, ggrp, dft64, tile):
    b, s, d = x.shape[0], x.shape[1], D_MODEL
    nt = s // tile
    hb = tile // SUBLANES
    last_hb = s // SUBLANES - 1
    const2 = lambda i, t: (0, 0)
    return pl.pallas_call(
        functools.partial(_mix_in_kernel, tile=tile),
        grid=(b, nt),
        in_specs=[
            pl.BlockSpec((1, tile, d), lambda i, t: (i, t, 0)),
            pl.BlockSpec((1, SUBLANES, d), lambda i, t: (i, jnp.maximum(t * hb - 1, 0), 0)),
            pl.BlockSpec((1, SUBLANES, d), lambda i, t: (i, jnp.minimum((t + 1) * hb, last_hb), 0)),
            pl.BlockSpec((1, 1, N_MOD * d), lambda i, t: (i, 0, 0)),
            pl.BlockSpec((1, d), const2),
            pl.BlockSpec((d, IN_W), const2),
            pl.BlockSpec((3, CONV_W), const2),
            pl.BlockSpec((1, SGU_W), const2),
            pl.BlockSpec((SGU_HEADS, CHUNK, CHUNK), lambda i, t: (0, 0, 0)),
            pl.BlockSpec((CHUNK, SGU_W), const2),
            pl.BlockSpec((1, d), const2),
            pl.BlockSpec((FOURIER_W, 2 * FOURIER_W), const2),
        ],
        out_specs=[
            pl.BlockSpec((1, tile, CONV_W + SGU_W), lambda i, t: (i, t, 0)),
            pl.BlockSpec((tile, FOURIER_W), lambda i, t: (t, i)),
            pl.BlockSpec((tile, FOURIER_W), lambda i, t: (t, i)),
        ],
        out_shape=[
            jax.ShapeDtypeStruct((b, s, CONV_W + SGU_W), _BF16),
            jax.ShapeDtypeStruct((s, b * FOURIER_W), _BF16),
            jax.ShapeDtypeStruct((s, b * FOURIER_W), _BF16),
        ],
        compiler_params=_params(("arbitrary", "arbitrary")),
        name="mix_in",
    )(x, x, x, mod_l, gmix, win_bf, convw, gsgu, wsp_bf, bsp_rows, ggrp, dft64)


def _seq_dft_kernel(wc_ref, ws_ref, fc_ref, fs_ref, o_ref):
    part = (jnp.dot(wc_ref[...], fc_ref[...], preferred_element_type=_F32)
            + jnp.dot(ws_ref[...], fs_ref[...], preferred_element_type=_F32))

    @pl.when(pl.program_id(2) == 0)
    def _():
        o_ref[...] = part

    @pl.when(pl.program_id(2) > 0)
    def _():
        o_ref[...] += part


def _seq_dft(wc, wsn, fc, fs):
    s = wc.shape[0]
    n = fc.shape[1]
    tm = min(1024, s)
    tn = min(1024, n)
    tk = min(1024, s)
    return pl.pallas_call(
        _seq_dft_kernel,
        grid=(s // tm, n // tn, s // tk),
        in_specs=[
            pl.BlockSpec((tm, tk), lambda i, j, k: (i, k)),
            pl.BlockSpec((tm, tk), lambda i, j, k: (i, k)),
            pl.BlockSpec((tk, tn), lambda i, j, k: (k, j)),
            pl.BlockSpec((tk, tn), lambda i, j, k: (k, j)),
        ],
        out_specs=pl.BlockSpec((tm, tn), lambda i, j, k: (i, j)),
        out_shape=jax.ShapeDtypeStruct((s, n), _F32),
        compiler_params=_params(("arbitrary", "arbitrary", "arbitrary")),
        name="seq_dft",
    )(wc, wsn, fc, fs)


def _mix_out_kernel(x_ref, yac_ref, yf_ref, mod_ref, ggrp_ref, wout_ref, gffn_ref, wr_ref,
                    xh_ref, aff_ref, *, seq_scale):
    d = D_MODEL
    gate_m = mod_ref[0, :, 2 * d:3 * d]
    shift_f = mod_ref[0, :, 3 * d:4 * d]
    scale_f = mod_ref[0, :, 4 * d:5 * d]
    yf = (_rms(yf_ref[...] * seq_scale) * ggrp_ref[:, CONV_W:CONV_W + FOURIER_W]).astype(_BF16)
    yac = yac_ref[0]
    ycat = jnp.concatenate([yac[:, :CONV_W], yf, yac[:, CONV_W:]], axis=1)
    mix = jnp.dot(ycat, wout_ref[...], preferred_element_type=_F32)
    x1 = x_ref[0] + gate_m * mix
    h2 = _rms(x1) * (gffn_ref[...] * (1.0 + scale_f)) + shift_f
    xh_ref[0, :, :d] = x1
    xh_ref[0, :, d:] = h2
    h_hi = h2.astype(_BF16)
    h_lo = (h2 - h_hi.astype(_F32)).astype(_BF16)
    by_hi = jnp.dot(h_hi, wr_ref[...], preferred_element_type=_F32)
    logits = (by_hi[:, :N_EXPERTS] + (by_hi[:, N_EXPERTS:]
              + jnp.dot(h_lo, wr_ref[:, :N_EXPERTS], preferred_element_type=_F32)))
    e = jnp.exp(logits - jnp.max(logits, axis=-1, keepdims=True))
    aff_ref[0] = e / jnp.sum(e, axis=-1, keepdims=True)


def _mix_out(x, yac, yf, mod_l, ggrp, wout_bf, gffn, w_router, tile):
    b, s, d = x.shape[0], x.shape[1], D_MODEL
    const2 = lambda i, t: (0, 0)
    return pl.pallas_call(
        functools.partial(_mix_out_kernel, seq_scale=1.0 / math.sqrt(s)),
        grid=(b, s // tile),
        in_specs=[
            pl.BlockSpec((1, tile, d), lambda i, t: (i, t, 0)),
            pl.BlockSpec((1, tile, CONV_W + SGU_W), lambda i, t: (i, t, 0)),
            pl.BlockSpec((tile, FOURIER_W), lambda i, t: (t, i)),
            pl.BlockSpec((1, 1, N_MOD * d), lambda i, t: (i, 0, 0)),
            pl.BlockSpec((1, d), const2),
            pl.BlockSpec((d, d), const2),
            pl.BlockSpec((1, d), const2),
            pl.BlockSpec((d, 2 * N_EXPERTS), const2),
        ],
        out_specs=[
            pl.BlockSpec((1, tile, 2 * d), lambda i, t: (i, t, 0)),
            pl.BlockSpec((1, tile, N_EXPERTS), lambda i, t: (i, t, 0)),
        ],
        out_shape=[
            jax.ShapeDtypeStruct((b, s, 2 * d), _F32),
            jax.ShapeDtypeStruct((b, s, N_EXPERTS), _F32),
        ],
        compiler_params=_params(("arbitrary", "arbitrary")),
        name="mix_out",
    )(x, yac, yf, mod_l, ggrp, wout_bf, gffn, w_router)


def _cumsum_mats(rows):
    qi = lax.broadcasted_iota(jnp.int32, (LANES, LANES), 0)
    qj = lax.broadcasted_iota(jnp.int32, (LANES, LANES), 1)
    ri = lax.broadcasted_iota(jnp.int32, (rows, rows), 0)
    rj = lax.broadcasted_iota(jnp.int32, (rows, rows), 1)
    def ones_where(mask):
        return jnp.where(mask, 1.0, 0.0).astype(_BF16)

    lane_excl = ones_where(qi < qj)
    lane_incl = ones_where(qi <= qj)
    row_excl = ones_where(rj < ri)
    row_incl_t = ones_where(ri <= rj)
    return lane_excl, lane_incl, row_excl, row_incl_t


def _route_kernel(aff_ref, idx_ref, gate_ref, *, cap, jb):
    a = aff_ref[0]
    rows = a.shape[0]
    lane_excl, lane_incl, row_excl, row_incl_t = _cumsum_mats(rows)

    def count(mask):
        return jnp.sum(jnp.where(mask, 1.0, 0.0)).astype(jnp.int32)

    def refine(thr_bits, shift, nbits):
        passing = jnp.int32(0)
        for m in range(1, 1 << nbits):
            cand_f = lax.bitcast_convert_type(thr_bits | (jnp.int32(m) << shift), _F32)
            passing += (count(a >= cand_f) >= cap).astype(jnp.int32)
        return thr_bits | (passing << shift)

    thr_bits = lax.fori_loop(0, 10, lambda i, t: refine(t, 27 - 3 * i, 3), jnp.int32(0))
    thr = lax.bitcast_convert_type(thr_bits, _F32)
    above = a > thr
    tied = a == thr
    need = cap - count(above)

    tied_b = jnp.where(tied, 1.0, 0.0).astype(_BF16)
    in_row = jnp.dot(tied_b, lane_excl, preferred_element_type=_F32)
    row_tot = jnp.dot(tied_b, jnp.ones((LANES, LANES), _BF16), preferred_element_type=_F32)
    before_row = jnp.dot(row_excl, row_tot.astype(_BF16), preferred_element_type=_F32)
    tie_rank = (in_row + before_row).astype(jnp.int32)
    sel = above | (tied & (tie_rank < need))
    sel_b = jnp.where(sel, 1.0, 0.0).astype(_BF16)

    row_cnt = lax.dot_general(jnp.ones((SUBLANES, LANES), _BF16), sel_b,
                              (((1,), (1,)), ((), ())), preferred_element_type=_F32)
    incl = jnp.dot(row_cnt.astype(_BF16), row_incl_t, preferred_element_type=_F32)[0:1, :]
    excl = incl - row_cnt[0:1, :]
    a_hi = a.astype(_BF16)
    a_mid = (a - a_hi.astype(_F32)).astype(_BF16)
    a_lo = (a - a_hi.astype(_F32) - a_mid.astype(_F32)).astype(_BF16)
    row_id = lax.broadcasted_iota(jnp.int32, (1, rows), 1).astype(_F32)
    lane_id = lax.broadcasted_iota(jnp.int32, (jb, LANES), 1).astype(_F32)

    for c in range(cap // jb):
        j = (lax.broadcasted_iota(jnp.int32, (jb, 1), 0) + c * jb).astype(_F32)
        hit = (excl <= j) & (j < incl)
        hit_f = jnp.where(hit, 1.0, 0.0)
        r_of_j = jnp.sum(hit_f * row_id, axis=1, keepdims=True)
        target = j + 1.0 - jnp.sum(hit_f * excl, axis=1, keepdims=True)
        row_sel = jnp.dot(hit_f.astype(_BF16), sel_b, preferred_element_type=_F32)
        within = jnp.dot(row_sel.astype(_BF16), lane_incl, preferred_element_type=_F32)
        l_of_j = jnp.sum(jnp.where(within < target, 1.0, 0.0), axis=1, keepdims=True)
        hit_b = hit_f.astype(_BF16)
        a_row = (jnp.dot(hit_b, a_hi, preferred_element_type=_F32)
                 + jnp.dot(hit_b, a_mid, preferred_element_type=_F32)
                 + jnp.dot(hit_b, a_lo, preferred_element_type=_F32))
        gate = jnp.sum(jnp.where(lane_id == l_of_j, a_row, 0.0), axis=1, keepdims=True)
        token = (r_of_j * LANES + l_of_j).astype(jnp.int32)
        idx_ref[0, c * jb:(c + 1) * jb, :] = jnp.broadcast_to(token, (jb, LANES))
        gate_ref[0, c * jb:(c + 1) * jb, :] = jnp.broadcast_to(gate, (jb, LANES))


def _route(aff_t, cap):
    e, rows, _ = aff_t.shape
    jb = min(512, cap)
    return pl.pallas_call(
        functools.partial(_route_kernel, cap=cap, jb=jb),
        grid=(e,),
        in_specs=[pl.BlockSpec((1, rows, LANES), lambda i: (i, 0, 0))],
        out_specs=[
            pl.BlockSpec((1, cap, LANES), lambda i: (i, 0, 0)),
            pl.BlockSpec((1, cap, LANES), lambda i: (i, 0, 0)),
        ],
        out_shape=[
            jax.ShapeDtypeStruct((e, cap, LANES), jnp.int32),
            jax.ShapeDtypeStruct((e, cap, LANES), _F32),
        ],
        compiler_params=_params(("arbitrary",)),
        name="route",
    )(aff_t)


def _experts_kernel(idx_hbm, xin_hbm, tok_ref, gate_ref, gf_ref, wg_ref, wu_ref, wd_ref,
                    xh_hbm, idx_ring, prev_first, buf_a, buf_b, wb_a, wb_b, sems,
                    *, rows, nsteps, seq_shift, nbatch, col_chunk):
    del xin_hbm
    d = D_MODEL
    e = pl.program_id(0)
    j = pl.program_id(1)
    step = e * nsteps + j
    last_step = pl.num_programs(0) * nsteps - 1
    blk_a = 2 * step
    blk_p = jnp.maximum(blk_a - 1, 0)
    last_blk = 2 * last_step + 1
    blk_n = jnp.minimum(blk_a + 2, last_blk)
    blk_m = jnp.minimum(blk_a + 3, last_blk)
    SEM_IDX, SEM_GA, SEM_GB, SEM_SA, SEM_SB, SEM_FIX = range(6)

    def idx_load(blk):
        return pltpu.make_async_copy(idx_hbm.at[blk], idx_ring.at[pl.ds(blk % IDX_RING, 1)], sems.at[SEM_IDX])

    @pl.when(step == 0)
    def _():
        for b in range(4):
            idx_load(b).start()
        for b in range(4):
            idx_load(b).wait()

    @pl.when((step > 0) & (step < last_step))
    def _():
        idx_load(blk_a + 2).wait()
        idx_load(blk_a + 3).wait()

    @pl.when(step + 2 <= last_step)
    def _():
        idx_load(blk_a + 4).start()
        idx_load(blk_a + 5).start()

    class RingRow:
        def __init__(self, blk):
            self.slot = blk % IDX_RING

        def __getitem__(self, key):
            return idx_ring[self.slot, key[1]]

    idx_p, idx_a, idx_b, idx_n, idx_m = (RingRow(b) for b in (blk_p, blk_a, blk_a + 1, blk_n, blk_m))

    res_cols = pl.ds(0, d)
    moe_cols = pl.ds(d, d)

    def row_copy(idx_ref, r, buf, sem, kind):
        tok = idx_ref[0, r]
        if kind == "store":
            return pltpu.make_async_copy(buf.at[pl.ds(r, 1)], xh_hbm.at[pl.ds(tok, 1), res_cols], sems.at[sem])
        if kind == "fetch":
            return pltpu.make_async_copy(xh_hbm.at[pl.ds(tok, 1)], buf.at[pl.ds(r, 1)], sems.at[sem])
        cols = moe_cols if kind == "fetch_moe" else res_cols
        return pltpu.make_async_copy(xh_hbm.at[pl.ds(tok, 1), cols], buf.at[pl.ds(r, 1), cols], sems.at[sem])

    def start_all(idx_ref, buf, sem, kind):
        def body(r, carry):
            row_copy(idx_ref, r, buf, sem, kind).start()
            return carry
        lax.fori_loop(0, rows, body, 0)

    def wait_all(buf, sem, kind):
        all_rows = pl.ds(0, rows)
        if kind == "store":
            pltpu.make_async_copy(buf, xh_hbm.at[all_rows, res_cols], sems.at[sem]).wait()
        elif kind == "fetch":
            pltpu.make_async_copy(xh_hbm.at[all_rows], buf, sems.at[sem]).wait()
        else:
            cols = moe_cols if kind == "fetch_moe" else res_cols
            pltpu.make_async_copy(xh_hbm.at[all_rows, cols], buf.at[:, cols], sems.at[sem]).wait()

    def block(part, buf, sem_g, wb, streams, fetched=False):
        if not fetched:
            wait_all(buf, sem_g, "fetch")
        todo = [functools.partial(row_copy, stream[0], r, *stream[1:])
                for r in range(rows) for stream in streams]
        nch = D_MODEL // col_chunk
        per_call = -(-len(todo) // (3 * nch))

        def start_some():
            for k, make in enumerate(todo[:per_call]):
                make().start(priority=k % 2)
            del todo[:per_call]

        x = buf[:, d:].astype(_BF16)
        hid = []
        for c in range(nch):
            cs = slice(c * col_chunk, (c + 1) * col_chunk)
            hg = jnp.dot(x, wg_ref[0, 0, :, cs].astype(_BF16), preferred_element_type=_F32)
            start_some()
            hu = jnp.dot(x, wu_ref[0, 0, :, cs].astype(_BF16), preferred_element_type=_F32)
            start_some()
            hid.append((hg * jax.nn.sigmoid(hg) * hu).astype(_BF16))
        hid = jnp.concatenate(hid, axis=1)
        rs = slice(part * rows, (part + 1) * rows)
        seq_of_row = tok_ref[rs, :] >> seq_shift
        gate = gate_ref[rs, :]
        upd = []
        for c in range(nch):
            cs = slice(c * col_chunk, (c + 1) * col_chunk)
            y = jnp.dot(hid, wd_ref[0, 0, :, cs].astype(_BF16), preferred_element_type=_F32)
            start_some()
            seq_c = jnp.concatenate([seq_of_row] * (col_chunk // LANES), axis=1)
            gf = jnp.zeros_like(y)
            for b in range(nbatch):
                gf = jnp.where(seq_c == b, gf_ref[b:b + 1, cs], gf)
            upd.append(gf * (y * jnp.concatenate([gate] * (col_chunk // LANES), axis=1)))
        assert not todo
        wb[...] = buf[:, :d] + jnp.concatenate(upd, axis=1)

    fetch_b = (idx_b, buf_b, SEM_GB, "fetch")
    fetch_n = (idx_n, buf_a, SEM_GA, "fetch")
    fetch_n_moe = (idx_n, buf_a, SEM_GA, "fetch_moe")
    store_p = (idx_p, wb_b, SEM_SB, "store")
    store_a = (idx_a, wb_a, SEM_SA, "store")
    store_b = (idx_b, wb_b, SEM_SB, "store")

    @pl.when(step == 0)
    def _():
        start_all(idx_a, buf_a, SEM_GA, "fetch")
        prev_first[0] = 0

    @pl.when(step > 0)
    def _():
        wait_all(wb_a, SEM_SA, "store")

    def overlaps(first, last, other_first, other_last):
        return jnp.logical_not((other_last < first) | (last < other_first))

    serial = (step == 0) | overlaps(idx_a[0, 0], idx_b[0, rows - 1], prev_first[0], idx_p[0, rows - 1])
    next_serial = overlaps(idx_n[0, 0], idx_m[0, rows - 1], idx_a[0, 0], idx_b[0, rows - 1])
    prev_first[0] = idx_a[0, 0]

    @pl.when(serial)
    def _():
        @pl.when(step == 0)
        def _():
            wait_all(buf_a, SEM_GA, "fetch")

        @pl.when(step > 0)
        def _():
            start_all(*store_p)
            wait_all(wb_b, SEM_SB, "store")
            wait_all(buf_a, SEM_GA, "fetch_moe")
            start_all(idx_a, buf_a, SEM_FIX, "fetch_res")
            wait_all(buf_a, SEM_FIX, "fetch_res")
        block(0, buf_a, SEM_GA, wb_a, [fetch_b], fetched=True)

    @pl.when(jnp.logical_not(serial))
    def _():
        block(0, buf_a, SEM_GA, wb_a, [store_p, fetch_b])
        wait_all(wb_b, SEM_SB, "store")

    @pl.when(next_serial)
    def _():
        block(1, buf_b, SEM_GB, wb_b, [store_a, fetch_n_moe])

    @pl.when(jnp.logical_not(next_serial))
    def _():
        block(1, buf_b, SEM_GB, wb_b, [store_a, fetch_n])

    @pl.when(step == last_step)
    def _():
        wait_all(buf_a, SEM_GA, "fetch_moe")
        start_all(*store_b)
        wait_all(wb_a, SEM_SA, "store")
        wait_all(wb_b, SEM_SB, "store")


def _experts(idx_blocks, xh, tok_rep, gate_rep, gate_f, w_gate, w_up, w_down, layer, seq, rows):
    n, d = xh.shape[0], D_MODEL
    e = w_gate.shape[1]
    cap = tok_rep.shape[0] // e
    nsteps = cap // (2 * rows)
    nbatch = gate_f.shape[0]
    seq_shift = seq.bit_length() - 1
    assert 1 << seq_shift == seq and nsteps * 2 * rows == cap
    wspec = pl.BlockSpec((1, 1, d, d), lambda i, j: (layer, i, 0, 0))
    any_spec = pl.BlockSpec(memory_space=pl.ANY)
    return pl.pallas_call(
        functools.partial(_experts_kernel, rows=rows, nsteps=nsteps, seq_shift=seq_shift, nbatch=nbatch,
                          col_chunk=EXPERT_COL_CHUNK),
        grid=(e, nsteps),
        in_specs=[
            any_spec, any_spec,
            pl.BlockSpec((2 * rows, LANES), lambda i, j: (i * nsteps + j, 0)),
            pl.BlockSpec((2 * rows, LANES), lambda i, j: (i * nsteps + j, 0)),
            pl.BlockSpec((nbatch, d), lambda i, j: (0, 0)),
            wspec, wspec, wspec,
        ],
        out_specs=any_spec,
        out_shape=jax.ShapeDtypeStruct((n, 2 * d), _F32),
        scratch_shapes=[
            pltpu.SMEM((IDX_RING, rows), jnp.int32),
            pltpu.SMEM((1,), jnp.int32),
            pltpu.VMEM((rows, 2 * d), _F32),
            pltpu.VMEM((rows, 2 * d), _F32),
            pltpu.VMEM((rows, d), _F32),
            pltpu.VMEM((rows, d), _F32),
            pltpu.SemaphoreType.DMA((6,)),
        ],
        input_output_aliases={1: 0},
        compiler_params=_params(("arbitrary", "arbitrary")),
        name="experts",
    )(idx_blocks, xh, tok_rep, gate_rep, gate_f, w_gate, w_up, w_down)


def _final_kernel(x_ref, g_ref, o_ref):
    o_ref[0] = _rms(x_ref[0]) * g_ref[...]


def _final_norm(x, g, tile):
    b, s, d = x.shape[0], x.shape[1], D_MODEL
    return pl.pallas_call(
        _final_kernel,
        grid=(b, s // tile),
        in_specs=[pl.BlockSpec((1, tile, d), lambda i, t: (i, t, 0)),
                  pl.BlockSpec((1, d), lambda i, t: (0, 0))],
        out_specs=pl.BlockSpec((1, tile, d), lambda i, t: (i, t, 0)),
        out_shape=jax.ShapeDtypeStruct((b, s, d), _F32),
        compiler_params=_params(("arbitrary", "arbitrary")),
        name="final_norm",
    )(x, g)


def _channel_dft_table():
    k = jnp.arange(HEAD_DIM, dtype=jnp.int32)
    ang = ((k[:, None] * k[None, :]) % HEAD_DIM).astype(_F32) * (2.0 * math.pi / HEAD_DIM)
    groups = FOURIER_W // HEAD_DIM
    eye = jnp.eye(groups, dtype=_F32)
    c = jnp.kron(eye, jnp.cos(ang)) / math.sqrt(HEAD_DIM)
    s = jnp.kron(eye, jnp.sin(ang)) / math.sqrt(HEAD_DIM)
    return jnp.concatenate([c, s], axis=1).astype(_BF16)


def _split_hi_lo(w):
    hi = w.astype(_BF16)
    lo = (w - hi.astype(_F32)).astype(_BF16)
    return jnp.concatenate([hi, lo], axis=-1)


def _seq_dft_tables(s):
    radix = 64
    assert s % radix == 0
    k = jnp.arange(s, dtype=jnp.int32)[:, None]
    t1 = jnp.arange(s // radix, dtype=jnp.int32)[None, :]
    t0 = jnp.arange(radix, dtype=jnp.int32)[None, :]
    ang_a = ((k * t1 * radix) % s).astype(_F32) * (2.0 * math.pi / s)
    ang_b = ((k * t0) % s).astype(_F32) * (2.0 * math.pi / s)
    ca, sa = jnp.cos(ang_a)[:, :, None], jnp.sin(ang_a)[:, :, None]
    cb, sb = jnp.cos(ang_b)[:, None, :], jnp.sin(ang_b)[:, None, :]
    cos = (ca * cb - sa * sb).reshape(s, s)
    nsin = (-(sa * cb + ca * sb)).reshape(s, s)
    return cos.astype(_BF16), nsin.astype(_BF16)


def _encoder(x, mod, p, dft64, tile, erows=None):
    b, s, d = x.shape
    n = b * s
    cap = CAPACITY_FACTOR * n // N_EXPERTS
    if erows is None:
        erows = min(EXPERT_MAX_ROWS, cap // 8)
    assert cap % (2 * erows) == 0 and cap // erows >= 2
    wc, wsn = _seq_dft_tables(s)
    depth = p["w_in"].shape[0]
    for l in range(depth):
        mod_l = mod[l][:, None, :]
        yac, fc, fs = _mix_in(x, mod_l, p["g_mix"][l][None], p["w_in_bf"][l], p["conv_w"][l],
                              p["g_sgu"][l][None], p["w_sp_bf"][l], p["b_sp_rows"][l],
                              p["g_grp"][l][None], dft64, tile)
        yf = _seq_dft(wc, wsn, fc, fs)
        xh, aff = _mix_out(x, yac, yf, mod_l, p["g_grp"][l][None], p["w_out_bf"][l],
                           p["g_ffn"][l][None], p["w_router_hl"][l], tile)
        aff_t = aff.reshape(n, N_EXPERTS).T.reshape(N_EXPERTS, n // LANES, LANES)
        tok_rep, gate_rep = _route(aff_t, cap)
        idx_blocks = tok_rep[:, :, 0].reshape(N_EXPERTS * cap // erows, 1, erows)
        gate_f = mod_l[:, 0, 5 * d:6 * d]
        x = _experts(idx_blocks, xh.reshape(n, 2 * d),
                     tok_rep.reshape(N_EXPERTS * cap, LANES), gate_rep.reshape(N_EXPERTS * cap, LANES),
                     gate_f, p["w_gate"], p["w_up"], p["w_down"], l, s, erows).reshape(b, s, 2 * d)
    return _final_norm(x, p["g_final"][None], tile)


def kernel(x_prompt, x_sample, c_prompt, c_sample, w_ada, b_ada, g_mix, w_in, conv_w, g_sgu, w_spatial,
           b_spatial, g_grp, w_out, g_ffn, w_router, w_gate, w_up, w_down, g_final):
    nb = x_prompt.shape[0]
    mod = _ada(jnp.concatenate([c_prompt, c_sample], axis=0), w_ada, b_ada)
    p = dict(
        g_mix=g_mix, conv_w=conv_w, g_sgu=g_sgu, g_grp=g_grp, g_ffn=g_ffn, w_router_hl=_split_hi_lo(w_router),
        w_gate=w_gate, w_up=w_up, w_down=w_down, g_final=g_final, w_in=w_in,
        w_in_bf=w_in.astype(_BF16), w_out_bf=w_out.astype(_BF16), w_sp_bf=w_spatial.astype(_BF16),
        b_sp_rows=jnp.repeat(jnp.swapaxes(b_spatial, 1, 2), HEAD_DIM, axis=2),
    )
    dft64 = _channel_dft_table()
    tile = min(512, x_sample.shape[1])
    y_prompt = _encoder(x_prompt, mod[:, :nb], p, dft64, tile)
    y_sample = _encoder(x_sample, mod[:, nb:], p, dft64, tile)
    return (y_prompt, y_sample)
```

```python
import functools
import math

import jax
import jax.numpy as jnp
from jax import lax
from jax.experimental import pallas as pl
from jax.experimental.pallas import tpu as pltpu

D_MODEL = 1024
HEAD_DIM = 64
CONV_W = 384
FOURIER_W = 256
SGU_W = 384
SGU_HEADS = SGU_W // HEAD_DIM
CHUNK = 128
IN_W = 3 * CONV_W + FOURIER_W + 2 * SGU_W
N_EXPERTS = 16
CAPACITY_FACTOR = 2
N_MOD = 6
EPS = 1e-6

LANES = 128
SUBLANES = 8
VMEM_LIMIT_BYTES = 52 * 1024 * 1024
MXU_COLS = 256
EXPERT_COL_CHUNK = MXU_COLS
EXPERT_MAX_ROWS = 512

_HI = lax.Precision.HIGHEST
_BF16 = jnp.bfloat16
_F32 = jnp.float32


def _rms(x):
    return x * lax.rsqrt(jnp.mean(x * x, axis=-1, keepdims=True) + EPS)


def _params(sem, vmem=VMEM_LIMIT_BYTES):
    return pltpu.CompilerParams(dimension_semantics=sem, vmem_limit_bytes=vmem)


def _ada_kernel(c_ref, w_ref, b_ref, o_ref):
    c = c_ref[...]
    a = c * jax.nn.sigmoid(c)
    o_ref[0] = jnp.dot(a, w_ref[0], precision=_HI, preferred_element_type=_F32) + b_ref[0]


def _ada(c_all, w_ada, b_ada):
    depth, d, m = w_ada.shape
    nb = c_all.shape[0]
    tn = 1536
    return pl.pallas_call(
        _ada_kernel,
        grid=(depth, m // tn),
        in_specs=[
            pl.BlockSpec((nb, d), lambda l, j: (0, 0)),
            pl.BlockSpec((1, d, tn), lambda l, j: (l, 0, j)),
            pl.BlockSpec((1, 1, tn), lambda l, j: (l, 0, j)),
        ],
        out_specs=pl.BlockSpec((1, nb, tn), lambda l, j: (l, 0, j)),
        out_shape=jax.ShapeDtypeStruct((depth, nb, m), _F32),
        compiler_params=_params(("arbitrary", "arbitrary")),
        name="ada",
    )(c_all, w_ada, b_ada.reshape(depth, 1, m))


def _mix_in_kernel(x_ref, xp_ref, xn_ref, mod_ref, gmix_ref, win_ref, convw_ref, gsgu_ref,
                   wsp_ref, bsp_ref, ggrp_ref, dft_ref, yac_ref, fc_ref, fs_ref, *, tile):
    t = pl.program_id(1)
    nt = pl.num_programs(1)
    shift = mod_ref[0, :, 0:D_MODEL]
    scale = mod_ref[0, :, D_MODEL:2 * D_MODEL]
    gain = gmix_ref[...] * (1.0 + scale)

    def modulated(x):
        return (_rms(x) * gain + shift).astype(_BF16)

    h = modulated(x_ref[0])
    z = jnp.dot(h, win_ref[...], preferred_element_type=_F32)

    def halo(xh_ref):
        zh = jnp.dot(modulated(xh_ref[0]), win_ref[:, CONV_W:3 * CONV_W],
                     preferred_element_type=_F32)
        return zh[:, :CONV_W] * zh[:, CONV_W:]

    a_before = jnp.where(t > 0, halo(xp_ref)[SUBLANES - 1:SUBLANES, :], 0.0)
    a_after = jnp.where(t < nt - 1, halo(xn_ref)[0:1, :], 0.0)

    zb = z[:, 0:CONV_W]
    a = z[:, CONV_W:2 * CONV_W] * z[:, 2 * CONV_W:3 * CONV_W]
    row = lax.broadcasted_iota(jnp.int32, (tile, CONV_W), 0)
    a_prev = jnp.where(row == 0, a_before, pltpu.roll(a, 1, axis=0))
    a_next = jnp.where(row == tile - 1, a_after, pltpu.roll(a, tile - 1, axis=0))
    cw = convw_ref[...]
    conv = zb * (cw[0:1, :] * a_prev + cw[1:2, :] * a + cw[2:3, :] * a_next)
    ya = (_rms(conv) * ggrp_ref[:, 0:CONV_W]).astype(_BF16)

    zf = z[:, 3 * CONV_W:3 * CONV_W + FOURIER_W].astype(_BF16)
    f2 = jnp.dot(zf, dft_ref[...], preferred_element_type=_F32)
    fc_ref[...] = f2[:, :FOURIER_W].astype(_BF16)
    fs_ref[...] = f2[:, FOURIER_W:].astype(_BF16)

    off = 3 * CONV_W + FOURIER_W
    u = jax.nn.gelu(z[:, off:off + SGU_W])
    v = (_rms(jax.nn.gelu(z[:, off + SGU_W:off + 2 * SGU_W])) * gsgu_ref[...]).astype(_BF16)
    lane = lax.broadcasted_iota(jnp.int32, (CHUNK, LANES), 1)
    first_head = lane < HEAD_DIM
    chunks = []
    for c in range(tile // CHUNK):
        pairs = []
        for k in range(SGU_W // LANES):
            vk = v[c * CHUNK:(c + 1) * CHUNK, k * LANES:(k + 1) * LANES]
            r0 = jnp.dot(wsp_ref[2 * k], vk, preferred_element_type=_F32)
            r1 = jnp.dot(wsp_ref[2 * k + 1], vk, preferred_element_type=_F32)
            pairs.append(jnp.where(first_head, r0, r1))
        chunks.append(jnp.concatenate(pairs, axis=1) + bsp_ref[...])
    sv = jnp.concatenate(chunks, axis=0)
    yc = (_rms(u * sv) * ggrp_ref[:, CONV_W + FOURIER_W:]).astype(_BF16)
    yac_ref[0] = jnp.concatenate([ya, yc], axis=1)


def _mix_in(x, mod_l, gmix, win_bf, convw, gsgu, wsp_bf, bsp_rows, ggrp, dft64, tile):
    b, s, d = x.shape[0], x.shape[1], D_MODEL
    nt = s // tile
    hb = tile // SUBLANES
    last_hb = s // SUBLANES - 1
    const2 = lambda i, t: (0, 0)
    return pl.pallas_call(
        functools.partial(_mix_in_kernel, tile=tile),
        grid=(b, nt),
        in_specs=[
            pl.BlockSpec((1, tile, d), lambda i, t: (i, t, 0)),
            pl.BlockSpec((1, SUBLANES, d), lambda i, t: (i, jnp.maximum(t * hb - 1, 0), 0)),
            pl.BlockSpec((1, SUBLANES, d), lambda i, t: (i, jnp.minimum((t + 1) * hb, last_hb), 0)),
            pl.BlockSpec((1, 1, N_MOD * d), lambda i, t: (i, 0, 0)),
            pl.BlockSpec((1, d), const2),
            pl.BlockSpec((d, IN_W), const2),
            pl.BlockSpec((3, CONV_W), const2),
            pl.BlockSpec((1, SGU_W), const2),
            pl.BlockSpec((SGU_HEADS, CHUNK, CHUNK), lambda i, t: (0, 0, 0)),
            pl.BlockSpec((CHUNK, SGU_W), const2),
            pl.BlockSpec((1, d), const2),
            pl.BlockSpec((FOURIER_W, 2 * FOURIER_W), const2),
        ],
        out_specs=[
            pl.BlockSpec((1, tile, CONV_W + SGU_W), lambda i, t: (i, t, 0)),
            pl.BlockSpec((tile, FOURIER_W), lambda i, t: (t, i)),
            pl.BlockSpec((tile, FOURIER_W), lambda i, t: (t, i)),
        ],
        out_shape=[
            jax.ShapeDtypeStruct((b, s, CONV_W + SGU_W), _BF16),
            jax.ShapeDtypeStruct((s, b * FOURIER_W), _BF16),
            jax.ShapeDtypeStruct((s, b * FOURIER_W), _BF16),
        ],
        compiler_params=_params(("arbitrary", "arbitrary")),
        name="mix_in",
    )(x, x, x, mod_l, gmix, win_bf, convw, gsgu, wsp_bf, bsp_rows, ggrp, dft64)


def _seq_dft_kernel(wc_ref, ws_ref, fc_ref, fs_ref, o_ref):
    part = (jnp.dot(wc_ref[...], fc_ref[...], preferred_element_type=_F32)
            + jnp.dot(ws_ref[...], fs_ref[...], preferred_element_type=_F32))

    @pl.when(pl.program_id(2) == 0)
    def _():
        o_ref[...] = part

    @pl.when(pl.program_id(2) > 0)
    def _():
        o_ref[...] += part


def _seq_dft(wc, wsn, fc, fs):
    s = wc.shape[0]
    n = fc.shape[1]
    tm = min(1024, s)
    tn = min(1024, n)
    tk = min(1024, s)
    return pl.pallas_call(
        _seq_dft_kernel,
        grid=(s // tm, n // tn, s // tk),
        in_specs=[
            pl.BlockSpec((tm, tk), lambda i, j, k: (i, k)),
            pl.BlockSpec((tm, tk), lambda i, j, k: (i, k)),
            pl.BlockSpec((tk, tn), lambda i, j, k: (k, j)),
            pl.BlockSpec((tk, tn), lambda i, j, k: (k, j)),
        ],
        out_specs=pl.BlockSpec((tm, tn), lambda i, j, k: (i, j)),
        out_shape=jax.ShapeDtypeStruct((s, n), _F32),
        compiler_params=_params(("arbitrary", "arbitrary", "arbitrary")),
        name="seq_dft",
    )(wc, wsn, fc, fs)


def _mix_out_kernel(x_ref, yac_ref, yf_ref, mod_ref, ggrp_ref, wout_ref, gffn_ref, wr_ref,
                    xh_ref, aff_ref, *, seq_scale):
    d = D_MODEL
    gate_m = mod_ref[0, :, 2 * d:3 * d]
    shift_f = mod_ref[0, :, 3 * d:4 * d]
    scale_f = mod_ref[0, :, 4 * d:5 * d]
    yf = (_rms(yf_ref[...] * seq_scale) * ggrp_ref[:, CONV_W:CONV_W + FOURIER_W]).astype(_BF16)
    yac = yac_ref[0]
    ycat = jnp.concatenate([yac[:, :CONV_W], yf, yac[:, CONV_W:]], axis=1)
    mix = jnp.dot(ycat, wout_ref[...], preferred_element_type=_F32)
    x1 = x_ref[0] + gate_m * mix
    h2 = _rms(x1) * (gffn_ref[...] * (1.0 + scale_f)) + shift_f
    xh_ref[0, :, :d] = x1
    xh_ref[0, :, d:] = h2
    h_hi = h2.astype(_BF16)
    h_lo = (h2 - h_hi.astype(_F32)).astype(_BF16)
    by_hi = jnp.dot(h_hi, wr_ref[...], preferred_element_type=_F32)
    logits = (by_hi[:, :N_EXPERTS] + (by_hi[:, N_EXPERTS:]
              + jnp.dot(h_lo, wr_ref[:, :N_EXPERTS], preferred_element_type=_F32)))
    e = jnp.exp(logits - jnp.max(logits, axis=-1, keepdims=True))
    aff_ref[0] = e / jnp.sum(e, axis=-1, keepdims=True)


def _mix_out(x, yac, yf, mod_l, ggrp, wout_bf, gffn, w_router, tile):
    b, s, d = x.shape[0], x.shape[1], D_MODEL
    const2 = lambda i, t: (0, 0)
    return pl.pallas_call(
        functools.partial(_mix_out_kernel, seq_scale=1.0 / math.sqrt(s)),
        grid=(b, s // tile),
        in_specs=[
            pl.BlockSpec((1, tile, d), lambda i, t: (i, t, 0)),
            pl.BlockSpec((1, tile, CONV_W + SGU_W), lambda i, t: (i, t, 0)),
            pl.BlockSpec((tile, FOURIER_W), lambda i, t: (t, i)),
            pl.BlockSpec((1, 1, N_MOD * d), lambda i, t: (i, 0, 0)),
            pl.BlockSpec((1, d), const2),
            pl.BlockSpec((d, d), const2),
            pl.BlockSpec((1, d), const2),
            pl.BlockSpec((d, 2 * N_EXPERTS), const2),
        ],
        out_specs=[
            pl.BlockSpec((1, tile, 2 * d), lambda i, t: (i, t, 0)),
            pl.BlockSpec((1, tile, N_EXPERTS), lambda i, t: (i, t, 0)),
        ],
        out_shape=[
            jax.ShapeDtypeStruct((b, s, 2 * d), _F32),
            jax.ShapeDtypeStruct((b, s, N_EXPERTS), _F32),
        ],
        compiler_params=_params(("arbitrary", "arbitrary")),
        name="mix_out",
    )(x, yac, yf, mod_l, ggrp, wout_bf, gffn, w_router)


def _cumsum_mats(rows):
    qi = lax.broadcasted_iota(jnp.int32, (LANES, LANES), 0)
    qj = lax.broadcasted_iota(jnp.int32, (LANES, LANES), 1)
    ri = lax.broadcasted_iota(jnp.int32, (rows, rows), 0)
    rj = lax.broadcasted_iota(jnp.int32, (rows, rows), 1)
    def ones_where(mask):
        return jnp.where(mask, 1.0, 0.0).astype(_BF16)

    lane_excl = ones_where(qi < qj)
    lane_incl = ones_where(qi <= qj)
    row_excl = ones_where(rj < ri)
    row_incl_t = ones_where(ri <= rj)
    return lane_excl, lane_incl, row_excl, row_incl_t


def _route_kernel(aff_ref, idx_ref, gate_ref, idxc_ref, *, cap, jb):
    a = aff_ref[0]
    rows = a.shape[0]
    lane_excl, lane_incl, row_excl, row_incl_t = _cumsum_mats(rows)

    def count(mask):
        return jnp.sum(jnp.where(mask, 1.0, 0.0)).astype(jnp.int32)

    def refine(thr_bits, shift, nbits):
        passing = jnp.int32(0)
        for m in range(1, 1 << nbits):
            cand_f = lax.bitcast_convert_type(thr_bits | (jnp.int32(m) << shift), _F32)
            passing += (count(a >= cand_f) >= cap).astype(jnp.int32)
        return thr_bits | (passing << shift)

    thr_bits = lax.fori_loop(0, 10, lambda i, t: refine(t, 27 - 3 * i, 3), jnp.int32(0))
    thr = lax.bitcast_convert_type(thr_bits, _F32)
    above = a > thr
    tied = a == thr
    need = cap - count(above)

    tied_b = jnp.where(tied, 1.0, 0.0).astype(_BF16)
    in_row = jnp.dot(tied_b, lane_excl, preferred_element_type=_F32)
    row_tot = jnp.dot(tied_b, jnp.ones((LANES, LANES), _BF16), preferred_element_type=_F32)
    before_row = jnp.dot(row_excl, row_tot.astype(_BF16), preferred_element_type=_F32)
    tie_rank = (in_row + before_row).astype(jnp.int32)
    sel = above | (tied & (tie_rank < need))
    sel_b = jnp.where(sel, 1.0, 0.0).astype(_BF16)

    row_cnt = lax.dot_general(jnp.ones((SUBLANES, LANES), _BF16), sel_b,
                              (((1,), (1,)), ((), ())), preferred_element_type=_F32)
    incl = jnp.dot(row_cnt.astype(_BF16), row_incl_t, preferred_element_type=_F32)[0:1, :]
    excl = incl - row_cnt[0:1, :]
    a_hi = a.astype(_BF16)
    a_mid = (a - a_hi.astype(_F32)).astype(_BF16)
    a_lo = (a - a_hi.astype(_F32) - a_mid.astype(_F32)).astype(_BF16)
    row_id = lax.broadcasted_iota(jnp.int32, (1, rows), 1).astype(_F32)
    lane_id = lax.broadcasted_iota(jnp.int32, (jb, LANES), 1).astype(_F32)
    diag = (lax.broadcasted_iota(jnp.int32, (LANES, LANES), 0)
            == lax.broadcasted_iota(jnp.int32, (LANES, LANES), 1))

    for c in range(cap // jb):
        j = (lax.broadcasted_iota(jnp.int32, (jb, 1), 0) + c * jb).astype(_F32)
        hit = (excl <= j) & (j < incl)
        hit_f = jnp.where(hit, 1.0, 0.0)
        r_of_j = jnp.sum(hit_f * row_id, axis=1, keepdims=True)
        target = j + 1.0 - jnp.sum(hit_f * excl, axis=1, keepdims=True)
        row_sel = jnp.dot(hit_f.astype(_BF16), sel_b, preferred_element_type=_F32)
        within = jnp.dot(row_sel.astype(_BF16), lane_incl, preferred_element_type=_F32)
        l_of_j = jnp.sum(jnp.where(within < target, 1.0, 0.0), axis=1, keepdims=True)
        hit_b = hit_f.astype(_BF16)
        a_row = (jnp.dot(hit_b, a_hi, preferred_element_type=_F32)
                 + jnp.dot(hit_b, a_mid, preferred_element_type=_F32)
                 + jnp.dot(hit_b, a_lo, preferred_element_type=_F32))
        gate = jnp.sum(jnp.where(lane_id == l_of_j, a_row, 0.0), axis=1, keepdims=True)
        token_rep = jnp.broadcast_to(r_of_j * LANES + l_of_j, (jb, LANES))
        idx_ref[0, c * jb:(c + 1) * jb, :] = token_rep.astype(jnp.int32)
        gate_ref[0, c * jb:(c + 1) * jb, :] = jnp.broadcast_to(gate, (jb, LANES))
        for g in range(jb // LANES):
            grp = jnp.where(diag, token_rep[g * LANES:(g + 1) * LANES, :], 0.0)
            out_row = c * (jb // LANES) + g
            idxc_ref[0, out_row:out_row + 1, :] = jnp.sum(grp, axis=0, keepdims=True).astype(jnp.int32)


def _route(aff_t, cap):
    e, rows, _ = aff_t.shape
    jb = min(512, cap)
    return pl.pallas_call(
        functools.partial(_route_kernel, cap=cap, jb=jb),
        grid=(e,),
        in_specs=[pl.BlockSpec((1, rows, LANES), lambda i: (i, 0, 0))],
        out_specs=[
            pl.BlockSpec((1, cap, LANES), lambda i: (i, 0, 0)),
            pl.BlockSpec((1, cap, LANES), lambda i: (i, 0, 0)),
            pl.BlockSpec((1, cap // LANES, LANES), lambda i: (i, 0, 0)),
        ],
        out_shape=[
            jax.ShapeDtypeStruct((e, cap, LANES), jnp.int32),
            jax.ShapeDtypeStruct((e, cap, LANES), _F32),
            jax.ShapeDtypeStruct((e, cap // LANES, LANES), jnp.int32),
        ],
        compiler_params=_params(("arbitrary",)),
        name="route",
    )(aff_t)


def _experts_kernel(idx_hbm, xin_hbm, tok_ref, gate_ref, gf_ref, wg_ref, wu_ref, wd_ref,
                    xh_hbm, idx_p, idx_a, idx_b, idx_n, idx_m, prev_first, buf_a, buf_b, wb_a, wb_b, sems,
                    *, rows, nsteps, seq_shift, nbatch, col_chunk):
    del xin_hbm
    d = D_MODEL
    e = pl.program_id(0)
    j = pl.program_id(1)
    step = e * nsteps + j
    last_step = pl.num_programs(0) * nsteps - 1
    blk_a = 2 * step
    blk_p = jnp.maximum(blk_a - 1, 0)
    last_blk = 2 * last_step + 1
    blk_n = jnp.minimum(blk_a + 2, last_blk)
    blk_m = jnp.minimum(blk_a + 3, last_blk)
    SEM_IDX, SEM_GA, SEM_GB, SEM_SA, SEM_SB, SEM_FIX = range(6)

    idx_copies = [pltpu.make_async_copy(idx_hbm.at[b], ref, sems.at[SEM_IDX])
                  for b, ref in ((blk_p, idx_p), (blk_a, idx_a), (blk_a + 1, idx_b), (blk_n, idx_n),
                                 (blk_m, idx_m))]
    for c in idx_copies:
        c.start()
    for c in idx_copies:
        c.wait()

    res_cols = pl.ds(0, d)
    moe_cols = pl.ds(d, d)

    def row_copy(idx_ref, r, buf, sem, kind):
        tok = idx_ref[0, r]
        if kind == "store":
            return pltpu.make_async_copy(buf.at[pl.ds(r, 1)], xh_hbm.at[pl.ds(tok, 1), res_cols], sems.at[sem])
        if kind == "fetch":
            return pltpu.make_async_copy(xh_hbm.at[pl.ds(tok, 1)], buf.at[pl.ds(r, 1)], sems.at[sem])
        cols = moe_cols if kind == "fetch_moe" else res_cols
        return pltpu.make_async_copy(xh_hbm.at[pl.ds(tok, 1), cols], buf.at[pl.ds(r, 1), cols], sems.at[sem])

    def start_all(idx_ref, buf, sem, kind):
        def body(r, carry):
            row_copy(idx_ref, r, buf, sem, kind).start()
            return carry
        lax.fori_loop(0, rows, body, 0)

    def wait_all(buf, sem, kind):
        all_rows = pl.ds(0, rows)
        if kind == "store":
            pltpu.make_async_copy(buf, xh_hbm.at[all_rows, res_cols], sems.at[sem]).wait()
        elif kind == "fetch":
            pltpu.make_async_copy(xh_hbm.at[all_rows], buf, sems.at[sem]).wait()
        else:
            cols = moe_cols if kind == "fetch_moe" else res_cols
            pltpu.make_async_copy(xh_hbm.at[all_rows, cols], buf.at[:, cols], sems.at[sem]).wait()

    def block(part, buf, sem_g, wb, streams, fetched=False):
        if not fetched:
            wait_all(buf, sem_g, "fetch")
        todo = [functools.partial(row_copy, stream[0], r, *stream[1:])
                for r in range(rows) for stream in streams]
        nch = D_MODEL // col_chunk
        per_call = -(-len(todo) // (3 * nch))

        def start_some():
            for k, make in enumerate(todo[:per_call]):
                make().start(priority=k % 2)
            del todo[:per_call]

        x = buf[:, d:].astype(_BF16)
        hid = []
        for c in range(nch):
            cs = slice(c * col_chunk, (c + 1) * col_chunk)
            hg = jnp.dot(x, wg_ref[0, 0, :, cs].astype(_BF16), preferred_element_type=_F32)
            start_some()
            hu = jnp.dot(x, wu_ref[0, 0, :, cs].astype(_BF16), preferred_element_type=_F32)
            start_some()
            hid.append((hg * jax.nn.sigmoid(hg) * hu).astype(_BF16))
        hid = jnp.concatenate(hid, axis=1)
        rs = slice(part * rows, (part + 1) * rows)
        seq_of_row = tok_ref[rs, :] >> seq_shift
        gate = gate_ref[rs, :]
        upd = []
        for c in range(nch):
            cs = slice(c * col_chunk, (c + 1) * col_chunk)
            y = jnp.dot(hid, wd_ref[0, 0, :, cs].astype(_BF16), preferred_element_type=_F32)
            start_some()
            seq_c = jnp.concatenate([seq_of_row] * (col_chunk // LANES), axis=1)
            gf = jnp.zeros_like(y)
            for b in range(nbatch):
                gf = jnp.where(seq_c == b, gf_ref[b:b + 1, cs], gf)
            upd.append(gf * (y * jnp.concatenate([gate] * (col_chunk // LANES), axis=1)))
        assert not todo
        wb[...] = buf[:, :d] + jnp.concatenate(upd, axis=1)

    fetch_b = (idx_b, buf_b, SEM_GB, "fetch")
    fetch_n = (idx_n, buf_a, SEM_GA, "fetch")
    fetch_n_moe = (idx_n, buf_a, SEM_GA, "fetch_moe")
    store_p = (idx_p, wb_b, SEM_SB, "store")
    store_a = (idx_a, wb_a, SEM_SA, "store")
    store_b = (idx_b, wb_b, SEM_SB, "store")

    @pl.when(step == 0)
    def _():
        start_all(idx_a, buf_a, SEM_GA, "fetch")
        prev_first[0] = 0

    @pl.when(step > 0)
    def _():
        wait_all(wb_a, SEM_SA, "store")

    def overlaps(first, last, other_first, other_last):
        return jnp.logical_not((other_last < first) | (last < other_first))

    serial = (step == 0) | overlaps(idx_a[0, 0], idx_b[0, rows - 1], prev_first[0], idx_p[0, rows - 1])
    next_serial = overlaps(idx_n[0, 0], idx_m[0, rows - 1], idx_a[0, 0], idx_b[0, rows - 1])
    prev_first[0] = idx_a[0, 0]

    @pl.when(serial)
    def _():
        @pl.when(step == 0)
        def _():
            wait_all(buf_a, SEM_GA, "fetch")

        @pl.when(step > 0)
        def _():
            start_all(*store_p)
            wait_all(wb_b, SEM_SB, "store")
            wait_all(buf_a, SEM_GA, "fetch_moe")
            start_all(idx_a, buf_a, SEM_FIX, "fetch_res")
            wait_all(buf_a, SEM_FIX, "fetch_res")
        block(0, buf_a, SEM_GA, wb_a, [fetch_b], fetched=True)

    @pl.when(jnp.logical_not(serial))
    def _():
        block(0, buf_a, SEM_GA, wb_a, [store_p, fetch_b])
        wait_all(wb_b, SEM_SB, "store")

    @pl.when(next_serial)
    def _():
        block(1, buf_b, SEM_GB, wb_b, [store_a, fetch_n_moe])

    @pl.when(jnp.logical_not(next_serial))
    def _():
        block(1, buf_b, SEM_GB, wb_b, [store_a, fetch_n])

    @pl.when(step == last_step)
    def _():
        wait_all(buf_a, SEM_GA, "fetch_moe")
        start_all(*store_b)
        wait_all(wb_a, SEM_SA, "store")
        wait_all(wb_b, SEM_SB, "store")


def _experts(idx_blocks, xh, tok_rep, gate_rep, gate_f, w_gate, w_up, w_down, layer, seq, rows):
    n, d = xh.shape[0], D_MODEL
    e = w_gate.shape[1]
    cap = tok_rep.shape[0] // e
    nsteps = cap // (2 * rows)
    nbatch = gate_f.shape[0]
    seq_shift = seq.bit_length() - 1
    assert 1 << seq_shift == seq and nsteps * 2 * rows == cap
    wspec = pl.BlockSpec((1, 1, d, d), lambda i, j: (layer, i, 0, 0))
    any_spec = pl.BlockSpec(memory_space=pl.ANY)
    return pl.pallas_call(
        functools.partial(_experts_kernel, rows=rows, nsteps=nsteps, seq_shift=seq_shift, nbatch=nbatch,
                          col_chunk=EXPERT_COL_CHUNK),
        grid=(e, nsteps),
        in_specs=[
            any_spec, any_spec,
            pl.BlockSpec((2 * rows, LANES), lambda i, j: (i * nsteps + j, 0)),
            pl.BlockSpec((2 * rows, LANES), lambda i, j: (i * nsteps + j, 0)),
            pl.BlockSpec((nbatch, d), lambda i, j: (0, 0)),
            wspec, wspec, wspec,
        ],
        out_specs=any_spec,
        out_shape=jax.ShapeDtypeStruct((n, 2 * d), _F32),
        scratch_shapes=[
            pltpu.SMEM((1, rows), jnp.int32),
            pltpu.SMEM((1, rows), jnp.int32),
            pltpu.SMEM((1, rows), jnp.int32),
            pltpu.SMEM((1, rows), jnp.int32),
            pltpu.SMEM((1, rows), jnp.int32),
            pltpu.SMEM((1,), jnp.int32),
            pltpu.VMEM((rows, 2 * d), _F32),
            pltpu.VMEM((rows, 2 * d), _F32),
            pltpu.VMEM((rows, d), _F32),
            pltpu.VMEM((rows, d), _F32),
            pltpu.SemaphoreType.DMA((6,)),
        ],
        input_output_aliases={1: 0},
        compiler_params=_params(("arbitrary", "arbitrary")),
        name="experts",
    )(idx_blocks, xh, tok_rep, gate_rep, gate_f, w_gate, w_up, w_down)


def _final_kernel(x_ref, g_ref, o_ref):
    o_ref[0] = _rms(x_ref[0]) * g_ref[...]


def _final_norm(x, g, tile):
    b, s, d = x.shape[0], x.shape[1], D_MODEL
    return pl.pallas_call(
        _final_kernel,
        grid=(b, s // tile),
        in_specs=[pl.BlockSpec((1, tile, d), lambda i, t: (i, t, 0)),
                  pl.BlockSpec((1, d), lambda i, t: (0, 0))],
        out_specs=pl.BlockSpec((1, tile, d), lambda i, t: (i, t, 0)),
        out_shape=jax.ShapeDtypeStruct((b, s, d), _F32),
        compiler_params=_params(("arbitrary", "arbitrary")),
        name="final_norm",
    )(x, g)


def _channel_dft_table():
    k = jnp.arange(HEAD_DIM, dtype=jnp.int32)
    ang = ((k[:, None] * k[None, :]) % HEAD_DIM).astype(_F32) * (2.0 * math.pi / HEAD_DIM)
    groups = FOURIER_W // HEAD_DIM
    eye = jnp.eye(groups, dtype=_F32)
    c = jnp.kron(eye, jnp.cos(ang)) / math.sqrt(HEAD_DIM)
    s = jnp.kron(eye, jnp.sin(ang)) / math.sqrt(HEAD_DIM)
    return jnp.concatenate([c, s], axis=1).astype(_BF16)


def _split_hi_lo(w):
    hi = w.astype(_BF16)
    lo = (w - hi.astype(_F32)).astype(_BF16)
    return jnp.concatenate([hi, lo], axis=-1)


def _seq_dft_tables(s):
    radix = 64
    assert s % radix == 0
    k = jnp.arange(s, dtype=jnp.int32)[:, None]
    t1 = jnp.arange(s // radix, dtype=jnp.int32)[None, :]
    t0 = jnp.arange(radix, dtype=jnp.int32)[None, :]
    ang_a = ((k * t1 * radix) % s).astype(_F32) * (2.0 * math.pi / s)
    ang_b = ((k * t0) % s).astype(_F32) * (2.0 * math.pi / s)
    ca, sa = jnp.cos(ang_a)[:, :, None], jnp.sin(ang_a)[:, :, None]
    cb, sb = jnp.cos(ang_b)[:, None, :], jnp.sin(ang_b)[:, None, :]
    cos = (ca * cb - sa * sb).reshape(s, s)
    nsin = (-(sa * cb + ca * sb)).reshape(s, s)
    return cos.astype(_BF16), nsin.astype(_BF16)


def _encoder(x, mod, p, dft64, seq_tables, tile, erows=None):
    b, s, d = x.shape
    n = b * s
    cap = CAPACITY_FACTOR * n // N_EXPERTS
    if erows is None:
        erows = min(EXPERT_MAX_ROWS, cap // 4)
    assert cap % (2 * erows) == 0 and cap // erows >= 2
    wc, wsn = seq_tables
    depth = p["w_in"].shape[0]
    for l in range(depth):
        mod_l = mod[l][:, None, :]
        yac, fc, fs = _mix_in(x, mod_l, p["g_mix"][l][None], p["w_in_bf"][l], p["conv_w"][l],
                              p["g_sgu"][l][None], p["w_sp_bf"][l], p["b_sp_rows"][l],
                              p["g_grp"][l][None], dft64, tile)
        yf = _seq_dft(wc, wsn, fc, fs)
        xh, aff = _mix_out(x, yac, yf, mod_l, p["g_grp"][l][None], p["w_out_bf"][l],
                           p["g_ffn"][l][None], p["w_router_hl"][l], tile)
        aff_t = aff.reshape(n, N_EXPERTS).T.reshape(N_EXPERTS, n // LANES, LANES)
        tok_rep, gate_rep, idx = _route(aff_t, cap)
        idx_blocks = idx.reshape(N_EXPERTS * cap // erows, 1, erows)
        gate_f = mod_l[:, 0, 5 * d:6 * d]
        x = _experts(idx_blocks, xh.reshape(n, 2 * d),
                     tok_rep.reshape(N_EXPERTS * cap, LANES), gate_rep.reshape(N_EXPERTS * cap, LANES),
                     gate_f, p["w_gate"], p["w_up"], p["w_down"], l, s, erows).reshape(b, s, 2 * d)
    return _final_norm(x, p["g_final"][None], tile)


def kernel(x_prompt, x_sample, c_prompt, c_sample, w_ada, b_ada, g_mix, w_in, conv_w, g_sgu, w_spatial,
           b_spatial, g_grp, w_out, g_ffn, w_router, w_gate, w_up, w_down, g_final):
    nb = x_prompt.shape[0]
    mod = _ada(jnp.concatenate([c_prompt, c_sample], axis=0), w_ada, b_ada)
    p = dict(
        g_mix=g_mix, conv_w=conv_w, g_sgu=g_sgu, g_grp=g_grp, g_ffn=g_ffn, w_router_hl=_split_hi_lo(w_router),
        w_gate=w_gate, w_up=w_up, w_down=w_down, g_final=g_final, w_in=w_in,
        w_in_bf=w_in.astype(_BF16), w_out_bf=w_out.astype(_BF16), w_sp_bf=w_spatial.astype(_BF16),
        b_sp_rows=jnp.repeat(jnp.swapaxes(b_spatial, 1, 2), HEAD_DIM, axis=2),
    )
    dft64 = _channel_dft_table()
    tile = min(512, x_sample.shape[1])
    s_long, s_short = x_prompt.shape[1], x_sample.shape[1]
    assert s_long % s_short == 0
    tables = _seq_dft_tables(s_long)
    tables_short = tuple(w[::s_long // s_short, :s_short] for w in tables)
    y_prompt = _encoder(x_prompt, mod[:, :nb], p, dft64, tables, tile)
    y_sample = _encoder(x_sample, mod[:, nb:], p, dft64, tables_short, tile)
    return (y_prompt, y_sample)
```

```python
import functools
import math

import jax
import jax.numpy as jnp
from jax import lax
from jax.experimental import pallas as pl
from jax.experimental.pallas import tpu as pltpu

D_MODEL = 1024
HEAD_DIM = 64
CONV_W = 384
FOURIER_W = 256
SGU_W = 384
SGU_HEADS = SGU_W // HEAD_DIM
CHUNK = 128
IN_W = 3 * CONV_W + FOURIER_W + 2 * SGU_W
N_EXPERTS = 16
CAPACITY_FACTOR = 2
N_MOD = 6
EPS = 1e-6

LANES = 128
SUBLANES = 8
VMEM_LIMIT_BYTES = 52 * 1024 * 1024
MXU_COLS = 256
EXPERT_COL_CHUNK = MXU_COLS
EXPERT_MAX_ROWS = 512

_HI = lax.Precision.HIGHEST
_BF16 = jnp.bfloat16
_F32 = jnp.float32


def _rms(x):
    return x * lax.rsqrt(jnp.mean(x * x, axis=-1, keepdims=True) + EPS)


def _params(sem, vmem=VMEM_LIMIT_BYTES):
    return pltpu.CompilerParams(dimension_semantics=sem, vmem_limit_bytes=vmem)


def _ada_kernel(c_ref, w_ref, b_ref, o_ref):
    c = c_ref[...]
    a = c * jax.nn.sigmoid(c)
    o_ref[0] = jnp.dot(a, w_ref[0], precision=_HI, preferred_element_type=_F32) + b_ref[0]


def _ada(c_all, w_ada, b_ada):
    depth, d, m = w_ada.shape
    nb = c_all.shape[0]
    tn = 1536
    return pl.pallas_call(
        _ada_kernel,
        grid=(depth, m // tn),
        in_specs=[
            pl.BlockSpec((nb, d), lambda l, j: (0, 0)),
            pl.BlockSpec((1, d, tn), lambda l, j: (l, 0, j)),
            pl.BlockSpec((1, 1, tn), lambda l, j: (l, 0, j)),
        ],
        out_specs=pl.BlockSpec((1, nb, tn), lambda l, j: (l, 0, j)),
        out_shape=jax.ShapeDtypeStruct((depth, nb, m), _F32),
        compiler_params=_params(("arbitrary", "arbitrary")),
        name="ada",
    )(c_all, w_ada, b_ada.reshape(depth, 1, m))


def _mix_in_kernel(x_ref, xp_ref, xn_ref, mod_ref, gmix_ref, win_ref, convw_ref, gsgu_ref,
                   wsp_ref, bsp_ref, ggrp_ref, dft_ref, yac_ref, fc_ref, fs_ref, *, tile):
    t = pl.program_id(1)
    nt = pl.num_programs(1)
    shift = mod_ref[0, :, 0:D_MODEL]
    scale = mod_ref[0, :, D_MODEL:2 * D_MODEL]
    gain = gmix_ref[...] * (1.0 + scale)

    def modulated(x):
        return (_rms(x) * gain + shift).astype(_BF16)

    h = modulated(x_ref[0])
    z = jnp.dot(h, win_ref[...], preferred_element_type=_F32)

    def halo(xh_ref):
        zh = jnp.dot(modulated(xh_ref[0]), win_ref[:, CONV_W:3 * CONV_W],
                     preferred_element_type=_F32)
        return zh[:, :CONV_W] * zh[:, CONV_W:]

    a_before = jnp.where(t > 0, halo(xp_ref)[SUBLANES - 1:SUBLANES, :], 0.0)
    a_after = jnp.where(t < nt - 1, halo(xn_ref)[0:1, :], 0.0)

    zb = z[:, 0:CONV_W]
    a = z[:, CONV_W:2 * CONV_W] * z[:, 2 * CONV_W:3 * CONV_W]
    row = lax.broadcasted_iota(jnp.int32, (tile, CONV_W), 0)
    a_prev = jnp.where(row == 0, a_before, pltpu.roll(a, 1, axis=0))
    a_next = jnp.where(row == tile - 1, a_after, pltpu.roll(a, tile - 1, axis=0))
    cw = convw_ref[...]
    conv = zb * (cw[0:1, :] * a_prev + cw[1:2, :] * a + cw[2:3, :] * a_next)
    ya = (_rms(conv) * ggrp_ref[:, 0:CONV_W]).astype(_BF16)

    zf = z[:, 3 * CONV_W:3 * CONV_W + FOURIER_W].astype(_BF16)
    f2 = jnp.dot(zf, dft_ref[...], preferred_element_type=_F32)
    fc_ref[...] = f2[:, :FOURIER_W].astype(_BF16)
    fs_ref[...] = f2[:, FOURIER_W:].astype(_BF16)

    off = 3 * CONV_W + FOURIER_W
    u = jax.nn.gelu(z[:, off:off + SGU_W])
    v = (_rms(jax.nn.gelu(z[:, off + SGU_W:off + 2 * SGU_W])) * gsgu_ref[...]).astype(_BF16)
    lane = lax.broadcasted_iota(jnp.int32, (CHUNK, LANES), 1)
    first_head = lane < HEAD_DIM
    chunks = []
    for c in range(tile // CHUNK):
        pairs = []
        for k in range(SGU_W // LANES):
            vk = v[c * CHUNK:(c + 1) * CHUNK, k * LANES:(k + 1) * LANES]
            r0 = jnp.dot(wsp_ref[2 * k], vk, preferred_element_type=_F32)
            r1 = jnp.dot(wsp_ref[2 * k + 1], vk, preferred_element_type=_F32)
            pairs.append(jnp.where(first_head, r0, r1))
        chunks.append(jnp.concatenate(pairs, axis=1) + bsp_ref[...])
    sv = jnp.concatenate(chunks, axis=0)
    yc = (_rms(u * sv) * ggrp_ref[:, CONV_W + FOURIER_W:]).astype(_BF16)
    yac_ref[0] = jnp.concatenate([ya, yc], axis=1)


def _mix_in(x, mod_l, gmix, win_bf, convw, gsgu, wsp_bf, bsp_rows, ggrp, dft64, tile):
    b, s, d = x.shape[0], x.shape[1], D_MODEL
    nt = s // tile
    hb = tile // SUBLANES
    last_hb = s // SUBLANES - 1
    const2 = lambda i, t: (0, 0)
    return pl.pallas_call(
        functools.partial(_mix_in_kernel, tile=tile),
        grid=(b, nt),
        in_specs=[
            pl.BlockSpec((1, tile, d), lambda i, t: (i, t, 0)),
            pl.BlockSpec((1, SUBLANES, d), lambda i, t: (i, jnp.maximum(t * hb - 1, 0), 0)),
            pl.BlockSpec((1, SUBLANES, d), lambda i, t: (i, jnp.minimum((t + 1) * hb, last_hb), 0)),
            pl.BlockSpec((1, 1, N_MOD * d), lambda i, t: (i, 0, 0)),
            pl.BlockSpec((1, d), const2),
            pl.BlockSpec((d, IN_W), const2),
            pl.BlockSpec((3, CONV_W), const2),
            pl.BlockSpec((1, SGU_W), const2),
            pl.BlockSpec((SGU_HEADS, CHUNK, CHUNK), lambda i, t: (0, 0, 0)),
            pl.BlockSpec((CHUNK, SGU_W), const2),
            pl.BlockSpec((1, d), const2),
            pl.BlockSpec((FOURIER_W, 2 * FOURIER_W), const2),
        ],
        out_specs=[
            pl.BlockSpec((1, tile, CONV_W + SGU_W), lambda i, t: (i, t, 0)),
            pl.BlockSpec((tile, FOURIER_W), lambda i, t: (t, i)),
            pl.BlockSpec((tile, FOURIER_W), lambda i, t: (t, i)),
        ],
        out_shape=[
            jax.ShapeDtypeStruct((b, s, CONV_W + SGU_W), _BF16),
            jax.ShapeDtypeStruct((s, b * FOURIER_W), _BF16),
            jax.ShapeDtypeStruct((s, b * FOURIER_W), _BF16),
        ],
        compiler_params=_params(("arbitrary", "arbitrary")),
        name="mix_in",
    )(x, x, x, mod_l, gmix, win_bf, convw, gsgu, wsp_bf, bsp_rows, ggrp, dft64)


def _seq_dft_kernel(wc_ref, ws_ref, fc_ref, fs_ref, o_ref):
    part = (jnp.dot(wc_ref[...], fc_ref[...], preferred_element_type=_F32)
            + jnp.dot(ws_ref[...], fs_ref[...], preferred_element_type=_F32))

    @pl.when(pl.program_id(2) == 0)
    def _():
        o_ref[...] = part

    @pl.when(pl.program_id(2) > 0)
    def _():
        o_ref[...] += part


def _seq_dft(wc, wsn, fc, fs):
    s = wc.shape[0]
    n = fc.shape[1]
    tm = min(1024, s)
    tn = min(1024, n)
    tk = min(1024, s)
    return pl.pallas_call(
        _seq_dft_kernel,
        grid=(s // tm, n // tn, s // tk),
        in_specs=[
            pl.BlockSpec((tm, tk), lambda i, j, k: (i, k)),
            pl.BlockSpec((tm, tk), lambda i, j, k: (i, k)),
            pl.BlockSpec((tk, tn), lambda i, j, k: (k, j)),
            pl.BlockSpec((tk, tn), lambda i, j, k: (k, j)),
        ],
        out_specs=pl.BlockSpec((tm, tn), lambda i, j, k: (i, j)),
        out_shape=jax.ShapeDtypeStruct((s, n), _F32),
        compiler_params=_params(("arbitrary", "arbitrary", "arbitrary")),
        name="seq_dft",
    )(wc, wsn, fc, fs)


def _mix_out_kernel(x_ref, yac_ref, yf_ref, mod_ref, ggrp_ref, wout_ref, gffn_ref, wr_ref,
                    xh_ref, aff_ref, *, seq_scale):
    d = D_MODEL
    gate_m = mod_ref[0, :, 2 * d:3 * d]
    shift_f = mod_ref[0, :, 3 * d:4 * d]
    scale_f = mod_ref[0, :, 4 * d:5 * d]
    yf = (_rms(yf_ref[...] * seq_scale) * ggrp_ref[:, CONV_W:CONV_W + FOURIER_W]).astype(_BF16)
    yac = yac_ref[0]
    ycat = jnp.concatenate([yac[:, :CONV_W], yf, yac[:, CONV_W:]], axis=1)
    mix = jnp.dot(ycat, wout_ref[...], preferred_element_type=_F32)
    x1 = x_ref[0] + gate_m * mix
    h2 = _rms(x1) * (gffn_ref[...] * (1.0 + scale_f)) + shift_f
    xh_ref[0, :, :d] = x1
    xh_ref[0, :, d:] = h2
    h_hi = h2.astype(_BF16)
    h_lo = (h2 - h_hi.astype(_F32)).astype(_BF16)
    by_hi = jnp.dot(h_hi, wr_ref[...], preferred_element_type=_F32)
    logits = (by_hi[:, :N_EXPERTS] + (by_hi[:, N_EXPERTS:]
              + jnp.dot(h_lo, wr_ref[:, :N_EXPERTS], preferred_element_type=_F32)))
    e = jnp.exp(logits - jnp.max(logits, axis=-1, keepdims=True))
    aff_ref[0] = e / jnp.sum(e, axis=-1, keepdims=True)


def _mix_out(x, yac, yf, mod_l, ggrp, wout_bf, gffn, w_router, tile):
    b, s, d = x.shape[0], x.shape[1], D_MODEL
    const2 = lambda i, t: (0, 0)
    return pl.pallas_call(
        functools.partial(_mix_out_kernel, seq_scale=1.0 / math.sqrt(s)),
        grid=(b, s // tile),
        in_specs=[
            pl.BlockSpec((1, tile, d), lambda i, t: (i, t, 0)),
            pl.BlockSpec((1, tile, CONV_W + SGU_W), lambda i, t: (i, t, 0)),
            pl.BlockSpec((tile, FOURIER_W), lambda i, t: (t, i)),
            pl.BlockSpec((1, 1, N_MOD * d), lambda i, t: (i, 0, 0)),
            pl.BlockSpec((1, d), const2),
            pl.BlockSpec((d, d), const2),
            pl.BlockSpec((1, d), const2),
            pl.BlockSpec((d, 2 * N_EXPERTS), const2),
        ],
        out_specs=[
            pl.BlockSpec((1, tile, 2 * d), lambda i, t: (i, t, 0)),
            pl.BlockSpec((1, tile, N_EXPERTS), lambda i, t: (i, t, 0)),
        ],
        out_shape=[
            jax.ShapeDtypeStruct((b, s, 2 * d), _F32),
            jax.ShapeDtypeStruct((b, s, N_EXPERTS), _F32),
        ],
        compiler_params=_params(("arbitrary", "arbitrary")),
        name="mix_out",
    )(x, yac, yf, mod_l, ggrp, wout_bf, gffn, w_router)


def _cumsum_mats(rows):
    qi = lax.broadcasted_iota(jnp.int32, (LANES, LANES), 0)
    qj = lax.broadcasted_iota(jnp.int32, (LANES, LANES), 1)
    ri = lax.broadcasted_iota(jnp.int32, (rows, rows), 0)
    rj = lax.broadcasted_iota(jnp.int32, (rows, rows), 1)
    def ones_where(mask):
        return jnp.where(mask, 1.0, 0.0).astype(_BF16)

    lane_excl = ones_where(qi < qj)
    lane_incl = ones_where(qi <= qj)
    row_excl = ones_where(rj < ri)
    row_incl_t = ones_where(ri <= rj)
    return lane_excl, lane_incl, row_excl, row_incl_t


def _route_kernel(aff_ref, idx_ref, gate_ref, idxc_ref, *, cap, jb):
    a = aff_ref[0]
    rows = a.shape[0]
    lane_excl, lane_incl, row_excl, row_incl_t = _cumsum_mats(rows)

    def count(mask):
        return jnp.sum(jnp.where(mask, 1.0, 0.0)).astype(jnp.int32)

    def refine(thr_bits, shift, nbits):
        passing = jnp.int32(0)
        for m in range(1, 1 << nbits):
            cand_f = lax.bitcast_convert_type(thr_bits | (jnp.int32(m) << shift), _F32)
            passing += (count(a >= cand_f) >= cap).astype(jnp.int32)
        return thr_bits | (passing << shift)

    thr_bits = lax.fori_loop(0, 10, lambda i, t: refine(t, 27 - 3 * i, 3), jnp.int32(0))
    thr = lax.bitcast_convert_type(thr_bits, _F32)
    above = a > thr
    tied = a == thr
    need = cap - count(above)

    tied_b = jnp.where(tied, 1.0, 0.0).astype(_BF16)
    in_row = jnp.dot(tied_b, lane_excl, preferred_element_type=_F32)
    row_tot = jnp.dot(tied_b, jnp.ones((LANES, LANES), _BF16), preferred_element_type=_F32)
    before_row = jnp.dot(row_excl, row_tot.astype(_BF16), preferred_element_type=_F32)
    tie_rank = (in_row + before_row).astype(jnp.int32)
    sel = above | (tied & (tie_rank < need))
    sel_b = jnp.where(sel, 1.0, 0.0).astype(_BF16)

    row_cnt = lax.dot_general(jnp.ones((SUBLANES, LANES), _BF16), sel_b,
                              (((1,), (1,)), ((), ())), preferred_element_type=_F32)
    incl = jnp.dot(row_cnt.astype(_BF16), row_incl_t, preferred_element_type=_F32)[0:1, :]
    excl = incl - row_cnt[0:1, :]
    a_hi = a.astype(_BF16)
    a_mid = (a - a_hi.astype(_F32)).astype(_BF16)
    a_lo = (a - a_hi.astype(_F32) - a_mid.astype(_F32)).astype(_BF16)
    row_id = lax.broadcasted_iota(jnp.int32, (1, rows), 1).astype(_F32)
    lane_id = lax.broadcasted_iota(jnp.int32, (jb, LANES), 1).astype(_F32)
    diag = (lax.broadcasted_iota(jnp.int32, (LANES, LANES), 0)
            == lax.broadcasted_iota(jnp.int32, (LANES, LANES), 1))

    for c in range(cap // jb):
        j = (lax.broadcasted_iota(jnp.int32, (jb, 1), 0) + c * jb).astype(_F32)
        hit = (excl <= j) & (j < incl)
        hit_f = jnp.where(hit, 1.0, 0.0)
        r_of_j = jnp.sum(hit_f * row_id, axis=1, keepdims=True)
        target = j + 1.0 - jnp.sum(hit_f * excl, axis=1, keepdims=True)
        row_sel = jnp.dot(hit_f.astype(_BF16), sel_b, preferred_element_type=_F32)
        within = jnp.dot(row_sel.astype(_BF16), lane_incl, preferred_element_type=_F32)
        l_of_j = jnp.sum(jnp.where(within < target, 1.0, 0.0), axis=1, keepdims=True)
        hit_b = hit_f.astype(_BF16)
        a_row = (jnp.dot(hit_b, a_hi, preferred_element_type=_F32)
                 + jnp.dot(hit_b, a_mid, preferred_element_type=_F32)
                 + jnp.dot(hit_b, a_lo, preferred_element_type=_F32))
        gate = jnp.sum(jnp.where(lane_id == l_of_j, a_row, 0.0), axis=1, keepdims=True)
        token_rep = jnp.broadcast_to(r_of_j * LANES + l_of_j, (jb, LANES))
        idx_ref[0, c * jb:(c + 1) * jb, :] = token_rep.astype(jnp.int32)
        gate_ref[0, c * jb:(c + 1) * jb, :] = jnp.broadcast_to(gate, (jb, LANES))
        for g in range(jb // LANES):
            grp = jnp.where(diag, token_rep[g * LANES:(g + 1) * LANES, :], 0.0)
            out_row = c * (jb // LANES) + g
            idxc_ref[0, out_row:out_row + 1, :] = jnp.sum(grp, axis=0, keepdims=True).astype(jnp.int32)


def _route(aff_t, cap):
    e, rows, _ = aff_t.shape
    jb = min(512, cap)
    return pl.pallas_call(
        functools.partial(_route_kernel, cap=cap, jb=jb),
        grid=(e,),
        in_specs=[pl.BlockSpec((1, rows, LANES), lambda i: (i, 0, 0))],
        out_specs=[
            pl.BlockSpec((1, cap, LANES), lambda i: (i, 0, 0)),
            pl.BlockSpec((1, cap, LANES), lambda i: (i, 0, 0)),
            pl.BlockSpec((1, cap // LANES, LANES), lambda i: (i, 0, 0)),
        ],
        out_shape=[
            jax.ShapeDtypeStruct((e, cap, LANES), jnp.int32),
            jax.ShapeDtypeStruct((e, cap, LANES), _F32),
            jax.ShapeDtypeStruct((e, cap // LANES, LANES), jnp.int32),
        ],
        compiler_params=_params(("arbitrary",)),
        name="route",
    )(aff_t)


def _experts_kernel(idx_hbm, xin_hbm, tok_ref, gate_ref, gf_ref, wg_ref, wu_ref, wd_ref,
                    xh_hbm, idx_p, idx_a, idx_b, idx_n, idx_m, prev_first, buf_a, buf_b, wb_a, wb_b, sems,
                    *, rows, nsteps, seq_shift, nbatch, col_chunk):
    del xin_hbm
    d = D_MODEL
    e = pl.program_id(0)
    j = pl.program_id(1)
    step = e * nsteps + j
    last_step = pl.num_programs(0) * nsteps - 1
    blk_a = 2 * step
    blk_p = jnp.maximum(blk_a - 1, 0)
    last_blk = 2 * last_step + 1
    blk_n = jnp.minimum(blk_a + 2, last_blk)
    blk_m = jnp.minimum(blk_a + 3, last_blk)
    SEM_IDX, SEM_GA, SEM_GB, SEM_SA, SEM_SB, SEM_FIX = range(6)

    idx_copies = [pltpu.make_async_copy(idx_hbm.at[b], ref, sems.at[SEM_IDX])
                  for b, ref in ((blk_p, idx_p), (blk_a, idx_a), (blk_a + 1, idx_b), (blk_n, idx_n),
                                 (blk_m, idx_m))]
    for c in idx_copies:
        c.start()
    for c in idx_copies:
        c.wait()

    res_cols = pl.ds(0, d)
    moe_cols = pl.ds(d, d)

    def row_copy(idx_ref, r, buf, sem, kind):
        tok = idx_ref[0, r]
        if kind == "store":
            return pltpu.make_async_copy(buf.at[pl.ds(r, 1)], xh_hbm.at[pl.ds(tok, 1), res_cols], sems.at[sem])
        if kind == "fetch":
            return pltpu.make_async_copy(xh_hbm.at[pl.ds(tok, 1)], buf.at[pl.ds(r, 1)], sems.at[sem])
        cols = moe_cols if kind == "fetch_moe" else res_cols
        return pltpu.make_async_copy(xh_hbm.at[pl.ds(tok, 1), cols], buf.at[pl.ds(r, 1), cols], sems.at[sem])

    def start_all(idx_ref, buf, sem, kind):
        def body(r, carry):
            row_copy(idx_ref, r, buf, sem, kind).start()
            return carry
        lax.fori_loop(0, rows, body, 0)

    def wait_all(buf, sem, kind):
        all_rows = pl.ds(0, rows)
        if kind == "store":
            pltpu.make_async_copy(buf, xh_hbm.at[all_rows, res_cols], sems.at[sem]).wait()
        elif kind == "fetch":
            pltpu.make_async_copy(xh_hbm.at[all_rows], buf, sems.at[sem]).wait()
        else:
            cols = moe_cols if kind == "fetch_moe" else res_cols
            pltpu.make_async_copy(xh_hbm.at[all_rows, cols], buf.at[:, cols], sems.at[sem]).wait()

    def block(part, buf, sem_g, wb, streams, fetched=False):
        if not fetched:
            wait_all(buf, sem_g, "fetch")
        todo = [functools.partial(row_copy, stream[0], r, *stream[1:])
                for r in range(rows) for stream in streams]
        nch = D_MODEL // col_chunk
        per_call = -(-len(todo) // (3 * nch))

        def start_some():
            for k, make in enumerate(todo[:per_call]):
                make().start(priority=k % 2)
            del todo[:per_call]

        x = buf[:, d:].astype(_BF16)
        hid = []
        for c in range(nch):
            cs = slice(c * col_chunk, (c + 1) * col_chunk)
            hg = jnp.dot(x, wg_ref[0, 0, :, cs].astype(_BF16), preferred_element_type=_F32)
            start_some()
            hu = jnp.dot(x, wu_ref[0, 0, :, cs].astype(_BF16), preferred_element_type=_F32)
            start_some()
            hid.append((hg * jax.nn.sigmoid(hg) * hu).astype(_BF16))
        hid = jnp.concatenate(hid, axis=1)
        rs = slice(part * rows, (part + 1) * rows)
        seq_of_row = tok_ref[rs, :] >> seq_shift
        gate = gate_ref[rs, :]
        upd = []
        for c in range(nch):
            cs = slice(c * col_chunk, (c + 1) * col_chunk)
            y = jnp.dot(hid, wd_ref[0, 0, :, cs].astype(_BF16), preferred_element_type=_F32)
            start_some()
            seq_c = jnp.concatenate([seq_of_row] * (col_chunk // LANES), axis=1)
            gf = jnp.zeros_like(y)
            for b in range(nbatch):
                gf = jnp.where(seq_c == b, gf_ref[b:b + 1, cs], gf)
            upd.append(gf * (y * jnp.concatenate([gate] * (col_chunk // LANES), axis=1)))
        assert not todo
        wb[...] = buf[:, :d] + jnp.concatenate(upd, axis=1)

    fetch_b = (idx_b, buf_b, SEM_GB, "fetch")
    fetch_n = (idx_n, buf_a, SEM_GA, "fetch")
    fetch_n_moe = (idx_n, buf_a, SEM_GA, "fetch_moe")
    store_p = (idx_p, wb_b, SEM_SB, "store")
    store_a = (idx_a, wb_a, SEM_SA, "store")
    store_b = (idx_b, wb_b, SEM_SB, "store")

    @pl.when(step == 0)
    def _():
        start_all(idx_a, buf_a, SEM_GA, "fetch")
        prev_first[0] = 0

    @pl.when(step > 0)
    def _():
        wait_all(wb_a, SEM_SA, "store")

    def overlaps(first, last, other_first, other_last):
        return jnp.logical_not((other_last < first) | (last < other_first))

    serial = (step == 0) | overlaps(idx_a[0, 0], idx_b[0, rows - 1], prev_first[0], idx_p[0, rows - 1])
    next_serial = overlaps(idx_n[0, 0], idx_m[0, rows - 1], idx_a[0, 0], idx_b[0, rows - 1])
    prev_first[0] = idx_a[0, 0]

    @pl.when(serial)
    def _():
        @pl.when(step == 0)
        def _():
            wait_all(buf_a, SEM_GA, "fetch")

        @pl.when(step > 0)
        def _():
            start_all(*store_p)
            wait_all(wb_b, SEM_SB, "store")
            wait_all(buf_a, SEM_GA, "fetch_moe")
            start_all(idx_a, buf_a, SEM_FIX, "fetch_res")
            wait_all(buf_a, SEM_FIX, "fetch_res")
        block(0, buf_a, SEM_GA, wb_a, [fetch_b], fetched=True)

    @pl.when(jnp.logical_not(serial))
    def _():
        block(0, buf_a, SEM_GA, wb_a, [store_p, fetch_b])
        wait_all(wb_b, SEM_SB, "store")

    @pl.when(next_serial)
    def _():
        block(1, buf_b, SEM_GB, wb_b, [store_a, fetch_n_moe])

    @pl.when(jnp.logical_not(next_serial))
    def _():
        block(1, buf_b, SEM_GB, wb_b, [store_a, fetch_n])

    @pl.when(step == last_step)
    def _():
        wait_all(buf_a, SEM_GA, "fetch_moe")
        start_all(*store_b)
        wait_all(wb_a, SEM_SA, "store")
        wait_all(wb_b, SEM_SB, "store")


def _experts(idx_blocks, xh, tok_rep, gate_rep, gate_f, w_gate, w_up, w_down, layer, seq, rows):
    n, d = xh.shape[0], D_MODEL
    e = w_gate.shape[1]
    cap = tok_rep.shape[0] // e
    nsteps = cap // (2 * rows)
    nbatch = gate_f.shape[0]
    seq_shift = seq.bit_length() - 1
    assert 1 << seq_shift == seq and nsteps * 2 * rows == cap
    wspec = pl.BlockSpec((1, 1, d, d), lambda i, j: (layer, i, 0, 0))
    any_spec = pl.BlockSpec(memory_space=pl.ANY)
    return pl.pallas_call(
        functools.partial(_experts_kernel, rows=rows, nsteps=nsteps, seq_shift=seq_shift, nbatch=nbatch,
                          col_chunk=EXPERT_COL_CHUNK),
        grid=(e, nsteps),
        in_specs=[
            any_spec, any_spec,
            pl.BlockSpec((2 * rows, LANES), lambda i, j: (i * nsteps + j, 0)),
            pl.BlockSpec((2 * rows, LANES), lambda i, j: (i * nsteps + j, 0)),
            pl.BlockSpec((nbatch, d), lambda i, j: (0, 0)),
            wspec, wspec, wspec,
        ],
        out_specs=any_spec,
        out_shape=jax.ShapeDtypeStruct((n, 2 * d), _F32),
        scratch_shapes=[
            pltpu.SMEM((1, rows), jnp.int32),
            pltpu.SMEM((1, rows), jnp.int32),
            pltpu.SMEM((1, rows), jnp.int32),
            pltpu.SMEM((1, rows), jnp.int32),
            pltpu.SMEM((1, rows), jnp.int32),
            pltpu.SMEM((1,), jnp.int32),
            pltpu.VMEM((rows, 2 * d), _F32),
            pltpu.VMEM((rows, 2 * d), _F32),
            pltpu.VMEM((rows, d), _F32),
            pltpu.VMEM((rows, d), _F32),
            pltpu.SemaphoreType.DMA((6,)),
        ],
        input_output_aliases={1: 0},
        compiler_params=_params(("arbitrary", "arbitrary")),
        name="experts",
    )(idx_blocks, xh, tok_rep, gate_rep, gate_f, w_gate, w_up, w_down)


def _final_kernel(x_ref, g_ref, o_ref):
    o_ref[0] = _rms(x_ref[0]) * g_ref[...]


def _final_norm(x, g, tile):
    b, s, d = x.shape[0], x.shape[1], D_MODEL
    return pl.pallas_call(
        _final_kernel,
        grid=(b, s // tile),
        in_specs=[pl.BlockSpec((1, tile, d), lambda i, t: (i, t, 0)),
                  pl.BlockSpec((1, d), lambda i, t: (0, 0))],
        out_specs=pl.BlockSpec((1, tile, d), lambda i, t: (i, t, 0)),
        out_shape=jax.ShapeDtypeStruct((b, s, d), _F32),
        compiler_params=_params(("arbitrary", "arbitrary")),
        name="final_norm",
    )(x, g)


def _channel_dft_table():
    k = jnp.arange(HEAD_DIM, dtype=jnp.int32)
    ang = ((k[:, None] * k[None, :]) % HEAD_DIM).astype(_F32) * (2.0 * math.pi / HEAD_DIM)
    groups = FOURIER_W // HEAD_DIM
    eye = jnp.eye(groups, dtype=_F32)
    c = jnp.kron(eye, jnp.cos(ang)) / math.sqrt(HEAD_DIM)
    s = jnp.kron(eye, jnp.sin(ang)) / math.sqrt(HEAD_DIM)
    return jnp.concatenate([c, s], axis=1).astype(_BF16)


def _split_hi_lo(w):
    hi = w.astype(_BF16)
    lo = (w - hi.astype(_F32)).astype(_BF16)
    return jnp.concatenate([hi, lo], axis=-1)


def _seq_dft_tables(s):
    radix = 64
    assert s % radix == 0
    k = jnp.arange(s, dtype=jnp.int32)[:, None]
    t1 = jnp.arange(s // radix, dtype=jnp.int32)[None, :]
    t0 = jnp.arange(radix, dtype=jnp.int32)[None, :]
    ang_a = ((k * t1 * radix) % s).astype(_F32) * (2.0 * math.pi / s)
    ang_b = ((k * t0) % s).astype(_F32) * (2.0 * math.pi / s)
    ca, sa = jnp.cos(ang_a)[:, :, None], jnp.sin(ang_a)[:, :, None]
    cb, sb = jnp.cos(ang_b)[:, None, :], jnp.sin(ang_b)[:, None, :]
    cos = (ca * cb - sa * sb).reshape(s, s)
    nsin = (-(sa * cb + ca * sb)).reshape(s, s)
    return cos.astype(_BF16), nsin.astype(_BF16)


def _encoder(x, mod, p, dft64, seq_tables, tile, erows=None):
    b, s, d = x.shape
    n = b * s
    cap = CAPACITY_FACTOR * n // N_EXPERTS
    if erows is None:
        erows = min(EXPERT_MAX_ROWS, cap // 4)
    assert cap % (2 * erows) == 0 and cap // erows >= 2
    wc, wsn = seq_tables
    depth = p["w_in"].shape[0]
    for l in range(depth):
        mod_l = mod[l][:, None, :]
        yac, fc, fs = _mix_in(x, mod_l, p["g_mix"][l][None], p["w_in_bf"][l], p["conv_w"][l],
                              p["g_sgu"][l][None], p["w_sp_bf"][l], p["b_sp_rows"][l],
                              p["g_grp"][l][None], dft64, tile)
        yf = _seq_dft(wc, wsn, fc, fs)
        xh, aff = _mix_out(x, yac, yf, mod_l, p["g_grp"][l][None], p["w_out_bf"][l],
                           p["g_ffn"][l][None], p["w_router_hl"][l], tile)
        aff_t = aff.reshape(n, N_EXPERTS).T.reshape(N_EXPERTS, n // LANES, LANES)
        tok_rep, gate_rep, idx = _route(aff_t, cap)
        idx_blocks = idx.reshape(N_EXPERTS * cap // erows, 1, erows)
        gate_f = mod_l[:, 0, 5 * d:6 * d]
        x = _experts(idx_blocks, xh.reshape(n, 2 * d),
                     tok_rep.reshape(N_EXPERTS * cap, LANES), gate_rep.reshape(N_EXPERTS * cap, LANES),
                     gate_f, p["w_gate"], p["w_up"], p["w_down"], l, s, erows).reshape(b, s, 2 * d)
    return _final_norm(x, p["g_final"][None], tile)


def kernel(x_prompt, x_sample, c_prompt, c_sample, w_ada, b_ada, g_mix, w_in, conv_w, g_sgu, w_spatial,
           b_spatial, g_grp, w_out, g_ffn, w_router, w_gate, w_up, w_down, g_final):
    nb = x_prompt.shape[0]
    mod = _ada(jnp.concatenate([c_prompt, c_sample], axis=0), w_ada, b_ada)
    p = dict(
        g_mix=g_mix, conv_w=conv_w, g_sgu=g_sgu, g_grp=g_grp, g_ffn=g_ffn, w_router_hl=_split_hi_lo(w_router),
        w_gate=w_gate, w_up=w_up, w_down=w_down, g_final=g_final, w_in=w_in,
        w_in_bf=w_in.astype(_BF16), w_out_bf=w_out.astype(_BF16), w_sp_bf=w_spatial.astype(_BF16),
        b_sp_rows=jnp.repeat(jnp.swapaxes(b_spatial, 1, 2), HEAD_DIM, axis=2),
    )
    dft64 = _channel_dft_table()
    tile = min(512, x_sample.shape[1])
    y_prompt = _encoder(x_prompt, mod[:, :nb], p, dft64, _seq_dft_tables(x_prompt.shape[1]), tile)
    y_sample = _encoder(x_sample, mod[:, nb:], p, dft64, _seq_dft_tables(x_sample.shape[1]), tile)
    return (y_prompt, y_sample)
```

```python
import functools
import math

import jax
import jax.numpy as jnp
from jax import lax
from jax.experimental import pallas as pl
from jax.experimental.pallas import tpu as pltpu

D_MODEL = 1024
HEAD_DIM = 64
CONV_W = 384
FOURIER_W = 256
SGU_W = 384
SGU_HEADS = SGU_W // HEAD_DIM
CHUNK = 128
IN_W = 3 * CONV_W + FOURIER_W + 2 * SGU_W
N_EXPERTS = 16
CAPACITY_FACTOR = 2
N_MOD = 6
EPS = 1e-6

LANES = 128
SUBLANES = 8
VMEM_LIMIT_BYTES = 52 * 1024 * 1024
MXU_COLS = 256
EXPERT_COL_CHUNK = MXU_COLS
EXPERT_MAX_ROWS = 512
SEQ_TILE = 1024

_HI = lax.Precision.HIGHEST
_BF16 = jnp.bfloat16
_F32 = jnp.float32


def _rms(x):
    return x * lax.rsqrt(jnp.mean(x * x, axis=-1, keepdims=True) + EPS)


def _params(sem, vmem=VMEM_LIMIT_BYTES):
    return pltpu.CompilerParams(dimension_semantics=sem, vmem_limit_bytes=vmem)


def _ada_kernel(c_ref, w_ref, b_ref, o_ref):
    c = c_ref[...]
    a = c * jax.nn.sigmoid(c)
    o_ref[0] = jnp.dot(a, w_ref[0], precision=_HI, preferred_element_type=_F32) + b_ref[0]


def _ada(c_all, w_ada, b_ada):
    depth, d, m = w_ada.shape
    nb = c_all.shape[0]
    tn = 1536
    return pl.pallas_call(
        _ada_kernel,
        grid=(depth, m // tn),
        in_specs=[
            pl.BlockSpec((nb, d), lambda l, j: (0, 0)),
            pl.BlockSpec((1, d, tn), lambda l, j: (l, 0, j)),
            pl.BlockSpec((1, 1, tn), lambda l, j: (l, 0, j)),
        ],
        out_specs=pl.BlockSpec((1, nb, tn), lambda l, j: (l, 0, j)),
        out_shape=jax.ShapeDtypeStruct((depth, nb, m), _F32),
        compiler_params=_params(("arbitrary", "arbitrary")),
        name="ada",
    )(c_all, w_ada, b_ada.reshape(depth, 1, m))


def _mix_in_kernel(x_ref, xp_ref, xn_ref, mod_ref, gmix_ref, win_ref, convw_ref, gsgu_ref,
                   wsp_ref, bsp_ref, ggrp_ref, dft_ref, yac_ref, fc_ref, fs_ref, *, tile):
    t = pl.program_id(1)
    nt = pl.num_programs(1)
    shift = mod_ref[0, :, 0:D_MODEL]
    scale = mod_ref[0, :, D_MODEL:2 * D_MODEL]
    gain = gmix_ref[...] * (1.0 + scale)

    def modulated(x):
        return (_rms(x) * gain + shift).astype(_BF16)

    h = modulated(x_ref[0])
    z = jnp.dot(h, win_ref[...], preferred_element_type=_F32)

    def halo(xh_ref):
        zh = jnp.dot(modulated(xh_ref[0]), win_ref[:, CONV_W:3 * CONV_W],
                     preferred_element_type=_F32)
        return zh[:, :CONV_W] * zh[:, CONV_W:]

    a_before = jnp.where(t > 0, halo(xp_ref)[SUBLANES - 1:SUBLANES, :], 0.0)
    a_after = jnp.where(t < nt - 1, halo(xn_ref)[0:1, :], 0.0)

    zb = z[:, 0:CONV_W]
    a = z[:, CONV_W:2 * CONV_W] * z[:, 2 * CONV_W:3 * CONV_W]
    row = lax.broadcasted_iota(jnp.int32, (tile, CONV_W), 0)
    a_prev = jnp.where(row == 0, a_before, pltpu.roll(a, 1, axis=0))
    a_next = jnp.where(row == tile - 1, a_after, pltpu.roll(a, tile - 1, axis=0))
    cw = convw_ref[...]
    conv = zb * (cw[0:1, :] * a_prev + cw[1:2, :] * a + cw[2:3, :] * a_next)
    ya = (_rms(conv) * ggrp_ref[:, 0:CONV_W]).astype(_BF16)

    zf = z[:, 3 * CONV_W:3 * CONV_W + FOURIER_W].astype(_BF16)
    f2 = jnp.dot(zf, dft_ref[...], preferred_element_type=_F32)
    fc_ref[...] = f2[:, :FOURIER_W].astype(_BF16)
    fs_ref[...] = f2[:, FOURIER_W:].astype(_BF16)

    off = 3 * CONV_W + FOURIER_W
    u = jax.nn.gelu(z[:, off:off + SGU_W])
    v = (_rms(jax.nn.gelu(z[:, off + SGU_W:off + 2 * SGU_W])) * gsgu_ref[...]).astype(_BF16)
    lane = lax.broadcasted_iota(jnp.int32, (CHUNK, LANES), 1)
    first_head = lane < HEAD_DIM
    chunks = []
    for c in range(tile // CHUNK):
        pairs = []
        for k in range(SGU_W // LANES):
            vk = v[c * CHUNK:(c + 1) * CHUNK, k * LANES:(k + 1) * LANES]
            r0 = jnp.dot(wsp_ref[2 * k], vk, preferred_element_type=_F32)
            r1 = jnp.dot(wsp_ref[2 * k + 1], vk, preferred_element_type=_F32)
            pairs.append(jnp.where(first_head, r0, r1))
        chunks.append(jnp.concatenate(pairs, axis=1) + bsp_ref[...])
    sv = jnp.concatenate(chunks, axis=0)
    yc = (_rms(u * sv) * ggrp_ref[:, CONV_W + FOURIER_W:]).astype(_BF16)
    yac_ref[0] = jnp.concatenate([ya, yc], axis=1)


def _mix_in(x, mod_l, gmix, win_bf, convw, gsgu, wsp_bf, bsp_rows, ggrp, dft64, tile):
    b, s, d = x.shape[0], x.shape[1], D_MODEL
    nt = s // tile
    hb = tile // SUBLANES
    last_hb = s // SUBLANES - 1
    const2 = lambda i, t: (0, 0)
    return pl.pallas_call(
        functools.partial(_mix_in_kernel, tile=tile),
        grid=(b, nt),
        in_specs=[
            pl.BlockSpec((1, tile, d), lambda i, t: (i, t, 0)),
            pl.BlockSpec((1, SUBLANES, d), lambda i, t: (i, jnp.maximum(t * hb - 1, 0), 0)),
            pl.BlockSpec((1, SUBLANES, d), lambda i, t: (i, jnp.minimum((t + 1) * hb, last_hb), 0)),
            pl.BlockSpec((1, 1, N_MOD * d), lambda i, t: (i, 0, 0)),
            pl.BlockSpec((1, d), const2),
            pl.BlockSpec((d, IN_W), const2),
            pl.BlockSpec((3, CONV_W), const2),
            pl.BlockSpec((1, SGU_W), const2),
            pl.BlockSpec((SGU_HEADS, CHUNK, CHUNK), lambda i, t: (0, 0, 0)),
            pl.BlockSpec((CHUNK, SGU_W), const2),
            pl.BlockSpec((1, d), const2),
            pl.BlockSpec((FOURIER_W, 2 * FOURIER_W), const2),
        ],
        out_specs=[
            pl.BlockSpec((1, tile, CONV_W + SGU_W), lambda i, t: (i, t, 0)),
            pl.BlockSpec((tile, FOURIER_W), lambda i, t: (t, i)),
            pl.BlockSpec((tile, FOURIER_W), lambda i, t: (t, i)),
        ],
        out_shape=[
            jax.ShapeDtypeStruct((b, s, CONV_W + SGU_W), _BF16),
            jax.ShapeDtypeStruct((s, b * FOURIER_W), _BF16),
            jax.ShapeDtypeStruct((s, b * FOURIER_W), _BF16),
        ],
        compiler_params=_params(("arbitrary", "arbitrary")),
        name="mix_in",
    )(x, x, x, mod_l, gmix, win_bf, convw, gsgu, wsp_bf, bsp_rows, ggrp, dft64)


def _seq_dft_kernel(wc_ref, ws_ref, fc_ref, fs_ref, o_ref):
    part = (jnp.dot(wc_ref[...], fc_ref[...], preferred_element_type=_F32)
            + jnp.dot(ws_ref[...], fs_ref[...], preferred_element_type=_F32))

    @pl.when(pl.program_id(2) == 0)
    def _():
        o_ref[...] = part

    @pl.when(pl.program_id(2) > 0)
    def _():
        o_ref[...] += part


def _seq_dft(wc, wsn, fc, fs):
    s = wc.shape[0]
    n = fc.shape[1]
    tm = min(1024, s)
    tn = min(1024, n)
    tk = min(1024, s)
    return pl.pallas_call(
        _seq_dft_kernel,
        grid=(s // tm, n // tn, s // tk),
        in_specs=[
            pl.BlockSpec((tm, tk), lambda i, j, k: (i, k)),
            pl.BlockSpec((tm, tk), lambda i, j, k: (i, k)),
            pl.BlockSpec((tk, tn), lambda i, j, k: (k, j)),
            pl.BlockSpec((tk, tn), lambda i, j, k: (k, j)),
        ],
        out_specs=pl.BlockSpec((tm, tn), lambda i, j, k: (i, j)),
        out_shape=jax.ShapeDtypeStruct((s, n), _F32),
        compiler_params=_params(("arbitrary", "arbitrary", "arbitrary")),
        name="seq_dft",
    )(wc, wsn, fc, fs)


def _mix_out_kernel(x_ref, yac_ref, yf_ref, mod_ref, ggrp_ref, wout_ref, gffn_ref, wr_ref,
                    xh_ref, aff_ref, *, seq_scale):
    d = D_MODEL
    gate_m = mod_ref[0, :, 2 * d:3 * d]
    shift_f = mod_ref[0, :, 3 * d:4 * d]
    scale_f = mod_ref[0, :, 4 * d:5 * d]
    yf = (_rms(yf_ref[...] * seq_scale) * ggrp_ref[:, CONV_W:CONV_W + FOURIER_W]).astype(_BF16)
    yac = yac_ref[0]
    ycat = jnp.concatenate([yac[:, :CONV_W], yf, yac[:, CONV_W:]], axis=1)
    mix = jnp.dot(ycat, wout_ref[...], preferred_element_type=_F32)
    x1 = x_ref[0] + gate_m * mix
    h2 = _rms(x1) * (gffn_ref[...] * (1.0 + scale_f)) + shift_f
    xh_ref[0, :, :d] = x1
    xh_ref[0, :, d:] = h2
    h_hi = h2.astype(_BF16)
    h_lo = (h2 - h_hi.astype(_F32)).astype(_BF16)
    by_hi = jnp.dot(h_hi, wr_ref[...], preferred_element_type=_F32)
    logits = (by_hi[:, :N_EXPERTS] + (by_hi[:, N_EXPERTS:]
              + jnp.dot(h_lo, wr_ref[:, :N_EXPERTS], preferred_element_type=_F32)))
    e = jnp.exp(logits - jnp.max(logits, axis=-1, keepdims=True))
    aff_ref[0] = e / jnp.sum(e, axis=-1, keepdims=True)


def _mix_out(x, yac, yf, mod_l, ggrp, wout_bf, gffn, w_router, tile):
    b, s, d = x.shape[0], x.shape[1], D_MODEL
    const2 = lambda i, t: (0, 0)
    return pl.pallas_call(
        functools.partial(_mix_out_kernel, seq_scale=1.0 / math.sqrt(s)),
        grid=(b, s // tile),
        in_specs=[
            pl.BlockSpec((1, tile, d), lambda i, t: (i, t, 0)),
            pl.BlockSpec((1, tile, CONV_W + SGU_W), lambda i, t: (i, t, 0)),
            pl.BlockSpec((tile, FOURIER_W), lambda i, t: (t, i)),
            pl.BlockSpec((1, 1, N_MOD * d), lambda i, t: (i, 0, 0)),
            pl.BlockSpec((1, d), const2),
            pl.BlockSpec((d, d), const2),
            pl.BlockSpec((1, d), const2),
            pl.BlockSpec((d, 2 * N_EXPERTS), const2),
        ],
        out_specs=[
            pl.BlockSpec((1, tile, 2 * d), lambda i, t: (i, t, 0)),
            pl.BlockSpec((1, tile, N_EXPERTS), lambda i, t: (i, t, 0)),
        ],
        out_shape=[
            jax.ShapeDtypeStruct((b, s, 2 * d), _F32),
            jax.ShapeDtypeStruct((b, s, N_EXPERTS), _F32),
        ],
        compiler_params=_params(("arbitrary", "arbitrary")),
        name="mix_out",
    )(x, yac, yf, mod_l, ggrp, wout_bf, gffn, w_router)


def _cumsum_mats(rows):
    qi = lax.broadcasted_iota(jnp.int32, (LANES, LANES), 0)
    qj = lax.broadcasted_iota(jnp.int32, (LANES, LANES), 1)
    ri = lax.broadcasted_iota(jnp.int32, (rows, rows), 0)
    rj = lax.broadcasted_iota(jnp.int32, (rows, rows), 1)
    def ones_where(mask):
        return jnp.where(mask, 1.0, 0.0).astype(_BF16)

    lane_excl = ones_where(qi < qj)
    lane_incl = ones_where(qi <= qj)
    row_excl = ones_where(rj < ri)
    row_incl_t = ones_where(ri <= rj)
    return lane_excl, lane_incl, row_excl, row_incl_t


def _route_kernel(aff_ref, idx_ref, gate_ref, idxc_ref, *, cap, jb):
    a = aff_ref[0]
    rows = a.shape[0]
    lane_excl, lane_incl, row_excl, row_incl_t = _cumsum_mats(rows)

    def count(mask):
        return jnp.sum(jnp.where(mask, 1.0, 0.0)).astype(jnp.int32)

    def refine(thr_bits, shift, nbits):
        passing = jnp.int32(0)
        for m in range(1, 1 << nbits):
            cand_f = lax.bitcast_convert_type(thr_bits | (jnp.int32(m) << shift), _F32)
            passing += (count(a >= cand_f) >= cap).astype(jnp.int32)
        return thr_bits | (passing << shift)

    thr_bits = lax.fori_loop(0, 10, lambda i, t: refine(t, 27 - 3 * i, 3), jnp.int32(0))
    thr = lax.bitcast_convert_type(thr_bits, _F32)
    above = a > thr
    tied = a == thr
    need = cap - count(above)

    tied_b = jnp.where(tied, 1.0, 0.0).astype(_BF16)
    in_row = jnp.dot(tied_b, lane_excl, preferred_element_type=_F32)
    row_tot = jnp.dot(tied_b, jnp.ones((LANES, LANES), _BF16), preferred_element_type=_F32)
    before_row = jnp.dot(row_excl, row_tot.astype(_BF16), preferred_element_type=_F32)
    tie_rank = (in_row + before_row).astype(jnp.int32)
    sel = above | (tied & (tie_rank < need))
    sel_b = jnp.where(sel, 1.0, 0.0).astype(_BF16)

    row_cnt = lax.dot_general(jnp.ones((SUBLANES, LANES), _BF16), sel_b,
                              (((1,), (1,)), ((), ())), preferred_element_type=_F32)
    incl = jnp.dot(row_cnt.astype(_BF16), row_incl_t, preferred_element_type=_F32)[0:1, :]
    excl = incl - row_cnt[0:1, :]
    a_hi = a.astype(_BF16)
    a_mid = (a - a_hi.astype(_F32)).astype(_BF16)
    a_lo = (a - a_hi.astype(_F32) - a_mid.astype(_F32)).astype(_BF16)
    row_id = lax.broadcasted_iota(jnp.int32, (1, rows), 1).astype(_F32)
    lane_id = lax.broadcasted_iota(jnp.int32, (jb, LANES), 1).astype(_F32)
    diag = (lax.broadcasted_iota(jnp.int32, (LANES, LANES), 0)
            == lax.broadcasted_iota(jnp.int32, (LANES, LANES), 1))

    for c in range(cap // jb):
        j = (lax.broadcasted_iota(jnp.int32, (jb, 1), 0) + c * jb).astype(_F32)
        hit = (excl <= j) & (j < incl)
        hit_f = jnp.where(hit, 1.0, 0.0)
        r_of_j = jnp.sum(hit_f * row_id, axis=1, keepdims=True)
        target = j + 1.0 - jnp.sum(hit_f * excl, axis=1, keepdims=True)
        row_sel = jnp.dot(hit_f.astype(_BF16), sel_b, preferred_element_type=_F32)
        within = jnp.dot(row_sel.astype(_BF16), lane_incl, preferred_element_type=_F32)
        l_of_j = jnp.sum(jnp.where(within < target, 1.0, 0.0), axis=1, keepdims=True)
        hit_b = hit_f.astype(_BF16)
        a_row = (jnp.dot(hit_b, a_hi, preferred_element_type=_F32)
                 + jnp.dot(hit_b, a_mid, preferred_element_type=_F32)
                 + jnp.dot(hit_b, a_lo, preferred_element_type=_F32))
        gate = jnp.sum(jnp.where(lane_id == l_of_j, a_row, 0.0), axis=1, keepdims=True)
        token_rep = jnp.broadcast_to(r_of_j * LANES + l_of_j, (jb, LANES))
        idx_ref[0, c * jb:(c + 1) * jb, :] = token_rep.astype(jnp.int32)
        gate_ref[0, c * jb:(c + 1) * jb, :] = jnp.broadcast_to(gate, (jb, LANES))
        for g in range(jb // LANES):
            grp = jnp.where(diag, token_rep[g * LANES:(g + 1) * LANES, :], 0.0)
            out_row = c * (jb // LANES) + g
            idxc_ref[0, out_row:out_row + 1, :] = jnp.sum(grp, axis=0, keepdims=True).astype(jnp.int32)


def _route(aff_t, cap):
    e, rows, _ = aff_t.shape
    jb = min(512, cap)
    return pl.pallas_call(
        functools.partial(_route_kernel, cap=cap, jb=jb),
        grid=(e,),
        in_specs=[pl.BlockSpec((1, rows, LANES), lambda i: (i, 0, 0))],
        out_specs=[
            pl.BlockSpec((1, cap, LANES), lambda i: (i, 0, 0)),
            pl.BlockSpec((1, cap, LANES), lambda i: (i, 0, 0)),
            pl.BlockSpec((1, cap // LANES, LANES), lambda i: (i, 0, 0)),
        ],
        out_shape=[
            jax.ShapeDtypeStruct((e, cap, LANES), jnp.int32),
            jax.ShapeDtypeStruct((e, cap, LANES), _F32),
            jax.ShapeDtypeStruct((e, cap // LANES, LANES), jnp.int32),
        ],
        compiler_params=_params(("arbitrary",)),
        name="route",
    )(aff_t)


def _experts_kernel(idx_hbm, xin_hbm, tok_ref, gate_ref, gf_ref, wg_ref, wu_ref, wd_ref,
                    xh_hbm, idx_p, idx_a, idx_b, idx_n, idx_m, prev_first, buf_a, buf_b, wb_a, wb_b, sems,
                    *, rows, nsteps, seq_shift, nbatch, col_chunk):
    del xin_hbm
    d = D_MODEL
    e = pl.program_id(0)
    j = pl.program_id(1)
    step = e * nsteps + j
    last_step = pl.num_programs(0) * nsteps - 1
    blk_a = 2 * step
    blk_p = jnp.maximum(blk_a - 1, 0)
    last_blk = 2 * last_step + 1
    blk_n = jnp.minimum(blk_a + 2, last_blk)
    blk_m = jnp.minimum(blk_a + 3, last_blk)
    SEM_IDX, SEM_GA, SEM_GB, SEM_SA, SEM_SB, SEM_FIX = range(6)

    idx_copies = [pltpu.make_async_copy(idx_hbm.at[b], ref, sems.at[SEM_IDX])
                  for b, ref in ((blk_p, idx_p), (blk_a, idx_a), (blk_a + 1, idx_b), (blk_n, idx_n),
                                 (blk_m, idx_m))]
    for c in idx_copies:
        c.start()
    for c in idx_copies:
        c.wait()

    res_cols = pl.ds(0, d)
    moe_cols = pl.ds(d, d)

    def row_copy(idx_ref, r, buf, sem, kind):
        tok = idx_ref[0, r]
        if kind == "store":
            return pltpu.make_async_copy(buf.at[pl.ds(r, 1)], xh_hbm.at[pl.ds(tok, 1), res_cols], sems.at[sem])
        if kind == "fetch":
            return pltpu.make_async_copy(xh_hbm.at[pl.ds(tok, 1)], buf.at[pl.ds(r, 1)], sems.at[sem])
        cols = moe_cols if kind == "fetch_moe" else res_cols
        return pltpu.make_async_copy(xh_hbm.at[pl.ds(tok, 1), cols], buf.at[pl.ds(r, 1), cols], sems.at[sem])

    def start_all(idx_ref, buf, sem, kind):
        def body(r, carry):
            row_copy(idx_ref, r, buf, sem, kind).start()
            return carry
        lax.fori_loop(0, rows, body, 0)

    def wait_all(buf, sem, kind):
        all_rows = pl.ds(0, rows)
        if kind == "store":
            pltpu.make_async_copy(buf, xh_hbm.at[all_rows, res_cols], sems.at[sem]).wait()
        elif kind == "fetch":
            pltpu.make_async_copy(xh_hbm.at[all_rows], buf, sems.at[sem]).wait()
        else:
            cols = moe_cols if kind == "fetch_moe" else res_cols
            pltpu.make_async_copy(xh_hbm.at[all_rows, cols], buf.at[:, cols], sems.at[sem]).wait()

    def block(part, buf, sem_g, wb, streams, fetched=False):
        if not fetched:
            wait_all(buf, sem_g, "fetch")
        todo = [functools.partial(row_copy, stream[0], r, *stream[1:])
                for r in range(rows) for stream in streams]
        nch = D_MODEL // col_chunk
        per_call = -(-len(todo) // (3 * nch))

        def start_some():
            for k, make in enumerate(todo[:per_call]):
                make().start(priority=k % 2)
            del todo[:per_call]

        x = buf[:, d:].astype(_BF16)
        hid = []
        for c in range(nch):
            cs = slice(c * col_chunk, (c + 1) * col_chunk)
            hg = jnp.dot(x, wg_ref[0, 0, :, cs].astype(_BF16), preferred_element_type=_F32)
            start_some()
            hu = jnp.dot(x, wu_ref[0, 0, :, cs].astype(_BF16), preferred_element_type=_F32)
            start_some()
            hid.append((hg * jax.nn.sigmoid(hg) * hu).astype(_BF16))
        hid = jnp.concatenate(hid, axis=1)
        rs = slice(part * rows, (part + 1) * rows)
        seq_of_row = tok_ref[rs, :] >> seq_shift
        gate = gate_ref[rs, :]
        upd = []
        for c in range(nch):
            cs = slice(c * col_chunk, (c + 1) * col_chunk)
            y = jnp.dot(hid, wd_ref[0, 0, :, cs].astype(_BF16), preferred_element_type=_F32)
            start_some()
            seq_c = jnp.concatenate([seq_of_row] * (col_chunk // LANES), axis=1)
            gf = jnp.zeros_like(y)
            for b in range(nbatch):
                gf = jnp.where(seq_c == b, gf_ref[b:b + 1, cs], gf)
            upd.append(gf * (y * jnp.concatenate([gate] * (col_chunk // LANES), axis=1)))
        assert not todo
        wb[...] = buf[:, :d] + jnp.concatenate(upd, axis=1)

    fetch_b = (idx_b, buf_b, SEM_GB, "fetch")
    fetch_n = (idx_n, buf_a, SEM_GA, "fetch")
    fetch_n_moe = (idx_n, buf_a, SEM_GA, "fetch_moe")
    store_p = (idx_p, wb_b, SEM_SB, "store")
    store_a = (idx_a, wb_a, SEM_SA, "store")
    store_b = (idx_b, wb_b, SEM_SB, "store")

    @pl.when(step == 0)
    def _():
        start_all(idx_a, buf_a, SEM_GA, "fetch")
        prev_first[0] = 0

    @pl.when(step > 0)
    def _():
        wait_all(wb_a, SEM_SA, "store")

    def overlaps(first, last, other_first, other_last):
        return jnp.logical_not((other_last < first) | (last < other_first))

    serial = (step == 0) | overlaps(idx_a[0, 0], idx_b[0, rows - 1], prev_first[0], idx_p[0, rows - 1])
    next_serial = overlaps(idx_n[0, 0], idx_m[0, rows - 1], idx_a[0, 0], idx_b[0, rows - 1])
    prev_first[0] = idx_a[0, 0]

    @pl.when(serial)
    def _():
        @pl.when(step == 0)
        def _():
            wait_all(buf_a, SEM_GA, "fetch")

        @pl.when(step > 0)
        def _():
            start_all(*store_p)
            wait_all(wb_b, SEM_SB, "store")
            wait_all(buf_a, SEM_GA, "fetch_moe")
            start_all(idx_a, buf_a, SEM_FIX, "fetch_res")
            wait_all(buf_a, SEM_FIX, "fetch_res")
        block(0, buf_a, SEM_GA, wb_a, [fetch_b], fetched=True)

    @pl.when(jnp.logical_not(serial))
    def _():
        block(0, buf_a, SEM_GA, wb_a, [store_p, fetch_b])
        wait_all(wb_b, SEM_SB, "store")

    @pl.when(next_serial)
    def _():
        block(1, buf_b, SEM_GB, wb_b, [store_a, fetch_n_moe])

    @pl.when(jnp.logical_not(next_serial))
    def _():
        block(1, buf_b, SEM_GB, wb_b, [store_a, fetch_n])

    @pl.when(step == last_step)
    def _():
        wait_all(buf_a, SEM_GA, "fetch_moe")
        start_all(*store_b)
        wait_all(wb_a, SEM_SA, "store")
        wait_all(wb_b, SEM_SB, "store")


def _experts(idx_blocks, xh, tok_rep, gate_rep, gate_f, w_gate, w_up, w_down, layer, seq, rows):
    n, d = xh.shape[0], D_MODEL
    e = w_gate.shape[1]
    cap = tok_rep.shape[0] // e
    nsteps = cap // (2 * rows)
    nbatch = gate_f.shape[0]
    seq_shift = seq.bit_length() - 1
    assert 1 << seq_shift == seq and nsteps * 2 * rows == cap
    wspec = pl.BlockSpec((1, 1, d, d), lambda i, j: (layer, i, 0, 0))
    any_spec = pl.BlockSpec(memory_space=pl.ANY)
    return pl.pallas_call(
        functools.partial(_experts_kernel, rows=rows, nsteps=nsteps, seq_shift=seq_shift, nbatch=nbatch,
                          col_chunk=EXPERT_COL_CHUNK),
        grid=(e, nsteps),
        in_specs=[
            any_spec, any_spec,
            pl.BlockSpec((2 * rows, LANES), lambda i, j: (i * nsteps + j, 0)),
            pl.BlockSpec((2 * rows, LANES), lambda i, j: (i * nsteps + j, 0)),
            pl.BlockSpec((nbatch, d), lambda i, j: (0, 0)),
            wspec, wspec, wspec,
        ],
        out_specs=any_spec,
        out_shape=jax.ShapeDtypeStruct((n, 2 * d), _F32),
        scratch_shapes=[
            pltpu.SMEM((1, rows), jnp.int32),
            pltpu.SMEM((1, rows), jnp.int32),
            pltpu.SMEM((1, rows), jnp.int32),
            pltpu.SMEM((1, rows), jnp.int32),
            pltpu.SMEM((1, rows), jnp.int32),
            pltpu.SMEM((1,), jnp.int32),
            pltpu.VMEM((rows, 2 * d), _F32),
            pltpu.VMEM((rows, 2 * d), _F32),
            pltpu.VMEM((rows, d), _F32),
            pltpu.VMEM((rows, d), _F32),
            pltpu.SemaphoreType.DMA((6,)),
        ],
        input_output_aliases={1: 0},
        compiler_params=_params(("arbitrary", "arbitrary")),
        name="experts",
    )(idx_blocks, xh, tok_rep, gate_rep, gate_f, w_gate, w_up, w_down)


def _final_kernel(x_ref, g_ref, o_ref):
    o_ref[0] = _rms(x_ref[0]) * g_ref[...]


def _final_norm(x, g, tile):
    b, s, d = x.shape[0], x.shape[1], D_MODEL
    return pl.pallas_call(
        _final_kernel,
        grid=(b, s // tile),
        in_specs=[pl.BlockSpec((1, tile, d), lambda i, t: (i, t, 0)),
                  pl.BlockSpec((1, d), lambda i, t: (0, 0))],
        out_specs=pl.BlockSpec((1, tile, d), lambda i, t: (i, t, 0)),
        out_shape=jax.ShapeDtypeStruct((b, s, d), _F32),
        compiler_params=_params(("arbitrary", "arbitrary")),
        name="final_norm",
    )(x, g)


def _channel_dft_table():
    k = jnp.arange(HEAD_DIM, dtype=jnp.int32)
    ang = ((k[:, None] * k[None, :]) % HEAD_DIM).astype(_F32) * (2.0 * math.pi / HEAD_DIM)
    groups = FOURIER_W // HEAD_DIM
    eye = jnp.eye(groups, dtype=_F32)
    c = jnp.kron(eye, jnp.cos(ang)) / math.sqrt(HEAD_DIM)
    s = jnp.kron(eye, jnp.sin(ang)) / math.sqrt(HEAD_DIM)
    return jnp.concatenate([c, s], axis=1).astype(_BF16)


def _split_hi_lo(w):
    hi = w.astype(_BF16)
    lo = (w - hi.astype(_F32)).astype(_BF16)
    return jnp.concatenate([hi, lo], axis=-1)


def _seq_dft_tables(s):
    radix = 64
    assert s % radix == 0
    k = jnp.arange(s, dtype=jnp.int32)[:, None]
    t1 = jnp.arange(s // radix, dtype=jnp.int32)[None, :]
    t0 = jnp.arange(radix, dtype=jnp.int32)[None, :]
    ang_a = ((k * t1 * radix) % s).astype(_F32) * (2.0 * math.pi / s)
    ang_b = ((k * t0) % s).astype(_F32) * (2.0 * math.pi / s)
    ca, sa = jnp.cos(ang_a)[:, :, None], jnp.sin(ang_a)[:, :, None]
    cb, sb = jnp.cos(ang_b)[:, None, :], jnp.sin(ang_b)[:, None, :]
    cos = (ca * cb - sa * sb).reshape(s, s)
    nsin = (-(sa * cb + ca * sb)).reshape(s, s)
    return cos.astype(_BF16), nsin.astype(_BF16)


def _encoder(x, mod, p, dft64, seq_tables, tile, erows=None):
    b, s, d = x.shape
    n = b * s
    cap = CAPACITY_FACTOR * n // N_EXPERTS
    if erows is None:
        erows = min(EXPERT_MAX_ROWS, cap // 4)
    assert cap % (2 * erows) == 0 and cap // erows >= 2
    wc, wsn = seq_tables
    depth = p["w_in"].shape[0]
    for l in range(depth):
        mod_l = mod[l][:, None, :]
        yac, fc, fs = _mix_in(x, mod_l, p["g_mix"][l][None], p["w_in_bf"][l], p["conv_w"][l],
                              p["g_sgu"][l][None], p["w_sp_bf"][l], p["b_sp_rows"][l],
                              p["g_grp"][l][None], dft64, tile)
        yf = _seq_dft(wc, wsn, fc, fs)
        xh, aff = _mix_out(x, yac, yf, mod_l, p["g_grp"][l][None], p["w_out_bf"][l],
                           p["g_ffn"][l][None], p["w_router_hl"][l], tile)
        aff_t = aff.reshape(n, N_EXPERTS).T.reshape(N_EXPERTS, n // LANES, LANES)
        tok_rep, gate_rep, idx = _route(aff_t, cap)
        idx_blocks = idx.reshape(N_EXPERTS * cap // erows, 1, erows)
        gate_f = mod_l[:, 0, 5 * d:6 * d]
        x = _experts(idx_blocks, xh.reshape(n, 2 * d),
                     tok_rep.reshape(N_EXPERTS * cap, LANES), gate_rep.reshape(N_EXPERTS * cap, LANES),
                     gate_f, p["w_gate"], p["w_up"], p["w_down"], l, s, erows).reshape(b, s, 2 * d)
    return _final_norm(x, p["g_final"][None], tile)


def kernel(x_prompt, x_sample, c_prompt, c_sample, w_ada, b_ada, g_mix, w_in, conv_w, g_sgu, w_spatial,
           b_spatial, g_grp, w_out, g_ffn, w_router, w_gate, w_up, w_down, g_final):
    nb = x_prompt.shape[0]
    mod = _ada(jnp.concatenate([c_prompt, c_sample], axis=0), w_ada, b_ada)
    p = dict(
        g_mix=g_mix, conv_w=conv_w, g_sgu=g_sgu, g_grp=g_grp, g_ffn=g_ffn, w_router_hl=_split_hi_lo(w_router),
        w_gate=w_gate, w_up=w_up, w_down=w_down, g_final=g_final, w_in=w_in,
        w_in_bf=w_in.astype(_BF16), w_out_bf=w_out.astype(_BF16), w_sp_bf=w_spatial.astype(_BF16),
        b_sp_rows=jnp.repeat(jnp.swapaxes(b_spatial, 1, 2), HEAD_DIM, axis=2),
    )
    dft64 = _channel_dft_table()
    tile = min(SEQ_TILE, x_sample.shape[1])
    y_prompt = _encoder(x_prompt, mod[:, :nb], p, dft64, _seq_dft_tables(x_prompt.shape[1]), tile)
    y_sample = _encoder(x_sample, mod[:, nb:], p, dft64, _seq_dft_tables(x_sample.shape[1]), tile)
    return (y_prompt, y_sample)
```

```python
import functools
import math

import jax
import jax.numpy as jnp
from jax import lax
from jax.experimental import pallas as pl
from jax.experimental.pallas import tpu as pltpu

D_MODEL = 1024
HEAD_DIM = 64
CONV_W = 384
FOURIER_W = 256
SGU_W = 384
SGU_HEADS = SGU_W // HEAD_DIM
CHUNK = 128
IN_W = 3 * CONV_W + FOURIER_W + 2 * SGU_W
N_EXPERTS = 16
CAPACITY_FACTOR = 2
N_MOD = 6
EPS = 1e-6

LANES = 128
SUBLANES = 8
VMEM_LIMIT_BYTES = 52 * 1024 * 1024
MXU_COLS = 256
EXPERT_COL_CHUNK = MXU_COLS
EXPERT_MAX_ROWS = 512
SEQ_TILE = 1024

_HI = lax.Precision.HIGHEST
_BF16 = jnp.bfloat16
_F32 = jnp.float32


def _rms(x):
    return x * lax.rsqrt(jnp.mean(x * x, axis=-1, keepdims=True) + EPS)


def _params(sem, vmem=VMEM_LIMIT_BYTES):
    return pltpu.CompilerParams(dimension_semantics=sem, vmem_limit_bytes=vmem)


def _ada_kernel(c_ref, w_ref, b_ref, o_ref):
    c = c_ref[...]
    a = c * jax.nn.sigmoid(c)
    o_ref[0] = jnp.dot(a, w_ref[0], precision=_HI, preferred_element_type=_F32) + b_ref[0]


def _ada(c_all, w_ada, b_ada):
    depth, d, m = w_ada.shape
    nb = c_all.shape[0]
    tn = 1536
    return pl.pallas_call(
        _ada_kernel,
        grid=(depth, m // tn),
        in_specs=[
            pl.BlockSpec((nb, d), lambda l, j: (0, 0)),
            pl.BlockSpec((1, d, tn), lambda l, j: (l, 0, j)),
            pl.BlockSpec((1, 1, tn), lambda l, j: (l, 0, j)),
        ],
        out_specs=pl.BlockSpec((1, nb, tn), lambda l, j: (l, 0, j)),
        out_shape=jax.ShapeDtypeStruct((depth, nb, m), _F32),
        compiler_params=_params(("arbitrary", "arbitrary")),
        name="ada",
    )(c_all, w_ada, b_ada.reshape(depth, 1, m))


def _mix_in_kernel(x_ref, xp_ref, xn_ref, mod_ref, gmix_ref, win_ref, convw_ref, gsgu_ref,
                   wsp_ref, bsp_ref, ggrp_ref, dft_ref, yac_ref, fc_ref, fs_ref, *, tile):
    t = pl.program_id(1)
    nt = pl.num_programs(1)
    shift = mod_ref[0, :, 0:D_MODEL]
    scale = mod_ref[0, :, D_MODEL:2 * D_MODEL]
    gain = gmix_ref[...] * (1.0 + scale)

    def modulated(x):
        return (_rms(x) * gain + shift).astype(_BF16)

    h = modulated(x_ref[0])
    z = jnp.dot(h, win_ref[...], preferred_element_type=_F32)

    def halo(xh_ref):
        zh = jnp.dot(modulated(xh_ref[0]), win_ref[:, CONV_W:3 * CONV_W],
                     preferred_element_type=_F32)
        return zh[:, :CONV_W] * zh[:, CONV_W:]

    a_before = jnp.where(t > 0, halo(xp_ref)[SUBLANES - 1:SUBLANES, :], 0.0)
    a_after = jnp.where(t < nt - 1, halo(xn_ref)[0:1, :], 0.0)

    zb = z[:, 0:CONV_W]
    a = z[:, CONV_W:2 * CONV_W] * z[:, 2 * CONV_W:3 * CONV_W]
    row = lax.broadcasted_iota(jnp.int32, (tile, CONV_W), 0)
    a_prev = jnp.where(row == 0, a_before, pltpu.roll(a, 1, axis=0))
    a_next = jnp.where(row == tile - 1, a_after, pltpu.roll(a, tile - 1, axis=0))
    cw = convw_ref[...]
    conv = zb * (cw[0:1, :] * a_prev + cw[1:2, :] * a + cw[2:3, :] * a_next)
    ya = (_rms(conv) * ggrp_ref[:, 0:CONV_W]).astype(_BF16)

    zf = z[:, 3 * CONV_W:3 * CONV_W + FOURIER_W].astype(_BF16)
    f2 = jnp.dot(zf, dft_ref[...], preferred_element_type=_F32)
    fc_ref[...] = f2[:, :FOURIER_W].astype(_BF16)
    fs_ref[...] = f2[:, FOURIER_W:].astype(_BF16)

    off = 3 * CONV_W + FOURIER_W
    u = jax.nn.gelu(z[:, off:off + SGU_W])
    v = (_rms(jax.nn.gelu(z[:, off + SGU_W:off + 2 * SGU_W])) * gsgu_ref[...]).astype(_BF16)
    lane = lax.broadcasted_iota(jnp.int32, (CHUNK, LANES), 1)
    first_head = lane < HEAD_DIM
    chunks = []
    for c in range(tile // CHUNK):
        pairs = []
        for k in range(SGU_W // LANES):
            vk = v[c * CHUNK:(c + 1) * CHUNK, k * LANES:(k + 1) * LANES]
            r0 = jnp.dot(wsp_ref[2 * k], vk, preferred_element_type=_F32)
            r1 = jnp.dot(wsp_ref[2 * k + 1], vk, preferred_element_type=_F32)
            pairs.append(jnp.where(first_head, r0, r1))
        chunks.append(jnp.concatenate(pairs, axis=1) + bsp_ref[...])
    sv = jnp.concatenate(chunks, axis=0)
    yc = (_rms(u * sv) * ggrp_ref[:, CONV_W + FOURIER_W:]).astype(_BF16)
    yac_ref[0] = jnp.concatenate([ya, yc], axis=1)


def _mix_in(x, mod_l, gmix, win_bf, convw, gsgu, wsp_bf, bsp_rows, ggrp, dft64, tile):
    b, s, d = x.shape[0], x.shape[1], D_MODEL
    nt = s // tile
    hb = tile // SUBLANES
    last_hb = s // SUBLANES - 1
    const2 = lambda i, t: (0, 0)
    return pl.pallas_call(
        functools.partial(_mix_in_kernel, tile=tile),
        grid=(b, nt),
        in_specs=[
            pl.BlockSpec((1, tile, d), lambda i, t: (i, t, 0)),
            pl.BlockSpec((1, SUBLANES, d), lambda i, t: (i, jnp.maximum(t * hb - 1, 0), 0)),
            pl.BlockSpec((1, SUBLANES, d), lambda i, t: (i, jnp.minimum((t + 1) * hb, last_hb), 0)),
            pl.BlockSpec((1, 1, N_MOD * d), lambda i, t: (i, 0, 0)),
            pl.BlockSpec((1, d), const2),
            pl.BlockSpec((d, IN_W), const2),
            pl.BlockSpec((3, CONV_W), const2),
            pl.BlockSpec((1, SGU_W), const2),
            pl.BlockSpec((SGU_HEADS, CHUNK, CHUNK), lambda i, t: (0, 0, 0)),
            pl.BlockSpec((CHUNK, SGU_W), const2),
            pl.BlockSpec((1, d), const2),
            pl.BlockSpec((FOURIER_W, 2 * FOURIER_W), const2),
        ],
        out_specs=[
            pl.BlockSpec((1, tile, CONV_W + SGU_W), lambda i, t: (i, t, 0)),
            pl.BlockSpec((tile, FOURIER_W), lambda i, t: (t, i)),
            pl.BlockSpec((tile, FOURIER_W), lambda i, t: (t, i)),
        ],
        out_shape=[
            jax.ShapeDtypeStruct((b, s, CONV_W + SGU_W), _BF16),
            jax.ShapeDtypeStruct((s, b * FOURIER_W), _BF16),
            jax.ShapeDtypeStruct((s, b * FOURIER_W), _BF16),
        ],
        compiler_params=_params(("arbitrary", "arbitrary")),
        name="mix_in",
    )(x, x, x, mod_l, gmix, win_bf, convw, gsgu, wsp_bf, bsp_rows, ggrp, dft64)


def _seq_dft_kernel(wc_ref, ws_ref, fc_ref, fs_ref, o_ref):
    part = (jnp.dot(wc_ref[...], fc_ref[...], preferred_element_type=_F32)
            + jnp.dot(ws_ref[...], fs_ref[...], preferred_element_type=_F32))

    @pl.when(pl.program_id(2) == 0)
    def _():
        o_ref[...] = part

    @pl.when(pl.program_id(2) > 0)
    def _():
        o_ref[...] += part


def _seq_dft(wc, wsn, fc, fs):
    s = wc.shape[0]
    n = fc.shape[1]
    tm = min(1024, s)
    tn = min(1024, n)
    tk = min(1024, s)
    return pl.pallas_call(
        _seq_dft_kernel,
        grid=(s // tm, n // tn, s // tk),
        in_specs=[
            pl.BlockSpec((tm, tk), lambda i, j, k: (i, k)),
            pl.BlockSpec((tm, tk), lambda i, j, k: (i, k)),
            pl.BlockSpec((tk, tn), lambda i, j, k: (k, j)),
            pl.BlockSpec((tk, tn), lambda i, j, k: (k, j)),
        ],
        out_specs=pl.BlockSpec((tm, tn), lambda i, j, k: (i, j)),
        out_shape=jax.ShapeDtypeStruct((s, n), _F32),
        compiler_params=_params(("arbitrary", "arbitrary", "arbitrary")),
        name="seq_dft",
    )(wc, wsn, fc, fs)


def _mix_out_kernel(x_ref, yac_ref, yf_ref, mod_ref, ggrp_ref, wout_ref, gffn_ref, wr_ref,
                    xh_ref, aff_ref, *, seq_scale):
    d = D_MODEL
    gate_m = mod_ref[0, :, 2 * d:3 * d]
    shift_f = mod_ref[0, :, 3 * d:4 * d]
    scale_f = mod_ref[0, :, 4 * d:5 * d]
    yf = (_rms(yf_ref[...] * seq_scale) * ggrp_ref[:, CONV_W:CONV_W + FOURIER_W]).astype(_BF16)
    yac = yac_ref[0]
    ycat = jnp.concatenate([yac[:, :CONV_W], yf, yac[:, CONV_W:]], axis=1)
    mix = jnp.dot(ycat, wout_ref[...], preferred_element_type=_F32)
    x1 = x_ref[0] + gate_m * mix
    h2 = _rms(x1) * (gffn_ref[...] * (1.0 + scale_f)) + shift_f
    xh_ref[0, :, :d] = x1
    xh_ref[0, :, d:] = h2
    h_hi = h2.astype(_BF16)
    h_lo = (h2 - h_hi.astype(_F32)).astype(_BF16)
    by_hi = jnp.dot(h_hi, wr_ref[...], preferred_element_type=_F32)
    logits = (by_hi[:, :N_EXPERTS] + (by_hi[:, N_EXPERTS:]
              + jnp.dot(h_lo, wr_ref[:, :N_EXPERTS], preferred_element_type=_F32)))
    e = jnp.exp(logits - jnp.max(logits, axis=-1, keepdims=True))
    aff_ref[0] = e / jnp.sum(e, axis=-1, keepdims=True)


def _mix_out(x, yac, yf, mod_l, ggrp, wout_bf, gffn, w_router, tile):
    b, s, d = x.shape[0], x.shape[1], D_MODEL
    const2 = lambda i, t: (0, 0)
    return pl.pallas_call(
        functools.partial(_mix_out_kernel, seq_scale=1.0 / math.sqrt(s)),
        grid=(b, s // tile),
        in_specs=[
            pl.BlockSpec((1, tile, d), lambda i, t: (i, t, 0)),
            pl.BlockSpec((1, tile, CONV_W + SGU_W), lambda i, t: (i, t, 0)),
            pl.BlockSpec((tile, FOURIER_W), lambda i, t: (t, i)),
            pl.BlockSpec((1, 1, N_MOD * d), lambda i, t: (i, 0, 0)),
            pl.BlockSpec((1, d), const2),
            pl.BlockSpec((d, d), const2),
            pl.BlockSpec((1, d), const2),
            pl.BlockSpec((d, 2 * N_EXPERTS), const2),
        ],
        out_specs=[
            pl.BlockSpec((1, tile, 2 * d), lambda i, t: (i, t, 0)),
            pl.BlockSpec((1, tile, N_EXPERTS), lambda i, t: (i, t, 0)),
        ],
        out_shape=[
            jax.ShapeDtypeStruct((b, s, 2 * d), _F32),
            jax.ShapeDtypeStruct((b, s, N_EXPERTS), _F32),
        ],
        compiler_params=_params(("arbitrary", "arbitrary")),
        name="mix_out",
    )(x, yac, yf, mod_l, ggrp, wout_bf, gffn, w_router)


def _cumsum_mats(rows):
    qi = lax.broadcasted_iota(jnp.int32, (LANES, LANES), 0)
    qj = lax.broadcasted_iota(jnp.int32, (LANES, LANES), 1)
    ri = lax.broadcasted_iota(jnp.int32, (rows, rows), 0)
    rj = lax.broadcasted_iota(jnp.int32, (rows, rows), 1)
    def ones_where(mask):
        return jnp.where(mask, 1.0, 0.0).astype(_BF16)

    lane_excl = ones_where(qi < qj)
    lane_incl = ones_where(qi <= qj)
    row_excl = ones_where(rj < ri)
    row_incl_t = ones_where(ri <= rj)
    return lane_excl, lane_incl, row_excl, row_incl_t


def _route_kernel(aff_ref, idx_ref, gate_ref, idxc_ref, *, cap, jb):
    a = aff_ref[0]
    rows = a.shape[0]
    lane_excl, lane_incl, row_excl, row_incl_t = _cumsum_mats(rows)

    def count(mask):
        return jnp.sum(jnp.where(mask, 1.0, 0.0)).astype(jnp.int32)

    def refine(thr_bits, shift, nbits):
        passing = jnp.int32(0)
        for m in range(1, 1 << nbits):
            cand_f = lax.bitcast_convert_type(thr_bits | (jnp.int32(m) << shift), _F32)
            passing += (count(a >= cand_f) >= cap).astype(jnp.int32)
        return thr_bits | (passing << shift)

    thr_bits = lax.fori_loop(0, 10, lambda i, t: refine(t, 27 - 3 * i, 3), jnp.int32(0))
    thr = lax.bitcast_convert_type(thr_bits, _F32)
    above = a > thr
    tied = a == thr
    need = cap - count(above)

    tied_b = jnp.where(tied, 1.0, 0.0).astype(_BF16)
    in_row = jnp.dot(tied_b, lane_excl, preferred_element_type=_F32)
    row_tot = jnp.dot(tied_b, jnp.ones((LANES, LANES), _BF16), preferred_element_type=_F32)
    before_row = jnp.dot(row_excl, row_tot.astype(_BF16), preferred_element_type=_F32)
    tie_rank = (in_row + before_row).astype(jnp.int32)
    sel = above | (tied & (tie_rank < need))
    sel_b = jnp.where(sel, 1.0, 0.0).astype(_BF16)

    row_cnt = lax.dot_general(jnp.ones((SUBLANES, LANES), _BF16), sel_b,
                              (((1,), (1,)), ((), ())), preferred_element_type=_F32)
    incl = jnp.dot(row_cnt.astype(_BF16), row_incl_t, preferred_element_type=_F32)[0:1, :]
    excl = incl - row_cnt[0:1, :]
    a_hi = a.astype(_BF16)
    a_mid = (a - a_hi.astype(_F32)).astype(_BF16)
    a_lo = (a - a_hi.astype(_F32) - a_mid.astype(_F32)).astype(_BF16)
    row_id = lax.broadcasted_iota(jnp.int32, (1, rows), 1).astype(_F32)
    lane_id = lax.broadcasted_iota(jnp.int32, (jb, LANES), 1).astype(_F32)
    diag = (lax.broadcasted_iota(jnp.int32, (LANES, LANES), 0)
            == lax.broadcasted_iota(jnp.int32, (LANES, LANES), 1))

    for c in range(cap // jb):
        j = (lax.broadcasted_iota(jnp.int32, (jb, 1), 0) + c * jb).astype(_F32)
        hit = (excl <= j) & (j < incl)
        hit_f = jnp.where(hit, 1.0, 0.0)
        r_of_j = jnp.sum(hit_f * row_id, axis=1, keepdims=True)
        target = j + 1.0 - jnp.sum(hit_f * excl, axis=1, keepdims=True)
        row_sel = jnp.dot(hit_f.astype(_BF16), sel_b, preferred_element_type=_F32)
        within = jnp.dot(row_sel.astype(_BF16), lane_incl, preferred_element_type=_F32)
        l_of_j = jnp.sum(jnp.where(within < target, 1.0, 0.0), axis=1, keepdims=True)
        hit_b = hit_f.astype(_BF16)
        a_row = (jnp.dot(hit_b, a_hi, preferred_element_type=_F32)
                 + jnp.dot(hit_b, a_mid, preferred_element_type=_F32)
                 + jnp.dot(hit_b, a_lo, preferred_element_type=_F32))
        gate = jnp.sum(jnp.where(lane_id == l_of_j, a_row, 0.0), axis=1, keepdims=True)
        token_rep = jnp.broadcast_to(r_of_j * LANES + l_of_j, (jb, LANES))
        idx_ref[0, c * jb:(c + 1) * jb, :] = token_rep.astype(jnp.int32)
        gate_ref[0, c * jb:(c + 1) * jb, :] = jnp.broadcast_to(gate, (jb, LANES))
        for g in range(jb // LANES):
            grp = jnp.where(diag, token_rep[g * LANES:(g + 1) * LANES, :], 0.0)
            out_row = c * (jb // LANES) + g
            idxc_ref[0, out_row:out_row + 1, :] = jnp.sum(grp, axis=0, keepdims=True).astype(jnp.int32)


def _route(aff_t, cap):
    e, rows, _ = aff_t.shape
    jb = min(512, cap)
    return pl.pallas_call(
        functools.partial(_route_kernel, cap=cap, jb=jb),
        grid=(e,),
        in_specs=[pl.BlockSpec((1, rows, LANES), lambda i: (i, 0, 0))],
        out_specs=[
            pl.BlockSpec((1, cap, LANES), lambda i: (i, 0, 0)),
            pl.BlockSpec((1, cap, LANES), lambda i: (i, 0, 0)),
            pl.BlockSpec((1, cap // LANES, LANES), lambda i: (i, 0, 0)),
        ],
        out_shape=[
            jax.ShapeDtypeStruct((e, cap, LANES), jnp.int32),
            jax.ShapeDtypeStruct((e, cap, LANES), _F32),
            jax.ShapeDtypeStruct((e, cap // LANES, LANES), jnp.int32),
        ],
        compiler_params=_params(("arbitrary",)),
        name="route",
    )(aff_t)


def _experts_kernel(idx_hbm, xin_hbm, tok_ref, gate_ref, gf_ref, wg_ref, wu_ref, wd_ref,
                    xh_hbm, idx_p, idx_a, idx_b, idx_n, idx_m, prev_first, buf_a, buf_b, wb_a, wb_b, sems,
                    *, rows, nsteps, seq_shift, nbatch, col_chunk):
    del xin_hbm
    d = D_MODEL
    e = pl.program_id(0)
    j = pl.program_id(1)
    step = e * nsteps + j
    last_step = pl.num_programs(0) * nsteps - 1
    blk_a = 2 * step
    blk_p = jnp.maximum(blk_a - 1, 0)
    last_blk = 2 * last_step + 1
    blk_n = jnp.minimum(blk_a + 2, last_blk)
    blk_m = jnp.minimum(blk_a + 3, last_blk)
    SEM_IDX, SEM_GA, SEM_GB, SEM_SA, SEM_SB, SEM_FIX = range(6)

    idx_copies = [pltpu.make_async_copy(idx_hbm.at[b], ref, sems.at[SEM_IDX])
                  for b, ref in ((blk_p, idx_p), (blk_a, idx_a), (blk_a + 1, idx_b), (blk_n, idx_n),
                                 (blk_m, idx_m))]
    for c in idx_copies:
        c.start()
    for c in idx_copies:
        c.wait()

    res_cols = pl.ds(0, d)
    moe_cols = pl.ds(d, d)

    def row_copy(idx_ref, r, buf, sem, kind):
        tok = idx_ref[0, r]
        if kind == "store":
            return pltpu.make_async_copy(buf.at[pl.ds(r, 1)], xh_hbm.at[pl.ds(tok, 1), res_cols], sems.at[sem])
        if kind == "fetch":
            return pltpu.make_async_copy(xh_hbm.at[pl.ds(tok, 1)], buf.at[pl.ds(r, 1)], sems.at[sem])
        cols = moe_cols if kind == "fetch_moe" else res_cols
        return pltpu.make_async_copy(xh_hbm.at[pl.ds(tok, 1), cols], buf.at[pl.ds(r, 1), cols], sems.at[sem])

    def start_all(idx_ref, buf, sem, kind):
        def body(r, carry):
            row_copy(idx_ref, r, buf, sem, kind).start()
            return carry
        lax.fori_loop(0, rows, body, 0)

    def wait_all(buf, sem, kind):
        all_rows = pl.ds(0, rows)
        if kind == "store":
            pltpu.make_async_copy(buf, xh_hbm.at[all_rows, res_cols], sems.at[sem]).wait()
        elif kind == "fetch":
            pltpu.make_async_copy(xh_hbm.at[all_rows], buf, sems.at[sem]).wait()
        else:
            cols = moe_cols if kind == "fetch_moe" else res_cols
            pltpu.make_async_copy(xh_hbm.at[all_rows, cols], buf.at[:, cols], sems.at[sem]).wait()

    def block(part, buf, sem_g, wb, streams, fetched=False):
        if not fetched:
            wait_all(buf, sem_g, "fetch")
        todo = [functools.partial(row_copy, stream[0], r, *stream[1:])
                for r in range(rows) for stream in streams]
        nch = D_MODEL // col_chunk
        per_call = -(-len(todo) // (3 * nch))

        def start_some():
            for k, make in enumerate(todo[:per_call]):
                make().start(priority=k % 2)
            del todo[:per_call]

        x = buf[:, d:].astype(_BF16)
        hid = []
        for c in range(nch):
            cs = slice(c * col_chunk, (c + 1) * col_chunk)
            hg = jnp.dot(x, wg_ref[0, 0, :, cs].astype(_BF16), preferred_element_type=_F32)
            start_some()
            hu = jnp.dot(x, wu_ref[0, 0, :, cs].astype(_BF16), preferred_element_type=_F32)
            start_some()
            hid.append((hg * jax.nn.sigmoid(hg) * hu).astype(_BF16))
        hid = jnp.concatenate(hid, axis=1)
        rs = slice(part * rows, (part + 1) * rows)
        seq_of_row = tok_ref[rs, :] >> seq_shift
        gate = gate_ref[rs, :]
        upd = []
        for c in range(nch):
            cs = slice(c * col_chunk, (c + 1) * col_chunk)
            y = jnp.dot(hid, wd_ref[0, 0, :, cs].astype(_BF16), preferred_element_type=_F32)
            start_some()
            seq_c = jnp.concatenate([seq_of_row] * (col_chunk // LANES), axis=1)
            gf = jnp.zeros_like(y)
            for b in range(nbatch):
                gf = jnp.where(seq_c == b, gf_ref[b:b + 1, cs], gf)
            upd.append(gf * (y * jnp.concatenate([gate] * (col_chunk // LANES), axis=1)))
        assert not todo
        wb[...] = buf[:, :d] + jnp.concatenate(upd, axis=1)

    fetch_b = (idx_b, buf_b, SEM_GB, "fetch")
    fetch_n = (idx_n, buf_a, SEM_GA, "fetch")
    fetch_n_moe = (idx_n, buf_a, SEM_GA, "fetch_moe")
    store_p = (idx_p, wb_b, SEM_SB, "store")
    store_a = (idx_a, wb_a, SEM_SA, "store")
    store_b = (idx_b, wb_b, SEM_SB, "store")

    @pl.when(step == 0)
    def _():
        start_all(idx_a, buf_a, SEM_GA, "fetch")
        prev_first[0] = 0

    @pl.when(step > 0)
    def _():
        wait_all(wb_a, SEM_SA, "store")

    def overlaps(first, last, other_first, other_last):
        return jnp.logical_not((other_last < first) | (last < other_first))

    serial = (step == 0) | overlaps(idx_a[0, 0], idx_b[0, rows - 1], prev_first[0], idx_p[0, rows - 1])
    next_serial = overlaps(idx_n[0, 0], idx_m[0, rows - 1], idx_a[0, 0], idx_b[0, rows - 1])
    prev_first[0] = idx_a[0, 0]

    @pl.when(serial)
    def _():
        @pl.when(step == 0)
        def _():
            wait_all(buf_a, SEM_GA, "fetch")

        @pl.when(step > 0)
        def _():
            start_all(*store_p)
            wait_all(wb_b, SEM_SB, "store")
            wait_all(buf_a, SEM_GA, "fetch_moe")
            start_all(idx_a, buf_a, SEM_FIX, "fetch_res")
            wait_all(buf_a, SEM_FIX, "fetch_res")
        block(0, buf_a, SEM_GA, wb_a, [fetch_b], fetched=True)

    @pl.when(jnp.logical_not(serial))
    def _():
        block(0, buf_a, SEM_GA, wb_a, [store_p, fetch_b])
        wait_all(wb_b, SEM_SB, "store")

    @pl.when(next_serial)
    def _():
        block(1, buf_b, SEM_GB, wb_b, [store_a, fetch_n_moe])

    @pl.when(jnp.logical_not(next_serial))
    def _():
        block(1, buf_b, SEM_GB, wb_b, [store_a, fetch_n])

    @pl.when(step == last_step)
    def _():
        wait_all(buf_a, SEM_GA, "fetch_moe")
        start_all(*store_b)
        wait_all(wb_a, SEM_SA, "store")
        wait_all(wb_b, SEM_SB, "store")


def _experts(idx_blocks, xh, tok_rep, gate_rep, gate_f, w_gate, w_up, w_down, layer, seq, rows):
    n, d = xh.shape[0], D_MODEL
    e = w_gate.shape[1]
    cap = tok_rep.shape[0] // e
    nsteps = cap // (2 * rows)
    nbatch = gate_f.shape[0]
    seq_shift = seq.bit_length() - 1
    assert 1 << seq_shift == seq and nsteps * 2 * rows == cap
    wspec = pl.BlockSpec((1, 1, d, d), lambda i, j: (layer, i, 0, 0))
    any_spec = pl.BlockSpec(memory_space=pl.ANY)
    return pl.pallas_call(
        functools.partial(_experts_kernel, rows=rows, nsteps=nsteps, seq_shift=seq_shift, nbatch=nbatch,
                          col_chunk=EXPERT_COL_CHUNK),
        grid=(e, nsteps),
        in_specs=[
            any_spec, any_spec,
            pl.BlockSpec((2 * rows, LANES), lambda i, j: (i * nsteps + j, 0)),
            pl.BlockSpec((2 * rows, LANES), lambda i, j: (i * nsteps + j, 0)),
            pl.BlockSpec((nbatch, d), lambda i, j: (0, 0)),
            wspec, wspec, wspec,
        ],
        out_specs=any_spec,
        out_shape=jax.ShapeDtypeStruct((n, 2 * d), _F32),
        scratch_shapes=[
            pltpu.SMEM((1, rows), jnp.int32),
            pltpu.SMEM((1, rows), jnp.int32),
            pltpu.SMEM((1, rows), jnp.int32),
            pltpu.SMEM((1, rows), jnp.int32),
            pltpu.SMEM((1, rows), jnp.int32),
            pltpu.SMEM((1,), jnp.int32),
            pltpu.VMEM((rows, 2 * d), _F32),
            pltpu.VMEM((rows, 2 * d), _F32),
            pltpu.VMEM((rows, d), _F32),
            pltpu.VMEM((rows, d), _F32),
            pltpu.SemaphoreType.DMA((6,)),
        ],
        input_output_aliases={1: 0},
        compiler_params=_params(("arbitrary", "arbitrary")),
        name="experts",
    )(idx_blocks, xh, tok_rep, gate_rep, gate_f, w_gate, w_up, w_down)


def _final_kernel(x_ref, g_ref, o_ref):
    o_ref[0] = _rms(x_ref[0]) * g_ref[...]


def _final_norm(x, g, tile):
    b, s, d = x.shape[0], x.shape[1], D_MODEL
    return pl.pallas_call(
        _final_kernel,
        grid=(b, s // tile),
        in_specs=[pl.BlockSpec((1, tile, d), lambda i, t: (i, t, 0)),
                  pl.BlockSpec((1, d), lambda i, t: (0, 0))],
        out_specs=pl.BlockSpec((1, tile, d), lambda i, t: (i, t, 0)),
        out_shape=jax.ShapeDtypeStruct((b, s, d), _F32),
        compiler_params=_params(("arbitrary", "arbitrary")),
        name="final_norm",
    )(x, g)


def _channel_dft_table():
    k = jnp.arange(HEAD_DIM, dtype=jnp.int32)
    ang = ((k[:, None] * k[None, :]) % HEAD_DIM).astype(_F32) * (2.0 * math.pi / HEAD_DIM)
    groups = FOURIER_W // HEAD_DIM
    eye = jnp.eye(groups, dtype=_F32)
    c = jnp.kron(eye, jnp.cos(ang)) / math.sqrt(HEAD_DIM)
    s = jnp.kron(eye, jnp.sin(ang)) / math.sqrt(HEAD_DIM)
    return jnp.concatenate([c, s], axis=1).astype(_BF16)


def _split_hi_lo(w):
    hi = w.astype(_BF16)
    lo = (w - hi.astype(_F32)).astype(_BF16)
    return jnp.concatenate([hi, lo], axis=-1)


def _seq_dft_tables(s):
    radix = 64
    assert s % radix == 0
    t = jnp.arange(s, dtype=jnp.int32)[None, :]
    k1 = jnp.arange(s // radix, dtype=jnp.int32)[:, None]
    k0 = jnp.arange(radix, dtype=jnp.int32)[:, None]
    ang_a = ((k1 * radix * t) % s).astype(_F32) * (2.0 * math.pi / s)
    ang_b = ((k0 * t) % s).astype(_F32) * (2.0 * math.pi / s)
    ca, sa = jnp.cos(ang_a)[:, None, :], jnp.sin(ang_a)[:, None, :]
    cb, sb = jnp.cos(ang_b)[None, :, :], jnp.sin(ang_b)[None, :, :]
    cos = (ca * cb - sa * sb).astype(_BF16).reshape(s, s)
    nsin = (-(sa * cb + ca * sb)).astype(_BF16).reshape(s, s)
    return cos, nsin


def _encoder(x, mod, p, dft64, seq_tables, tile, erows=None):
    b, s, d = x.shape
    n = b * s
    cap = CAPACITY_FACTOR * n // N_EXPERTS
    if erows is None:
        erows = min(EXPERT_MAX_ROWS, cap // 4)
    assert cap % (2 * erows) == 0 and cap // erows >= 2
    wc, wsn = seq_tables
    depth = p["w_in"].shape[0]
    for l in range(depth):
        mod_l = mod[l][:, None, :]
        yac, fc, fs = _mix_in(x, mod_l, p["g_mix"][l][None], p["w_in_bf"][l], p["conv_w"][l],
                              p["g_sgu"][l][None], p["w_sp_bf"][l], p["b_sp_rows"][l],
                              p["g_grp"][l][None], dft64, tile)
        yf = _seq_dft(wc, wsn, fc, fs)
        xh, aff = _mix_out(x, yac, yf, mod_l, p["g_grp"][l][None], p["w_out_bf"][l],
                           p["g_ffn"][l][None], p["w_router_hl"][l], tile)
        aff_t = aff.reshape(n, N_EXPERTS).T.reshape(N_EXPERTS, n // LANES, LANES)
        tok_rep, gate_rep, idx = _route(aff_t, cap)
        idx_blocks = idx.reshape(N_EXPERTS * cap // erows, 1, erows)
        gate_f = mod_l[:, 0, 5 * d:6 * d]
        x = _experts(idx_blocks, xh.reshape(n, 2 * d),
                     tok_rep.reshape(N_EXPERTS * cap, LANES), gate_rep.reshape(N_EXPERTS * cap, LANES),
                     gate_f, p["w_gate"], p["w_up"], p["w_down"], l, s, erows).reshape(b, s, 2 * d)
    return _final_norm(x, p["g_final"][None], tile)


def kernel(x_prompt, x_sample, c_prompt, c_sample, w_ada, b_ada, g_mix, w_in, conv_w, g_sgu, w_spatial,
           b_spatial, g_grp, w_out, g_ffn, w_router, w_gate, w_up, w_down, g_final):
    nb = x_prompt.shape[0]
    mod = _ada(jnp.concatenate([c_prompt, c_sample], axis=0), w_ada, b_ada)
    p = dict(
        g_mix=g_mix, conv_w=conv_w, g_sgu=g_sgu, g_grp=g_grp, g_ffn=g_ffn, w_router_hl=_split_hi_lo(w_router),
        w_gate=w_gate, w_up=w_up, w_down=w_down, g_final=g_final, w_in=w_in,
        w_in_bf=w_in.astype(_BF16), w_out_bf=w_out.astype(_BF16), w_sp_bf=w_spatial.astype(_BF16),
        b_sp_rows=jnp.repeat(jnp.swapaxes(b_spatial, 1, 2), HEAD_DIM, axis=2),
    )
    dft64 = _channel_dft_table()
    tile = min(SEQ_TILE, x_sample.shape[1])
    y_prompt = _encoder(x_prompt, mod[:, :nb], p, dft64, _seq_dft_tables(x_prompt.shape[1]), tile)
    y_sample = _encoder(x_sample, mod[:, nb:], p, dft64, _seq_dft_tables(x_sample.shape[1]), tile)
    return (y_prompt, y_sample)
```

```python
import functools
import math

import jax
import jax.numpy as jnp
from jax import lax
from jax.experimental import pallas as pl
from jax.experimental.pallas import tpu as pltpu

D_MODEL = 1024
HEAD_DIM = 64
CONV_W = 384
FOURIER_W = 256
SGU_W = 384
SGU_HEADS = SGU_W // HEAD_DIM
CHUNK = 128
IN_W = 3 * CONV_W + FOURIER_W + 2 * SGU_W
N_EXPERTS = 16
CAPACITY_FACTOR = 2
N_MOD = 6
EPS = 1e-6

LANES = 128
SUBLANES = 8
VMEM_LIMIT_BYTES = 52 * 1024 * 1024
MXU_COLS = 256
EXPERT_COL_CHUNK = MXU_COLS
EXPERT_MAX_ROWS = 512
SEQ_TILE = 1024

_HI = lax.Precision.HIGHEST
_BF16 = jnp.bfloat16
_F32 = jnp.float32


def _rms(x):
    return x * lax.rsqrt(jnp.mean(x * x, axis=-1, keepdims=True) + EPS)


def _params(sem, vmem=VMEM_LIMIT_BYTES):
    return pltpu.CompilerParams(dimension_semantics=sem, vmem_limit_bytes=vmem)


def _ada_kernel(c_ref, w_ref, b_ref, o_ref):
    c = c_ref[...]
    a = c * jax.nn.sigmoid(c)
    o_ref[0] = jnp.dot(a, w_ref[0], precision=_HI, preferred_element_type=_F32) + b_ref[0]


def _ada(c_all, w_ada, b_ada):
    depth, d, m = w_ada.shape
    nb = c_all.shape[0]
    tn = 1536
    return pl.pallas_call(
        _ada_kernel,
        grid=(depth, m // tn),
        in_specs=[
            pl.BlockSpec((nb, d), lambda l, j: (0, 0)),
            pl.BlockSpec((1, d, tn), lambda l, j: (l, 0, j)),
            pl.BlockSpec((1, 1, tn), lambda l, j: (l, 0, j)),
        ],
        out_specs=pl.BlockSpec((1, nb, tn), lambda l, j: (l, 0, j)),
        out_shape=jax.ShapeDtypeStruct((depth, nb, m), _F32),
        compiler_params=_params(("arbitrary", "arbitrary")),
        name="ada",
    )(c_all, w_ada, b_ada.reshape(depth, 1, m))


def _mix_in_kernel(x_ref, xp_ref, xn_ref, mod_ref, gmix_ref, win_ref, convw_ref, gsgu_ref,
                   wsp_ref, bsp_ref, ggrp_ref, dft_ref, yac_ref, fc_ref, fs_ref, *, tile):
    t = pl.program_id(1)
    nt = pl.num_programs(1)
    shift = mod_ref[0, :, 0:D_MODEL]
    scale = mod_ref[0, :, D_MODEL:2 * D_MODEL]
    gain = gmix_ref[...] * (1.0 + scale)

    def modulated(x):
        return (_rms(x) * gain + shift).astype(_BF16)

    h = modulated(x_ref[0])
    z = jnp.dot(h, win_ref[...], preferred_element_type=_F32)

    def halo(xh_ref):
        zh = jnp.dot(modulated(xh_ref[0]), win_ref[:, CONV_W:3 * CONV_W],
                     preferred_element_type=_F32)
        return zh[:, :CONV_W] * zh[:, CONV_W:]

    a_before = jnp.where(t > 0, halo(xp_ref)[SUBLANES - 1:SUBLANES, :], 0.0)
    a_after = jnp.where(t < nt - 1, halo(xn_ref)[0:1, :], 0.0)

    zb = z[:, 0:CONV_W]
    a = z[:, CONV_W:2 * CONV_W] * z[:, 2 * CONV_W:3 * CONV_W]
    row = lax.broadcasted_iota(jnp.int32, (tile, CONV_W), 0)
    a_prev = jnp.where(row == 0, a_before, pltpu.roll(a, 1, axis=0))
    a_next = jnp.where(row == tile - 1, a_after, pltpu.roll(a, tile - 1, axis=0))
    cw = convw_ref[...]
    conv = zb * (cw[0:1, :] * a_prev + cw[1:2, :] * a + cw[2:3, :] * a_next)
    ya = (_rms(conv) * ggrp_ref[:, 0:CONV_W]).astype(_BF16)

    zf = z[:, 3 * CONV_W:3 * CONV_W + FOURIER_W].astype(_BF16)
    f2 = jnp.dot(zf, dft_ref[...], preferred_element_type=_F32)
    fc_ref[...] = f2[:, :FOURIER_W].astype(_BF16)
    fs_ref[...] = f2[:, FOURIER_W:].astype(_BF16)

    off = 3 * CONV_W + FOURIER_W
    u = jax.nn.gelu(z[:, off:off + SGU_W])
    v = (_rms(jax.nn.gelu(z[:, off + SGU_W:off + 2 * SGU_W])) * gsgu_ref[...]).astype(_BF16)
    lane = lax.broadcasted_iota(jnp.int32, (CHUNK, LANES), 1)
    first_head = lane < HEAD_DIM
    chunks = []
    for c in range(tile // CHUNK):
        pairs = []
        for k in range(SGU_W // LANES):
            vk = v[c * CHUNK:(c + 1) * CHUNK, k * LANES:(k + 1) * LANES]
            r0 = jnp.dot(wsp_ref[2 * k], vk, preferred_element_type=_F32)
            r1 = jnp.dot(wsp_ref[2 * k + 1], vk, preferred_element_type=_F32)
            pairs.append(jnp.where(first_head, r0, r1))
        chunks.append(jnp.concatenate(pairs, axis=1) + bsp_ref[...])
    sv = jnp.concatenate(chunks, axis=0)
    yc = (_rms(u * sv) * ggrp_ref[:, CONV_W + FOURIER_W:]).astype(_BF16)
    yac_ref[0] = jnp.concatenate([ya, yc], axis=1)


def _mix_in(x, mod_l, gmix, win_bf, convw, gsgu, wsp_bf, bsp_rows, ggrp, dft64, tile):
    b, s, d = x.shape[0], x.shape[1], D_MODEL
    nt = s // tile
    hb = tile // SUBLANES
    last_hb = s // SUBLANES - 1
    const2 = lambda i, t: (0, 0)
    return pl.pallas_call(
        functools.partial(_mix_in_kernel, tile=tile),
        grid=(b, nt),
        in_specs=[
            pl.BlockSpec((1, tile, d), lambda i, t: (i, t, 0)),
            pl.BlockSpec((1, SUBLANES, d), lambda i, t: (i, jnp.maximum(t * hb - 1, 0), 0)),
            pl.BlockSpec((1, SUBLANES, d), lambda i, t: (i, jnp.minimum((t + 1) * hb, last_hb), 0)),
            pl.BlockSpec((1, 1, N_MOD * d), lambda i, t: (i, 0, 0)),
            pl.BlockSpec((1, d), const2),
            pl.BlockSpec((d, IN_W), const2),
            pl.BlockSpec((3, CONV_W), const2),
            pl.BlockSpec((1, SGU_W), const2),
            pl.BlockSpec((SGU_HEADS, CHUNK, CHUNK), lambda i, t: (0, 0, 0)),
            pl.BlockSpec((CHUNK, SGU_W), const2),
            pl.BlockSpec((1, d), const2),
            pl.BlockSpec((FOURIER_W, 2 * FOURIER_W), const2),
        ],
        out_specs=[
            pl.BlockSpec((1, tile, CONV_W + SGU_W), lambda i, t: (i, t, 0)),
            pl.BlockSpec((tile, FOURIER_W), lambda i, t: (t, i)),
            pl.BlockSpec((tile, FOURIER_W), lambda i, t: (t, i)),
        ],
        out_shape=[
            jax.ShapeDtypeStruct((b, s, CONV_W + SGU_W), _BF16),
            jax.ShapeDtypeStruct((s, b * FOURIER_W), _BF16),
            jax.ShapeDtypeStruct((s, b * FOURIER_W), _BF16),
        ],
        compiler_params=_params(("arbitrary", "arbitrary")),
        name="mix_in",
    )(x, x, x, mod_l, gmix, win_bf, convw, gsgu, wsp_bf, bsp_rows, ggrp, dft64)


def _seq_dft_kernel(wc_ref, ws_ref, fc_ref, fs_ref, fcm_ref, fsm_ref, mid_ref, o_ref, *, tm):
    gc = (fc_ref[...].astype(_F32) + fcm_ref[...].astype(_F32)).astype(_BF16)
    gs = (fs_ref[...].astype(_F32) - fsm_ref[...].astype(_F32)).astype(_BF16)
    part = (jnp.dot(wc_ref[...], gc, preferred_element_type=_F32)
            + jnp.dot(ws_ref[...], gs, preferred_element_type=_F32))

    @pl.when(pl.program_id(2) == 0)
    def _():
        k = pl.program_id(0) * tm + lax.broadcasted_iota(jnp.int32, (tm, 1), 0)
        sign = (1 - 2 * (k & 1)).astype(_F32)
        o_ref[...] = part + sign * mid_ref[0:1, :].astype(_F32)

    @pl.when(pl.program_id(2) > 0)
    def _():
        o_ref[...] += part


def _seq_dft(wc, wsn, fc, fs):
    s = wc.shape[0]
    n = fc.shape[1]
    half = s // 2
    tm = min(1024, s)
    tn = min(1024, n)
    tk = min(1024, half)

    def mirrored(f):
        return jnp.concatenate([jnp.zeros_like(f[:1]), jnp.flip(f[half + 1:], axis=0)], axis=0)

    return pl.pallas_call(
        functools.partial(_seq_dft_kernel, tm=tm),
        grid=(s // tm, n // tn, half // tk),
        in_specs=[
            pl.BlockSpec((tm, tk), lambda i, j, k: (i, k)),
            pl.BlockSpec((tm, tk), lambda i, j, k: (i, k)),
            pl.BlockSpec((tk, tn), lambda i, j, k: (k, j)),
            pl.BlockSpec((tk, tn), lambda i, j, k: (k, j)),
            pl.BlockSpec((tk, tn), lambda i, j, k: (k, j)),
            pl.BlockSpec((tk, tn), lambda i, j, k: (k, j)),
            pl.BlockSpec((SUBLANES, tn), lambda i, j, k: (0, j)),
        ],
        out_specs=pl.BlockSpec((tm, tn), lambda i, j, k: (i, j)),
        out_shape=jax.ShapeDtypeStruct((s, n), _F32),
        compiler_params=_params(("arbitrary", "arbitrary", "arbitrary")),
        name="seq_dft",
    )(wc, wsn, fc, fs, mirrored(fc), mirrored(fs), fc[half:half + SUBLANES])


def _mix_out_kernel(x_ref, yac_ref, yf_ref, mod_ref, ggrp_ref, wout_ref, gffn_ref, wr_ref,
                    xh_ref, aff_ref, *, seq_scale):
    d = D_MODEL
    gate_m = mod_ref[0, :, 2 * d:3 * d]
    shift_f = mod_ref[0, :, 3 * d:4 * d]
    scale_f = mod_ref[0, :, 4 * d:5 * d]
    yf = (_rms(yf_ref[...] * seq_scale) * ggrp_ref[:, CONV_W:CONV_W + FOURIER_W]).astype(_BF16)
    yac = yac_ref[0]
    ycat = jnp.concatenate([yac[:, :CONV_W], yf, yac[:, CONV_W:]], axis=1)
    mix = jnp.dot(ycat, wout_ref[...], preferred_element_type=_F32)
    x1 = x_ref[0] + gate_m * mix
    h2 = _rms(x1) * (gffn_ref[...] * (1.0 + scale_f)) + shift_f
    xh_ref[0, :, :d] = x1
    xh_ref[0, :, d:] = h2
    h_hi = h2.astype(_BF16)
    h_lo = (h2 - h_hi.astype(_F32)).astype(_BF16)
    by_hi = jnp.dot(h_hi, wr_ref[...], preferred_element_type=_F32)
    logits = (by_hi[:, :N_EXPERTS] + (by_hi[:, N_EXPERTS:]
              + jnp.dot(h_lo, wr_ref[:, :N_EXPERTS], preferred_element_type=_F32)))
    e = jnp.exp(logits - jnp.max(logits, axis=-1, keepdims=True))
    aff_ref[0] = e / jnp.sum(e, axis=-1, keepdims=True)


def _mix_out(x, yac, yf, mod_l, ggrp, wout_bf, gffn, w_router, tile):
    b, s, d = x.shape[0], x.shape[1], D_MODEL
    const2 = lambda i, t: (0, 0)
    return pl.pallas_call(
        functools.partial(_mix_out_kernel, seq_scale=1.0 / math.sqrt(s)),
        grid=(b, s // tile),
        in_specs=[
            pl.BlockSpec((1, tile, d), lambda i, t: (i, t, 0)),
            pl.BlockSpec((1, tile, CONV_W + SGU_W), lambda i, t: (i, t, 0)),
            pl.BlockSpec((tile, FOURIER_W), lambda i, t: (t, i)),
            pl.BlockSpec((1, 1, N_MOD * d), lambda i, t: (i, 0, 0)),
            pl.BlockSpec((1, d), const2),
            pl.BlockSpec((d, d), const2),
            pl.BlockSpec((1, d), const2),
            pl.BlockSpec((d, 2 * N_EXPERTS), const2),
        ],
        out_specs=[
            pl.BlockSpec((1, tile, 2 * d), lambda i, t: (i, t, 0)),
            pl.BlockSpec((1, tile, N_EXPERTS), lambda i, t: (i, t, 0)),
        ],
        out_shape=[
            jax.ShapeDtypeStruct((b, s, 2 * d), _F32),
            jax.ShapeDtypeStruct((b, s, N_EXPERTS), _F32),
        ],
        compiler_params=_params(("arbitrary", "arbitrary")),
        name="mix_out",
    )(x, yac, yf, mod_l, ggrp, wout_bf, gffn, w_router)


def _cumsum_mats(rows):
    qi = lax.broadcasted_iota(jnp.int32, (LANES, LANES), 0)
    qj = lax.broadcasted_iota(jnp.int32, (LANES, LANES), 1)
    ri = lax.broadcasted_iota(jnp.int32, (rows, rows), 0)
    rj = lax.broadcasted_iota(jnp.int32, (rows, rows), 1)
    def ones_where(mask):
        return jnp.where(mask, 1.0, 0.0).astype(_BF16)

    lane_excl = ones_where(qi < qj)
    lane_incl = ones_where(qi <= qj)
    row_excl = ones_where(rj < ri)
    row_incl_t = ones_where(ri <= rj)
    return lane_excl, lane_incl, row_excl, row_incl_t


def _route_kernel(aff_ref, idx_ref, gate_ref, idxc_ref, *, cap, jb):
    a = aff_ref[0]
    rows = a.shape[0]
    lane_excl, lane_incl, row_excl, row_incl_t = _cumsum_mats(rows)

    def count(mask):
        return jnp.sum(jnp.where(mask, 1.0, 0.0)).astype(jnp.int32)

    def refine(thr_bits, shift, nbits):
        passing = jnp.int32(0)
        for m in range(1, 1 << nbits):
            cand_f = lax.bitcast_convert_type(thr_bits | (jnp.int32(m) << shift), _F32)
            passing += (count(a >= cand_f) >= cap).astype(jnp.int32)
        return thr_bits | (passing << shift)

    thr_bits = lax.fori_loop(0, 10, lambda i, t: refine(t, 27 - 3 * i, 3), jnp.int32(0))
    thr = lax.bitcast_convert_type(thr_bits, _F32)
    above = a > thr
    tied = a == thr
    need = cap - count(above)

    tied_b = jnp.where(tied, 1.0, 0.0).astype(_BF16)
    in_row = jnp.dot(tied_b, lane_excl, preferred_element_type=_F32)
    row_tot = jnp.dot(tied_b, jnp.ones((LANES, LANES), _BF16), preferred_element_type=_F32)
    before_row = jnp.dot(row_excl, row_tot.astype(_BF16), preferred_element_type=_F32)
    tie_rank = (in_row + before_row).astype(jnp.int32)
    sel = above | (tied & (tie_rank < need))
    sel_b = jnp.where(sel, 1.0, 0.0).astype(_BF16)

    row_cnt = lax.dot_general(jnp.ones((SUBLANES, LANES), _BF16), sel_b,
                              (((1,), (1,)), ((), ())), preferred_element_type=_F32)
    incl = jnp.dot(row_cnt.astype(_BF16), row_incl_t, preferred_element_type=_F32)[0:1, :]
    excl = incl - row_cnt[0:1, :]
    a_hi = a.astype(_BF16)
    a_mid = (a - a_hi.astype(_F32)).astype(_BF16)
    a_lo = (a - a_hi.astype(_F32) - a_mid.astype(_F32)).astype(_BF16)
    row_id = lax.broadcasted_iota(jnp.int32, (1, rows), 1).astype(_F32)
    lane_id = lax.broadcasted_iota(jnp.int32, (jb, LANES), 1).astype(_F32)
    diag = (lax.broadcasted_iota(jnp.int32, (LANES, LANES), 0)
            == lax.broadcasted_iota(jnp.int32, (LANES, LANES), 1))

    for c in range(cap // jb):
        j = (lax.broadcasted_iota(jnp.int32, (jb, 1), 0) + c * jb).astype(_F32)
        hit = (excl <= j) & (j < incl)
        hit_f = jnp.where(hit, 1.0, 0.0)
        r_of_j = jnp.sum(hit_f * row_id, axis=1, keepdims=True)
        target = j + 1.0 - jnp.sum(hit_f * excl, axis=1, keepdims=True)
        row_sel = jnp.dot(hit_f.astype(_BF16), sel_b, preferred_element_type=_F32)
        within = jnp.dot(row_sel.astype(_BF16), lane_incl, preferred_element_type=_F32)
        l_of_j = jnp.sum(jnp.where(within < target, 1.0, 0.0), axis=1, keepdims=True)
        hit_b = hit_f.astype(_BF16)
        a_row = (jnp.dot(hit_b, a_hi, preferred_element_type=_F32)
                 + jnp.dot(hit_b, a_mid, preferred_element_type=_F32)
                 + jnp.dot(hit_b, a_lo, preferred_element_type=_F32))
        gate = jnp.sum(jnp.where(lane_id == l_of_j, a_row, 0.0), axis=1, keepdims=True)
        token_rep = jnp.broadcast_to(r_of_j * LANES + l_of_j, (jb, LANES))
        idx_ref[0, c * jb:(c + 1) * jb, :] = token_rep.astype(jnp.int32)
        gate_ref[0, c * jb:(c + 1) * jb, :] = jnp.broadcast_to(gate, (jb, LANES))
        for g in range(jb // LANES):
            grp = jnp.where(diag, token_rep[g * LANES:(g + 1) * LANES, :], 0.0)
            out_row = c * (jb // LANES) + g
            idxc_ref[0, out_row:out_row + 1, :] = jnp.sum(grp, axis=0, keepdims=True).astype(jnp.int32)


def _route(aff_t, cap):
    e, rows, _ = aff_t.shape
    jb = min(512, cap)
    return pl.pallas_call(
        functools.partial(_route_kernel, cap=cap, jb=jb),
        grid=(e,),
        in_specs=[pl.BlockSpec((1, rows, LANES), lambda i: (i, 0, 0))],
        out_specs=[
            pl.BlockSpec((1, cap, LANES), lambda i: (i, 0, 0)),
            pl.BlockSpec((1, cap, LANES), lambda i: (i, 0, 0)),
            pl.BlockSpec((1, cap // LANES, LANES), lambda i: (i, 0, 0)),
        ],
        out_shape=[
            jax.ShapeDtypeStruct((e, cap, LANES), jnp.int32),
            jax.ShapeDtypeStruct((e, cap, LANES), _F32),
            jax.ShapeDtypeStruct((e, cap // LANES, LANES), jnp.int32),
        ],
        compiler_params=_params(("arbitrary",)),
        name="route",
    )(aff_t)


def _experts_kernel(idx_hbm, xin_hbm, tok_ref, gate_ref, gf_ref, wg_ref, wu_ref, wd_ref,
                    xh_hbm, idx_p, idx_a, idx_b, idx_n, idx_m, prev_first, buf_a, buf_b, wb_a, wb_b, sems,
                    *, rows, nsteps, seq_shift, nbatch, col_chunk):
    del xin_hbm
    d = D_MODEL
    e = pl.program_id(0)
    j = pl.program_id(1)
    step = e * nsteps + j
    last_step = pl.num_programs(0) * nsteps - 1
    blk_a = 2 * step
    blk_p = jnp.maximum(blk_a - 1, 0)
    last_blk = 2 * last_step + 1
    blk_n = jnp.minimum(blk_a + 2, last_blk)
    blk_m = jnp.minimum(blk_a + 3, last_blk)
    SEM_IDX, SEM_GA, SEM_GB, SEM_SA, SEM_SB, SEM_FIX = range(6)

    idx_copies = [pltpu.make_async_copy(idx_hbm.at[b], ref, sems.at[SEM_IDX])
                  for b, ref in ((blk_p, idx_p), (blk_a, idx_a), (blk_a + 1, idx_b), (blk_n, idx_n),
                                 (blk_m, idx_m))]
    for c in idx_copies:
        c.start()
    for c in idx_copies:
        c.wait()

    res_cols = pl.ds(0, d)
    moe_cols = pl.ds(d, d)

    def row_copy(idx_ref, r, buf, sem, kind):
        tok = idx_ref[0, r]
        if kind == "store":
            return pltpu.make_async_copy(buf.at[pl.ds(r, 1)], xh_hbm.at[pl.ds(tok, 1), res_cols], sems.at[sem])
        if kind == "fetch":
            return pltpu.make_async_copy(xh_hbm.at[pl.ds(tok, 1)], buf.at[pl.ds(r, 1)], sems.at[sem])
        cols = moe_cols if kind == "fetch_moe" else res_cols
        return pltpu.make_async_copy(xh_hbm.at[pl.ds(tok, 1), cols], buf.at[pl.ds(r, 1), cols], sems.at[sem])

    def start_all(idx_ref, buf, sem, kind):
        def body(r, carry):
            row_copy(idx_ref, r, buf, sem, kind).start()
            return carry
        lax.fori_loop(0, rows, body, 0)

    def wait_all(buf, sem, kind):
        all_rows = pl.ds(0, rows)
        if kind == "store":
            pltpu.make_async_copy(buf, xh_hbm.at[all_rows, res_cols], sems.at[sem]).wait()
        elif kind == "fetch":
            pltpu.make_async_copy(xh_hbm.at[all_rows], buf, sems.at[sem]).wait()
        else:
            cols = moe_cols if kind == "fetch_moe" else res_cols
            pltpu.make_async_copy(xh_hbm.at[all_rows, cols], buf.at[:, cols], sems.at[sem]).wait()

    def block(part, buf, sem_g, wb, streams, fetched=False):
        if not fetched:
            wait_all(buf, sem_g, "fetch")
        todo = [functools.partial(row_copy, stream[0], r, *stream[1:])
                for r in range(rows) for stream in streams]
        nch = D_MODEL // col_chunk
        per_call = -(-len(todo) // (3 * nch))

        def start_some():
            for k, make in enumerate(todo[:per_call]):
                make().start(priority=k % 2)
            del todo[:per_call]

        x = buf[:, d:].astype(_BF16)
        hid = []
        for c in range(nch):
            cs = slice(c * col_chunk, (c + 1) * col_chunk)
            hg = jnp.dot(x, wg_ref[0, 0, :, cs].astype(_BF16), preferred_element_type=_F32)
            start_some()
            hu = jnp.dot(x, wu_ref[0, 0, :, cs].astype(_BF16), preferred_element_type=_F32)
            start_some()
            hid.append((hg * jax.nn.sigmoid(hg) * hu).astype(_BF16))
        hid = jnp.concatenate(hid, axis=1)
        rs = slice(part * rows, (part + 1) * rows)
        seq_of_row = tok_ref[rs, :] >> seq_shift
        gate = gate_ref[rs, :]
        upd = []
        for c in range(nch):
            cs = slice(c * col_chunk, (c + 1) * col_chunk)
            y = jnp.dot(hid, wd_ref[0, 0, :, cs].astype(_BF16), preferred_element_type=_F32)
            start_some()
            seq_c = jnp.concatenate([seq_of_row] * (col_chunk // LANES), axis=1)
            gf = jnp.zeros_like(y)
            for b in range(nbatch):
                gf = jnp.where(seq_c == b, gf_ref[b:b + 1, cs], gf)
            upd.append(gf * (y * jnp.concatenate([gate] * (col_chunk // LANES), axis=1)))
        assert not todo
        wb[...] = buf[:, :d] + jnp.concatenate(upd, axis=1)

    fetch_b = (idx_b, buf_b, SEM_GB, "fetch")
    fetch_n = (idx_n, buf_a, SEM_GA, "fetch")
    fetch_n_moe = (idx_n, buf_a, SEM_GA, "fetch_moe")
    store_p = (idx_p, wb_b, SEM_SB, "store")
    store_a = (idx_a, wb_a, SEM_SA, "store")
    store_b = (idx_b, wb_b, SEM_SB, "store")

    @pl.when(step == 0)
    def _():
        start_all(idx_a, buf_a, SEM_GA, "fetch")
        prev_first[0] = 0

    @pl.when(step > 0)
    def _():
        wait_all(wb_a, SEM_SA, "store")

    def overlaps(first, last, other_first, other_last):
        return jnp.logical_not((other_last < first) | (last < other_first))

    serial = (step == 0) | overlaps(idx_a[0, 0], idx_b[0, rows - 1], prev_first[0], idx_p[0, rows - 1])
    next_serial = overlaps(idx_n[0, 0], idx_m[0, rows - 1], idx_a[0, 0], idx_b[0, rows - 1])
    prev_first[0] = idx_a[0, 0]

    @pl.when(serial)
    def _():
        @pl.when(step == 0)
        def _():
            wait_all(buf_a, SEM_GA, "fetch")

        @pl.when(step > 0)
        def _():
            start_all(*store_p)
            wait_all(wb_b, SEM_SB, "store")
            wait_all(buf_a, SEM_GA, "fetch_moe")
            start_all(idx_a, buf_a, SEM_FIX, "fetch_res")
            wait_all(buf_a, SEM_FIX, "fetch_res")
        block(0, buf_a, SEM_GA, wb_a, [fetch_b], fetched=True)

    @pl.when(jnp.logical_not(serial))
    def _():
        block(0, buf_a, SEM_GA, wb_a, [store_p, fetch_b])
        wait_all(wb_b, SEM_SB, "store")

    @pl.when(next_serial)
    def _():
        block(1, buf_b, SEM_GB, wb_b, [store_a, fetch_n_moe])

    @pl.when(jnp.logical_not(next_serial))
    def _():
        block(1, buf_b, SEM_GB, wb_b, [store_a, fetch_n])

    @pl.when(step == last_step)
    def _():
        wait_all(buf_a, SEM_GA, "fetch_moe")
        start_all(*store_b)
        wait_all(wb_a, SEM_SA, "store")
        wait_all(wb_b, SEM_SB, "store")


def _experts(idx_blocks, xh, tok_rep, gate_rep, gate_f, w_gate, w_up, w_down, layer, seq, rows):
    n, d = xh.shape[0], D_MODEL
    e = w_gate.shape[1]
    cap = tok_rep.shape[0] // e
    nsteps = cap // (2 * rows)
    nbatch = gate_f.shape[0]
    seq_shift = seq.bit_length() - 1
    assert 1 << seq_shift == seq and nsteps * 2 * rows == cap
    wspec = pl.BlockSpec((1, 1, d, d), lambda i, j: (layer, i, 0, 0))
    any_spec = pl.BlockSpec(memory_space=pl.ANY)
    return pl.pallas_call(
        functools.partial(_experts_kernel, rows=rows, nsteps=nsteps, seq_shift=seq_shift, nbatch=nbatch,
                          col_chunk=EXPERT_COL_CHUNK),
        grid=(e, nsteps),
        in_specs=[
            any_spec, any_spec,
            pl.BlockSpec((2 * rows, LANES), lambda i, j: (i * nsteps + j, 0)),
            pl.BlockSpec((2 * rows, LANES), lambda i, j: (i * nsteps + j, 0)),
            pl.BlockSpec((nbatch, d), lambda i, j: (0, 0)),
            wspec, wspec, wspec,
        ],
        out_specs=any_spec,
        out_shape=jax.ShapeDtypeStruct((n, 2 * d), _F32),
        scratch_shapes=[
            pltpu.SMEM((1, rows), jnp.int32),
            pltpu.SMEM((1, rows), jnp.int32),
            pltpu.SMEM((1, rows), jnp.int32),
            pltpu.SMEM((1, rows), jnp.int32),
            pltpu.SMEM((1, rows), jnp.int32),
            pltpu.SMEM((1,), jnp.int32),
            pltpu.VMEM((rows, 2 * d), _F32),
            pltpu.VMEM((rows, 2 * d), _F32),
            pltpu.VMEM((rows, d), _F32),
            pltpu.VMEM((rows, d), _F32),
            pltpu.SemaphoreType.DMA((6,)),
        ],
        input_output_aliases={1: 0},
        compiler_params=_params(("arbitrary", "arbitrary")),
        name="experts",
    )(idx_blocks, xh, tok_rep, gate_rep, gate_f, w_gate, w_up, w_down)


def _final_kernel(x_ref, g_ref, o_ref):
    o_ref[0] = _rms(x_ref[0]) * g_ref[...]


def _final_norm(x, g, tile):
    b, s, d = x.shape[0], x.shape[1], D_MODEL
    return pl.pallas_call(
        _final_kernel,
        grid=(b, s // tile),
        in_specs=[pl.BlockSpec((1, tile, d), lambda i, t: (i, t, 0)),
                  pl.BlockSpec((1, d), lambda i, t: (0, 0))],
        out_specs=pl.BlockSpec((1, tile, d), lambda i, t: (i, t, 0)),
        out_shape=jax.ShapeDtypeStruct((b, s, d), _F32),
        compiler_params=_params(("arbitrary", "arbitrary")),
        name="final_norm",
    )(x, g)


def _channel_dft_table():
    k = jnp.arange(HEAD_DIM, dtype=jnp.int32)
    ang = ((k[:, None] * k[None, :]) % HEAD_DIM).astype(_F32) * (2.0 * math.pi / HEAD_DIM)
    groups = FOURIER_W // HEAD_DIM
    eye = jnp.eye(groups, dtype=_F32)
    c = jnp.kron(eye, jnp.cos(ang)) / math.sqrt(HEAD_DIM)
    s = jnp.kron(eye, jnp.sin(ang)) / math.sqrt(HEAD_DIM)
    return jnp.concatenate([c, s], axis=1).astype(_BF16)


def _split_hi_lo(w):
    hi = w.astype(_BF16)
    lo = (w - hi.astype(_F32)).astype(_BF16)
    return jnp.concatenate([hi, lo], axis=-1)


def _seq_dft_tables(s):
    radix = 64
    assert s % radix == 0
    t = jnp.arange(s, dtype=jnp.int32)[None, :]
    k1 = jnp.arange(s // radix, dtype=jnp.int32)[:, None]
    k0 = jnp.arange(radix, dtype=jnp.int32)[:, None]
    ang_a = ((k1 * radix * t) % s).astype(_F32) * (2.0 * math.pi / s)
    ang_b = ((k0 * t) % s).astype(_F32) * (2.0 * math.pi / s)
    ca, sa = jnp.cos(ang_a)[:, None, :], jnp.sin(ang_a)[:, None, :]
    cb, sb = jnp.cos(ang_b)[None, :, :], jnp.sin(ang_b)[None, :, :]
    cos = (ca * cb - sa * sb).astype(_BF16).reshape(s, s)
    nsin = (-(sa * cb + ca * sb)).astype(_BF16).reshape(s, s)
    return cos, nsin


def _encoder(x, mod, p, dft64, seq_tables, tile, erows=None):
    b, s, d = x.shape
    n = b * s
    cap = CAPACITY_FACTOR * n // N_EXPERTS
    if erows is None:
        erows = min(EXPERT_MAX_ROWS, cap // 4)
    assert cap % (2 * erows) == 0 and cap // erows >= 2
    wc, wsn = seq_tables
    depth = p["w_in"].shape[0]
    for l in range(depth):
        mod_l = mod[l][:, None, :]
        yac, fc, fs = _mix_in(x, mod_l, p["g_mix"][l][None], p["w_in_bf"][l], p["conv_w"][l],
                              p["g_sgu"][l][None], p["w_sp_bf"][l], p["b_sp_rows"][l],
                              p["g_grp"][l][None], dft64, tile)
        yf = _seq_dft(wc, wsn, fc, fs)
        xh, aff = _mix_out(x, yac, yf, mod_l, p["g_grp"][l][None], p["w_out_bf"][l],
                           p["g_ffn"][l][None], p["w_router_hl"][l], tile)
        aff_t = aff.reshape(n, N_EXPERTS).T.reshape(N_EXPERTS, n // LANES, LANES)
        tok_rep, gate_rep, idx = _route(aff_t, cap)
        idx_blocks = idx.reshape(N_EXPERTS * cap // erows, 1, erows)
        gate_f = mod_l[:, 0, 5 * d:6 * d]
        x = _experts(idx_blocks, xh.reshape(n, 2 * d),
                     tok_rep.reshape(N_EXPERTS * cap, LANES), gate_rep.reshape(N_EXPERTS * cap, LANES),
                     gate_f, p["w_gate"], p["w_up"], p["w_down"], l, s, erows).reshape(b, s, 2 * d)
    return _final_norm(x, p["g_final"][None], tile)


def kernel(x_prompt, x_sample, c_prompt, c_sample, w_ada, b_ada, g_mix, w_in, conv_w, g_sgu, w_spatial,
           b_spatial, g_grp, w_out, g_ffn, w_router, w_gate, w_up, w_down, g_final):
    nb = x_prompt.shape[0]
    mod = _ada(jnp.concatenate([c_prompt, c_sample], axis=0), w_ada, b_ada)
    p = dict(
        g_mix=g_mix, conv_w=conv_w, g_sgu=g_sgu, g_grp=g_grp, g_ffn=g_ffn, w_router_hl=_split_hi_lo(w_router),
        w_gate=w_gate, w_up=w_up, w_down=w_down, g_final=g_final, w_in=w_in,
        w_in_bf=w_in.astype(_BF16), w_out_bf=w_out.astype(_BF16), w_sp_bf=w_spatial.astype(_BF16),
        b_sp_rows=jnp.repeat(jnp.swapaxes(b_spatial, 1, 2), HEAD_DIM, axis=2),
    )
    dft64 = _channel_dft_table()
    tile = min(SEQ_TILE, x_sample.shape[1])
    y_prompt = _encoder(x_prompt, mod[:, :nb], p, dft64, _seq_dft_tables(x_prompt.shape[1]), tile)
    y_sample = _encoder(x_sample, mod[:, nb:], p, dft64, _seq_dft_tables(x_sample.shape[1]), tile)
    return (y_prompt, y_sample)
```

```python
import functools
import math

import jax
import jax.numpy as jnp
from jax import lax
from jax.experimental import pallas as pl
from jax.experimental.pallas import tpu as pltpu

D_MODEL = 1024
HEAD_DIM = 64
CONV_W = 384
FOURIER_W = 256
SGU_W = 384
SGU_HEADS = SGU_W // HEAD_DIM
CHUNK = 128
IN_W = 3 * CONV_W + FOURIER_W + 2 * SGU_W
N_EXPERTS = 16
CAPACITY_FACTOR = 2
N_MOD = 6
EPS = 1e-6

LANES = 128
SUBLANES = 8
VMEM_LIMIT_BYTES = 52 * 1024 * 1024
MXU_COLS = 256
EXPERT_COL_CHUNK = MXU_COLS
EXPERT_MAX_ROWS = 512
SEQ_TILE = 1024

_HI = lax.Precision.HIGHEST
_BF16 = jnp.bfloat16
_F32 = jnp.float32


def _rms(x):
    return x * lax.rsqrt(jnp.mean(x * x, axis=-1, keepdims=True) + EPS)


def _params(sem, vmem=VMEM_LIMIT_BYTES):
    return pltpu.CompilerParams(dimension_semantics=sem, vmem_limit_bytes=vmem)


def _ada_kernel(c_ref, w_ref, b_ref, o_ref):
    c = c_ref[...]
    a = c * jax.nn.sigmoid(c)
    o_ref[0] = jnp.dot(a, w_ref[0], precision=_HI, preferred_element_type=_F32) + b_ref[0]


def _ada(c_all, w_ada, b_ada):
    depth, d, m = w_ada.shape
    nb = c_all.shape[0]
    tn = 1536
    return pl.pallas_call(
        _ada_kernel,
        grid=(depth, m // tn),
        in_specs=[
            pl.BlockSpec((nb, d), lambda l, j: (0, 0)),
            pl.BlockSpec((1, d, tn), lambda l, j: (l, 0, j)),
            pl.BlockSpec((1, 1, tn), lambda l, j: (l, 0, j)),
        ],
        out_specs=pl.BlockSpec((1, nb, tn), lambda l, j: (l, 0, j)),
        out_shape=jax.ShapeDtypeStruct((depth, nb, m), _F32),
        compiler_params=_params(("arbitrary", "arbitrary")),
        name="ada",
    )(c_all, w_ada, b_ada.reshape(depth, 1, m))


def _mix_in_kernel(x_ref, xp_ref, xn_ref, mod_ref, gmix_ref, win_ref, convw_ref, gsgu_ref,
                   wsp_ref, bsp_ref, ggrp_ref, dft_ref, flip_ref, yac_ref, fc_ref, fs_ref, fcm_ref, fsm_ref,
                   *, tile):
    t = pl.program_id(1)
    nt = pl.num_programs(1)
    shift = mod_ref[0, :, 0:D_MODEL]
    scale = mod_ref[0, :, D_MODEL:2 * D_MODEL]
    gain = gmix_ref[...] * (1.0 + scale)

    def modulated(x):
        return (_rms(x) * gain + shift).astype(_BF16)

    h = modulated(x_ref[0])
    z = jnp.dot(h, win_ref[...], preferred_element_type=_F32)

    def halo(hh):
        zh = jnp.dot(hh, win_ref[:, CONV_W:3 * CONV_W], preferred_element_type=_F32)
        return zh[:, :CONV_W] * zh[:, CONV_W:]

    h_after = modulated(xn_ref[0])
    a_before = jnp.where(t > 0, halo(modulated(xp_ref[0]))[SUBLANES - 1:SUBLANES, :], 0.0)
    a_after = jnp.where(t < nt - 1, halo(h_after)[0:1, :], 0.0)

    zb = z[:, 0:CONV_W]
    a = z[:, CONV_W:2 * CONV_W] * z[:, 2 * CONV_W:3 * CONV_W]
    row = lax.broadcasted_iota(jnp.int32, (tile, CONV_W), 0)
    a_prev = jnp.where(row == 0, a_before, pltpu.roll(a, 1, axis=0))
    a_next = jnp.where(row == tile - 1, a_after, pltpu.roll(a, tile - 1, axis=0))
    cw = convw_ref[...]
    conv = zb * (cw[0:1, :] * a_prev + cw[1:2, :] * a + cw[2:3, :] * a_next)
    ya = (_rms(conv) * ggrp_ref[:, 0:CONV_W]).astype(_BF16)

    zf = z[:, 3 * CONV_W:3 * CONV_W + FOURIER_W].astype(_BF16)
    f2 = jnp.dot(zf, dft_ref[...], preferred_element_type=_F32)
    f2 = f2.astype(_BF16)
    fc_ref[...] = f2[:, :FOURIER_W]
    fs_ref[...] = f2[:, FOURIER_W:]

    @pl.when(t >= nt // 2)
    def _():
        mirrored = jnp.dot(flip_ref[...], f2, preferred_element_type=_F32)
        zf_after = jnp.dot(h_after, win_ref[:, 3 * CONV_W:3 * CONV_W + FOURIER_W],
                           preferred_element_type=_F32).astype(_BF16)
        first = jnp.dot(zf_after, dft_ref[...], preferred_element_type=_F32)[0:1, :]
        first = jnp.where(t < nt - 1, first, 0.0)
        row2 = lax.broadcasted_iota(jnp.int32, (tile, 2 * FOURIER_W), 0)
        mirrored = jnp.where(row2 == 0, first, mirrored).astype(_BF16)
        fcm_ref[...] = mirrored[:, :FOURIER_W]
        fsm_ref[...] = mirrored[:, FOURIER_W:]

    off = 3 * CONV_W + FOURIER_W
    u = jax.nn.gelu(z[:, off:off + SGU_W])
    v = (_rms(jax.nn.gelu(z[:, off + SGU_W:off + 2 * SGU_W])) * gsgu_ref[...]).astype(_BF16)
    lane = lax.broadcasted_iota(jnp.int32, (CHUNK, LANES), 1)
    first_head = lane < HEAD_DIM
    chunks = []
    for c in range(tile // CHUNK):
        pairs = []
        for k in range(SGU_W // LANES):
            vk = v[c * CHUNK:(c + 1) * CHUNK, k * LANES:(k + 1) * LANES]
            r0 = jnp.dot(wsp_ref[2 * k], vk, preferred_element_type=_F32)
            r1 = jnp.dot(wsp_ref[2 * k + 1], vk, preferred_element_type=_F32)
            pairs.append(jnp.where(first_head, r0, r1))
        chunks.append(jnp.concatenate(pairs, axis=1) + bsp_ref[...])
    sv = jnp.concatenate(chunks, axis=0)
    yc = (_rms(u * sv) * ggrp_ref[:, CONV_W + FOURIER_W:]).astype(_BF16)
    yac_ref[0] = jnp.concatenate([ya, yc], axis=1)


def _mix_in(x, mod_l, gmix, win_bf, convw, gsgu, wsp_bf, bsp_rows, ggrp, dft64, tile):
    b, s, d = x.shape[0], x.shape[1], D_MODEL
    nt = s // tile
    assert nt % 2 == 0
    pos = jnp.arange(tile, dtype=jnp.int32)
    flip = (pos[:, None] + pos[None, :] == tile).astype(_BF16)
    hb = tile // SUBLANES
    last_hb = s // SUBLANES - 1
    const2 = lambda i, t: (0, 0)
    return pl.pallas_call(
        functools.partial(_mix_in_kernel, tile=tile),
        grid=(b, nt),
        in_specs=[
            pl.BlockSpec((1, tile, d), lambda i, t: (i, t, 0)),
            pl.BlockSpec((1, SUBLANES, d), lambda i, t: (i, jnp.maximum(t * hb - 1, 0), 0)),
            pl.BlockSpec((1, SUBLANES, d), lambda i, t: (i, jnp.minimum((t + 1) * hb, last_hb), 0)),
            pl.BlockSpec((1, 1, N_MOD * d), lambda i, t: (i, 0, 0)),
            pl.BlockSpec((1, d), const2),
            pl.BlockSpec((d, IN_W), const2),
            pl.BlockSpec((3, CONV_W), const2),
            pl.BlockSpec((1, SGU_W), const2),
            pl.BlockSpec((SGU_HEADS, CHUNK, CHUNK), lambda i, t: (0, 0, 0)),
            pl.BlockSpec((CHUNK, SGU_W), const2),
            pl.BlockSpec((1, d), const2),
            pl.BlockSpec((FOURIER_W, 2 * FOURIER_W), const2),
            pl.BlockSpec((tile, tile), const2),
        ],
        out_specs=[
            pl.BlockSpec((1, tile, CONV_W + SGU_W), lambda i, t: (i, t, 0)),
            pl.BlockSpec((tile, FOURIER_W), lambda i, t: (t, i)),
            pl.BlockSpec((tile, FOURIER_W), lambda i, t: (t, i)),
            pl.BlockSpec((tile, FOURIER_W), lambda i, t: (jnp.minimum(nt - 1 - t, nt // 2 - 1), i)),
            pl.BlockSpec((tile, FOURIER_W), lambda i, t: (jnp.minimum(nt - 1 - t, nt // 2 - 1), i)),
        ],
        out_shape=[
            jax.ShapeDtypeStruct((b, s, CONV_W + SGU_W), _BF16),
            jax.ShapeDtypeStruct((s, b * FOURIER_W), _BF16),
            jax.ShapeDtypeStruct((s, b * FOURIER_W), _BF16),
            jax.ShapeDtypeStruct((s // 2, b * FOURIER_W), _BF16),
            jax.ShapeDtypeStruct((s // 2, b * FOURIER_W), _BF16),
        ],
        compiler_params=_params(("arbitrary", "arbitrary")),
        name="mix_in",
    )(x, x, x, mod_l, gmix, win_bf, convw, gsgu, wsp_bf, bsp_rows, ggrp, dft64, flip)


def _seq_dft_kernel(wc_ref, ws_ref, fc_ref, fs_ref, fcm_ref, fsm_ref, mid_ref, o_ref, *, tm):
    gc = (fc_ref[...].astype(_F32) + fcm_ref[...].astype(_F32)).astype(_BF16)
    gs = (fs_ref[...].astype(_F32) - fsm_ref[...].astype(_F32)).astype(_BF16)
    part = (jnp.dot(wc_ref[...], gc, preferred_element_type=_F32)
            + jnp.dot(ws_ref[...], gs, preferred_element_type=_F32))

    @pl.when(pl.program_id(2) == 0)
    def _():
        k = pl.program_id(0) * tm + lax.broadcasted_iota(jnp.int32, (tm, 1), 0)
        sign = (1 - 2 * (k & 1)).astype(_F32)
        o_ref[...] = part + sign * mid_ref[0:1, :].astype(_F32)

    @pl.when(pl.program_id(2) > 0)
    def _():
        o_ref[...] += part


def _seq_dft(wc, wsn, fc, fs, fcm, fsm):
    s = wc.shape[0]
    n = fc.shape[1]
    half = s // 2
    tm = min(1024, s)
    tn = min(1024, n)
    tk = min(1024, half)
    fblock = pl.BlockSpec((tk, tn), lambda i, j, k: (k, j))
    return pl.pallas_call(
        functools.partial(_seq_dft_kernel, tm=tm),
        grid=(s // tm, n // tn, half // tk),
        in_specs=[
            pl.BlockSpec((tm, tk), lambda i, j, k: (i, k)),
            pl.BlockSpec((tm, tk), lambda i, j, k: (i, k)),
            fblock, fblock, fblock, fblock,
            pl.BlockSpec((SUBLANES, tn), lambda i, j, k: (0, j)),
        ],
        out_specs=pl.BlockSpec((tm, tn), lambda i, j, k: (i, j)),
        out_shape=jax.ShapeDtypeStruct((s, n), _F32),
        compiler_params=_params(("arbitrary", "arbitrary", "arbitrary")),
        name="seq_dft",
    )(wc, wsn, fc, fs, fcm, fsm, fc[half:half + SUBLANES])


def _mix_out_kernel(x_ref, yac_ref, yf_ref, mod_ref, ggrp_ref, wout_ref, gffn_ref, wr_ref,
                    xh_ref, aff_ref, *, seq_scale):
    d = D_MODEL
    gate_m = mod_ref[0, :, 2 * d:3 * d]
    shift_f = mod_ref[0, :, 3 * d:4 * d]
    scale_f = mod_ref[0, :, 4 * d:5 * d]
    yf = (_rms(yf_ref[...] * seq_scale) * ggrp_ref[:, CONV_W:CONV_W + FOURIER_W]).astype(_BF16)
    yac = yac_ref[0]
    ycat = jnp.concatenate([yac[:, :CONV_W], yf, yac[:, CONV_W:]], axis=1)
    mix = jnp.dot(ycat, wout_ref[...], preferred_element_type=_F32)
    x1 = x_ref[0] + gate_m * mix
    h2 = _rms(x1) * (gffn_ref[...] * (1.0 + scale_f)) + shift_f
    xh_ref[0, :, :d] = x1
    xh_ref[0, :, d:] = h2
    h_hi = h2.astype(_BF16)
    h_lo = (h2 - h_hi.astype(_F32)).astype(_BF16)
    by_hi = jnp.dot(h_hi, wr_ref[...], preferred_element_type=_F32)
    logits = (by_hi[:, :N_EXPERTS] + (by_hi[:, N_EXPERTS:]
              + jnp.dot(h_lo, wr_ref[:, :N_EXPERTS], preferred_element_type=_F32)))
    e = jnp.exp(logits - jnp.max(logits, axis=-1, keepdims=True))
    aff_ref[0] = e / jnp.sum(e, axis=-1, keepdims=True)


def _mix_out(x, yac, yf, mod_l, ggrp, wout_bf, gffn, w_router, tile):
    b, s, d = x.shape[0], x.shape[1], D_MODEL
    const2 = lambda i, t: (0, 0)
    return pl.pallas_call(
        functools.partial(_mix_out_kernel, seq_scale=1.0 / math.sqrt(s)),
        grid=(b, s // tile),
        in_specs=[
            pl.BlockSpec((1, tile, d), lambda i, t: (i, t, 0)),
            pl.BlockSpec((1, tile, CONV_W + SGU_W), lambda i, t: (i, t, 0)),
            pl.BlockSpec((tile, FOURIER_W), lambda i, t: (t, i)),
            pl.BlockSpec((1, 1, N_MOD * d), lambda i, t: (i, 0, 0)),
            pl.BlockSpec((1, d), const2),
            pl.BlockSpec((d, d), const2),
            pl.BlockSpec((1, d), const2),
            pl.BlockSpec((d, 2 * N_EXPERTS), const2),
        ],
        out_specs=[
            pl.BlockSpec((1, tile, 2 * d), lambda i, t: (i, t, 0)),
            pl.BlockSpec((1, tile, N_EXPERTS), lambda i, t: (i, t, 0)),
        ],
        out_shape=[
            jax.ShapeDtypeStruct((b, s, 2 * d), _F32),
            jax.ShapeDtypeStruct((b, s, N_EXPERTS), _F32),
        ],
        compiler_params=_params(("arbitrary", "arbitrary")),
        name="mix_out",
    )(x, yac, yf, mod_l, ggrp, wout_bf, gffn, w_router)


def _cumsum_mats(rows):
    qi = lax.broadcasted_iota(jnp.int32, (LANES, LANES), 0)
    qj = lax.broadcasted_iota(jnp.int32, (LANES, LANES), 1)
    ri = lax.broadcasted_iota(jnp.int32, (rows, rows), 0)
    rj = lax.broadcasted_iota(jnp.int32, (rows, rows), 1)
    def ones_where(mask):
        return jnp.where(mask, 1.0, 0.0).astype(_BF16)

    lane_excl = ones_where(qi < qj)
    lane_incl = ones_where(qi <= qj)
    row_excl = ones_where(rj < ri)
    row_incl_t = ones_where(ri <= rj)
    return lane_excl, lane_incl, row_excl, row_incl_t


def _route_kernel(aff_ref, idx_ref, gate_ref, idxc_ref, *, cap, jb):
    a = aff_ref[0]
    rows = a.shape[0]
    lane_excl, lane_incl, row_excl, row_incl_t = _cumsum_mats(rows)

    def count(mask):
        return jnp.sum(jnp.where(mask, 1.0, 0.0)).astype(jnp.int32)

    def refine(thr_bits, shift, nbits):
        passing = jnp.int32(0)
        for m in range(1, 1 << nbits):
            cand_f = lax.bitcast_convert_type(thr_bits | (jnp.int32(m) << shift), _F32)
            passing += (count(a >= cand_f) >= cap).astype(jnp.int32)
        return thr_bits | (passing << shift)

    thr_bits = lax.fori_loop(0, 10, lambda i, t: refine(t, 27 - 3 * i, 3), jnp.int32(0))
    thr = lax.bitcast_convert_type(thr_bits, _F32)
    above = a > thr
    tied = a == thr
    need = cap - count(above)

    tied_b = jnp.where(tied, 1.0, 0.0).astype(_BF16)
    in_row = jnp.dot(tied_b, lane_excl, preferred_element_type=_F32)
    row_tot = jnp.dot(tied_b, jnp.ones((LANES, LANES), _BF16), preferred_element_type=_F32)
    before_row = jnp.dot(row_excl, row_tot.astype(_BF16), preferred_element_type=_F32)
    tie_rank = (in_row + before_row).astype(jnp.int32)
    sel = above | (tied & (tie_rank < need))
    sel_b = jnp.where(sel, 1.0, 0.0).astype(_BF16)

    row_cnt = lax.dot_general(jnp.ones((SUBLANES, LANES), _BF16), sel_b,
                              (((1,), (1,)), ((), ())), preferred_element_type=_F32)
    incl = jnp.dot(row_cnt.astype(_BF16), row_incl_t, preferred_element_type=_F32)[0:1, :]
    excl = incl - row_cnt[0:1, :]
    a_hi = a.astype(_BF16)
    a_mid = (a - a_hi.astype(_F32)).astype(_BF16)
    a_lo = (a - a_hi.astype(_F32) - a_mid.astype(_F32)).astype(_BF16)
    row_id = lax.broadcasted_iota(jnp.int32, (1, rows), 1).astype(_F32)
    lane_id = lax.broadcasted_iota(jnp.int32, (jb, LANES), 1).astype(_F32)
    diag = (lax.broadcasted_iota(jnp.int32, (LANES, LANES), 0)
            == lax.broadcasted_iota(jnp.int32, (LANES, LANES), 1))

    for c in range(cap // jb):
        j = (lax.broadcasted_iota(jnp.int32, (jb, 1), 0) + c * jb).astype(_F32)
        hit = (excl <= j) & (j < incl)
        hit_f = jnp.where(hit, 1.0, 0.0)
        r_of_j = jnp.sum(hit_f * row_id, axis=1, keepdims=True)
        target = j + 1.0 - jnp.sum(hit_f * excl, axis=1, keepdims=True)
        row_sel = jnp.dot(hit_f.astype(_BF16), sel_b, preferred_element_type=_F32)
        within = jnp.dot(row_sel.astype(_BF16), lane_incl, preferred_element_type=_F32)
        l_of_j = jnp.sum(jnp.where(within < target, 1.0, 0.0), axis=1, keepdims=True)
        hit_b = hit_f.astype(_BF16)
        a_row = (jnp.dot(hit_b, a_hi, preferred_element_type=_F32)
                 + jnp.dot(hit_b, a_mid, preferred_element_type=_F32)
                 + jnp.dot(hit_b, a_lo, preferred_element_type=_F32))
        gate = jnp.sum(jnp.where(lane_id == l_of_j, a_row, 0.0), axis=1, keepdims=True)
        token_rep = jnp.broadcast_to(r_of_j * LANES + l_of_j, (jb, LANES))
        idx_ref[0, c * jb:(c + 1) * jb, :] = token_rep.astype(jnp.int32)
        gate_ref[0, c * jb:(c + 1) * jb, :] = jnp.broadcast_to(gate, (jb, LANES))
        for g in range(jb // LANES):
            grp = jnp.where(diag, token_rep[g * LANES:(g + 1) * LANES, :], 0.0)
            out_row = c * (jb // LANES) + g
            idxc_ref[0, out_row:out_row + 1, :] = jnp.sum(grp, axis=0, keepdims=True).astype(jnp.int32)


def _route(aff_t, cap):
    e, rows, _ = aff_t.shape
    jb = min(512, cap)
    return pl.pallas_call(
        functools.partial(_route_kernel, cap=cap, jb=jb),
        grid=(e,),
        in_specs=[pl.BlockSpec((1, rows, LANES), lambda i: (i, 0, 0))],
        out_specs=[
            pl.BlockSpec((1, cap, LANES), lambda i: (i, 0, 0)),
            pl.BlockSpec((1, cap, LANES), lambda i: (i, 0, 0)),
            pl.BlockSpec((1, cap // LANES, LANES), lambda i: (i, 0, 0)),
        ],
        out_shape=[
            jax.ShapeDtypeStruct((e, cap, LANES), jnp.int32),
            jax.ShapeDtypeStruct((e, cap, LANES), _F32),
            jax.ShapeDtypeStruct((e, cap // LANES, LANES), jnp.int32),
        ],
        compiler_params=_params(("arbitrary",)),
        name="route",
    )(aff_t)


def _experts_kernel(idx_hbm, xin_hbm, tok_ref, gate_ref, gf_ref, wg_ref, wu_ref, wd_ref,
                    xh_hbm, idx_p, idx_a, idx_b, idx_n, idx_m, prev_first, buf_a, buf_b, wb_a, wb_b, sems,
                    *, rows, nsteps, seq_shift, nbatch, col_chunk):
    del xin_hbm
    d = D_MODEL
    e = pl.program_id(0)
    j = pl.program_id(1)
    step = e * nsteps + j
    last_step = pl.num_programs(0) * nsteps - 1
    blk_a = 2 * step
    blk_p = jnp.maximum(blk_a - 1, 0)
    last_blk = 2 * last_step + 1
    blk_n = jnp.minimum(blk_a + 2, last_blk)
    blk_m = jnp.minimum(blk_a + 3, last_blk)
    SEM_IDX, SEM_GA, SEM_GB, SEM_SA, SEM_SB, SEM_FIX = range(6)

    idx_copies = [pltpu.make_async_copy(idx_hbm.at[b], ref, sems.at[SEM_IDX])
                  for b, ref in ((blk_p, idx_p), (blk_a, idx_a), (blk_a + 1, idx_b), (blk_n, idx_n),
                                 (blk_m, idx_m))]
    for c in idx_copies:
        c.start()
    for c in idx_copies:
        c.wait()

    res_cols = pl.ds(0, d)
    moe_cols = pl.ds(d, d)

    def row_copy(idx_ref, r, buf, sem, kind):
        tok = idx_ref[0, r]
        if kind == "store":
            return pltpu.make_async_copy(buf.at[pl.ds(r, 1)], xh_hbm.at[pl.ds(tok, 1), res_cols], sems.at[sem])
        if kind == "fetch":
            return pltpu.make_async_copy(xh_hbm.at[pl.ds(tok, 1)], buf.at[pl.ds(r, 1)], sems.at[sem])
        cols = moe_cols if kind == "fetch_moe" else res_cols
        return pltpu.make_async_copy(xh_hbm.at[pl.ds(tok, 1), cols], buf.at[pl.ds(r, 1), cols], sems.at[sem])

    def start_all(idx_ref, buf, sem, kind):
        def body(r, carry):
            row_copy(idx_ref, r, buf, sem, kind).start()
            return carry
        lax.fori_loop(0, rows, body, 0)

    def wait_all(buf, sem, kind):
        all_rows = pl.ds(0, rows)
        if kind == "store":
            pltpu.make_async_copy(buf, xh_hbm.at[all_rows, res_cols], sems.at[sem]).wait()
        elif kind == "fetch":
            pltpu.make_async_copy(xh_hbm.at[all_rows], buf, sems.at[sem]).wait()
        else:
            cols = moe_cols if kind == "fetch_moe" else res_cols
            pltpu.make_async_copy(xh_hbm.at[all_rows, cols], buf.at[:, cols], sems.at[sem]).wait()

    def block(part, buf, sem_g, wb, streams, fetched=False):
        if not fetched:
            wait_all(buf, sem_g, "fetch")
        todo = [functools.partial(row_copy, stream[0], r, *stream[1:])
                for r in range(rows) for stream in streams]
        nch = D_MODEL // col_chunk
        per_call = -(-len(todo) // (3 * nch))

        def start_some():
            for k, make in enumerate(todo[:per_call]):
                make().start(priority=k % 2)
            del todo[:per_call]

        x = buf[:, d:].astype(_BF16)
        hid = []
        for c in range(nch):
            cs = slice(c * col_chunk, (c + 1) * col_chunk)
            hg = jnp.dot(x, wg_ref[0, 0, :, cs].astype(_BF16), preferred_element_type=_F32)
            start_some()
            hu = jnp.dot(x, wu_ref[0, 0, :, cs].astype(_BF16), preferred_element_type=_F32)
            start_some()
            hid.append((hg * jax.nn.sigmoid(hg) * hu).astype(_BF16))
        hid = jnp.concatenate(hid, axis=1)
        rs = slice(part * rows, (part + 1) * rows)
        seq_of_row = tok_ref[rs, :] >> seq_shift
        gate = gate_ref[rs, :]
        upd = []
        for c in range(nch):
            cs = slice(c * col_chunk, (c + 1) * col_chunk)
            y = jnp.dot(hid, wd_ref[0, 0, :, cs].astype(_BF16), preferred_element_type=_F32)
            start_some()
            seq_c = jnp.concatenate([seq_of_row] * (col_chunk // LANES), axis=1)
            gf = jnp.zeros_like(y)
            for b in range(nbatch):
                gf = jnp.where(seq_c == b, gf_ref[b:b + 1, cs], gf)
            upd.append(gf * (y * jnp.concatenate([gate] * (col_chunk // LANES), axis=1)))
        assert not todo
        wb[...] = buf[:, :d] + jnp.concatenate(upd, axis=1)

    fetch_b = (idx_b, buf_b, SEM_GB, "fetch")
    fetch_n = (idx_n, buf_a, SEM_GA, "fetch")
    fetch_n_moe = (idx_n, buf_a, SEM_GA, "fetch_moe")
    store_p = (idx_p, wb_b, SEM_SB, "store")
    store_a = (idx_a, wb_a, SEM_SA, "store")
    store_b = (idx_b, wb_b, SEM_SB, "store")

    @pl.when(step == 0)
    def _():
        start_all(idx_a, buf_a, SEM_GA, "fetch")
        prev_first[0] = 0

    @pl.when(step > 0)
    def _():
        wait_all(wb_a, SEM_SA, "store")

    def overlaps(first, last, other_first, other_last):
        return jnp.logical_not((other_last < first) | (last < other_first))

    serial = (step == 0) | overlaps(idx_a[0, 0], idx_b[0, rows - 1], prev_first[0], idx_p[0, rows - 1])
    next_serial = overlaps(idx_n[0, 0], idx_m[0, rows - 1], idx_a[0, 0], idx_b[0, rows - 1])
    prev_first[0] = idx_a[0, 0]

    @pl.when(serial)
    def _():
        @pl.when(step == 0)
        def _():
            wait_all(buf_a, SEM_GA, "fetch")

        @pl.when(step > 0)
        def _():
            start_all(*store_p)
            wait_all(wb_b, SEM_SB, "store")
            wait_all(buf_a, SEM_GA, "fetch_moe")
            start_all(idx_a, buf_a, SEM_FIX, "fetch_res")
            wait_all(buf_a, SEM_FIX, "fetch_res")
        block(0, buf_a, SEM_GA, wb_a, [fetch_b], fetched=True)

    @pl.when(jnp.logical_not(serial))
    def _():
        block(0, buf_a, SEM_GA, wb_a, [store_p, fetch_b])
        wait_all(wb_b, SEM_SB, "store")

    @pl.when(next_serial)
    def _():
        block(1, buf_b, SEM_GB, wb_b, [store_a, fetch_n_moe])

    @pl.when(jnp.logical_not(next_serial))
    def _():
        block(1, buf_b, SEM_GB, wb_b, [store_a, fetch_n])

    @pl.when(step == last_step)
    def _():
        wait_all(buf_a, SEM_GA, "fetch_moe")
        start_all(*store_b)
        wait_all(wb_a, SEM_SA, "store")
        wait_all(wb_b, SEM_SB, "store")


def _experts(idx_blocks, xh, tok_rep, gate_rep, gate_f, w_gate, w_up, w_down, layer, seq, rows):
    n, d = xh.shape[0], D_MODEL
    e = w_gate.shape[1]
    cap = tok_rep.shape[0] // e
    nsteps = cap // (2 * rows)
    nbatch = gate_f.shape[0]
    seq_shift = seq.bit_length() - 1
    assert 1 << seq_shift == seq and nsteps * 2 * rows == cap
    wspec = pl.BlockSpec((1, 1, d, d), lambda i, j: (layer, i, 0, 0))
    any_spec = pl.BlockSpec(memory_space=pl.ANY)
    return pl.pallas_call(
        functools.partial(_experts_kernel, rows=rows, nsteps=nsteps, seq_shift=seq_shift, nbatch=nbatch,
                          col_chunk=EXPERT_COL_CHUNK),
        grid=(e, nsteps),
        in_specs=[
            any_spec, any_spec,
            pl.BlockSpec((2 * rows, LANES), lambda i, j: (i * nsteps + j, 0)),
            pl.BlockSpec((2 * rows, LANES), lambda i, j: (i * nsteps + j, 0)),
            pl.BlockSpec((nbatch, d), lambda i, j: (0, 0)),
            wspec, wspec, wspec,
        ],
        out_specs=any_spec,
        out_shape=jax.ShapeDtypeStruct((n, 2 * d), _F32),
        scratch_shapes=[
            pltpu.SMEM((1, rows), jnp.int32),
            pltpu.SMEM((1, rows), jnp.int32),
            pltpu.SMEM((1, rows), jnp.int32),
            pltpu.SMEM((1, rows), jnp.int32),
            pltpu.SMEM((1, rows), jnp.int32),
            pltpu.SMEM((1,), jnp.int32),
            pltpu.VMEM((rows, 2 * d), _F32),
            pltpu.VMEM((rows, 2 * d), _F32),
            pltpu.VMEM((rows, d), _F32),
            pltpu.VMEM((rows, d), _F32),
            pltpu.SemaphoreType.DMA((6,)),
        ],
        input_output_aliases={1: 0},
        compiler_params=_params(("arbitrary", "arbitrary")),
        name="experts",
    )(idx_blocks, xh, tok_rep, gate_rep, gate_f, w_gate, w_up, w_down)


def _final_kernel(x_ref, g_ref, o_ref):
    o_ref[0] = _rms(x_ref[0]) * g_ref[...]


def _final_norm(x, g, tile):
    b, s, d = x.shape[0], x.shape[1], D_MODEL
    return pl.pallas_call(
        _final_kernel,
        grid=(b, s // tile),
        in_specs=[pl.BlockSpec((1, tile, d), lambda i, t: (i, t, 0)),
                  pl.BlockSpec((1, d), lambda i, t: (0, 0))],
        out_specs=pl.BlockSpec((1, tile, d), lambda i, t: (i, t, 0)),
        out_shape=jax.ShapeDtypeStruct((b, s, d), _F32),
        compiler_params=_params(("arbitrary", "arbitrary")),
        name="final_norm",
    )(x, g)


def _channel_dft_table():
    k = jnp.arange(HEAD_DIM, dtype=jnp.int32)
    ang = ((k[:, None] * k[None, :]) % HEAD_DIM).astype(_F32) * (2.0 * math.pi / HEAD_DIM)
    groups = FOURIER_W // HEAD_DIM
    eye = jnp.eye(groups, dtype=_F32)
    c = jnp.kron(eye, jnp.cos(ang)) / math.sqrt(HEAD_DIM)
    s = jnp.kron(eye, jnp.sin(ang)) / math.sqrt(HEAD_DIM)
    return jnp.concatenate([c, s], axis=1).astype(_BF16)


def _split_hi_lo(w):
    hi = w.astype(_BF16)
    lo = (w - hi.astype(_F32)).astype(_BF16)
    return jnp.concatenate([hi, lo], axis=-1)


def _seq_dft_tables(s):
    radix = 64
    assert s % radix == 0
    t = jnp.arange(s, dtype=jnp.int32)[None, :]
    k1 = jnp.arange(s // radix, dtype=jnp.int32)[:, None]
    k0 = jnp.arange(radix, dtype=jnp.int32)[:, None]
    ang_a = ((k1 * radix * t) % s).astype(_F32) * (2.0 * math.pi / s)
    ang_b = ((k0 * t) % s).astype(_F32) * (2.0 * math.pi / s)
    ca, sa = jnp.cos(ang_a)[:, None, :], jnp.sin(ang_a)[:, None, :]
    cb, sb = jnp.cos(ang_b)[None, :, :], jnp.sin(ang_b)[None, :, :]
    cos = (ca * cb - sa * sb).astype(_BF16).reshape(s, s)
    nsin = (-(sa * cb + ca * sb)).astype(_BF16).reshape(s, s)
    return cos, nsin


def _encoder(x, mod, p, dft64, seq_tables, tile, erows=None):
    b, s, d = x.shape
    n = b * s
    cap = CAPACITY_FACTOR * n // N_EXPERTS
    if erows is None:
        erows = min(EXPERT_MAX_ROWS, cap // 4)
    assert cap % (2 * erows) == 0 and cap // erows >= 2
    wc, wsn = seq_tables
    depth = p["w_in"].shape[0]
    for l in range(depth):
        mod_l = mod[l][:, None, :]
        yac, fc, fs, fcm, fsm = _mix_in(x, mod_l, p["g_mix"][l][None], p["w_in_bf"][l], p["conv_w"][l],
                                        p["g_sgu"][l][None], p["w_sp_bf"][l], p["b_sp_rows"][l],
                                        p["g_grp"][l][None], dft64, tile)
        yf = _seq_dft(wc, wsn, fc, fs, fcm, fsm)
        xh, aff = _mix_out(x, yac, yf, mod_l, p["g_grp"][l][None], p["w_out_bf"][l],
                           p["g_ffn"][l][None], p["w_router_hl"][l], tile)
        aff_t = aff.reshape(n, N_EXPERTS).T.reshape(N_EXPERTS, n // LANES, LANES)
        tok_rep, gate_rep, idx = _route(aff_t, cap)
        idx_blocks = idx.reshape(N_EXPERTS * cap // erows, 1, erows)
        gate_f = mod_l[:, 0, 5 * d:6 * d]
        x = _experts(idx_blocks, xh.reshape(n, 2 * d),
                     tok_rep.reshape(N_EXPERTS * cap, LANES), gate_rep.reshape(N_EXPERTS * cap, LANES),
                     gate_f, p["w_gate"], p["w_up"], p["w_down"], l, s, erows).reshape(b, s, 2 * d)
    return _final_norm(x, p["g_final"][None], tile)


def kernel(x_prompt, x_sample, c_prompt, c_sample, w_ada, b_ada, g_mix, w_in, conv_w, g_sgu, w_spatial,
           b_spatial, g_grp, w_out, g_ffn, w_router, w_gate, w_up, w_down, g_final):
    nb = x_prompt.shape[0]
    mod = _ada(jnp.concatenate([c_prompt, c_sample], axis=0), w_ada, b_ada)
    p = dict(
        g_mix=g_mix, conv_w=conv_w, g_sgu=g_sgu, g_grp=g_grp, g_ffn=g_ffn, w_router_hl=_split_hi_lo(w_router),
        w_gate=w_gate, w_up=w_up, w_down=w_down, g_final=g_final, w_in=w_in,
        w_in_bf=w_in.astype(_BF16), w_out_bf=w_out.astype(_BF16), w_sp_bf=w_spatial.astype(_BF16),
        b_sp_rows=jnp.repeat(jnp.swapaxes(b_spatial, 1, 2), HEAD_DIM, axis=2),
    )
    dft64 = _channel_dft_table()
    tile = min(SEQ_TILE, x_sample.shape[1])
    y_prompt = _encoder(x_prompt, mod[:, :nb], p, dft64, _seq_dft_tables(x_prompt.shape[1]), tile)
    y_sample = _encoder(x_sample, mod[:, nb:], p, dft64, _seq_dft_tables(x_sample.shape[1]), tile)
    return (y_prompt, y_sample)
```

```python
import functools
import math

import jax
import jax.numpy as jnp
from jax import lax
from jax.experimental import pallas as pl
from jax.experimental.pallas import tpu as pltpu

D_MODEL = 1024
HEAD_DIM = 64
CONV_W = 384
FOURIER_W = 256
SGU_W = 384
SGU_HEADS = SGU_W // HEAD_DIM
CHUNK = 128
IN_W = 3 * CONV_W + FOURIER_W + 2 * SGU_W
N_EXPERTS = 16
CAPACITY_FACTOR = 2
N_MOD = 6
EPS = 1e-6

LANES = 128
SUBLANES = 8
VMEM_LIMIT_BYTES = 52 * 1024 * 1024
MXU_COLS = 256
EXPERT_COL_CHUNK = MXU_COLS
EXPERT_MAX_ROWS = 512
SEQ_TILE = 1024

_HI = lax.Precision.HIGHEST
_BF16 = jnp.bfloat16
_F32 = jnp.float32


def _rms(x):
    return x * lax.rsqrt(jnp.mean(x * x, axis=-1, keepdims=True) + EPS)


def _params(sem, vmem=VMEM_LIMIT_BYTES):
    return pltpu.CompilerParams(dimension_semantics=sem, vmem_limit_bytes=vmem)


def _ada_kernel(c_ref, w_ref, b_ref, o_ref):
    c = c_ref[...]
    a = c * jax.nn.sigmoid(c)
    o_ref[0] = jnp.dot(a, w_ref[0], precision=_HI, preferred_element_type=_F32) + b_ref[0]


def _ada(c_all, w_ada, b_ada):
    depth, d, m = w_ada.shape
    nb = c_all.shape[0]
    tn = 1536
    return pl.pallas_call(
        _ada_kernel,
        grid=(depth, m // tn),
        in_specs=[
            pl.BlockSpec((nb, d), lambda l, j: (0, 0)),
            pl.BlockSpec((1, d, tn), lambda l, j: (l, 0, j)),
            pl.BlockSpec((1, 1, tn), lambda l, j: (l, 0, j)),
        ],
        out_specs=pl.BlockSpec((1, nb, tn), lambda l, j: (l, 0, j)),
        out_shape=jax.ShapeDtypeStruct((depth, nb, m), _F32),
        compiler_params=_params(("arbitrary", "arbitrary")),
        name="ada",
    )(c_all, w_ada, b_ada.reshape(depth, 1, m))


def _mix_in_kernel(x_ref, xp_ref, xn_ref, mod_ref, gmix_ref, win_ref, convw_ref, gsgu_ref,
                   wsp_ref, bsp_ref, ggrp_ref, dft_ref, flip_ref, yac_ref, fc_ref, fs_ref, fcm_ref, fsm_ref,
                   *, tile):
    t = pl.program_id(1)
    nt = pl.num_programs(1)
    shift = mod_ref[0, :, 0:D_MODEL]
    scale = mod_ref[0, :, D_MODEL:2 * D_MODEL]
    gain = gmix_ref[...] * (1.0 + scale)

    def modulated(x):
        return (_rms(x) * gain + shift).astype(_BF16)

    h = modulated(x_ref[0])
    z = jnp.dot(h, win_ref[...], preferred_element_type=_F32)

    def halo(hh):
        zh = jnp.dot(hh, win_ref[:, CONV_W:3 * CONV_W], preferred_element_type=_F32)
        return zh[:, :CONV_W] * zh[:, CONV_W:]

    h_after = modulated(xn_ref[0])
    a_before = jnp.where(t > 0, halo(modulated(xp_ref[0]))[SUBLANES - 1:SUBLANES, :], 0.0)
    a_after = jnp.where(t < nt - 1, halo(h_after)[0:1, :], 0.0)

    zb = z[:, 0:CONV_W]
    a = z[:, CONV_W:2 * CONV_W] * z[:, 2 * CONV_W:3 * CONV_W]
    row = lax.broadcasted_iota(jnp.int32, (tile, CONV_W), 0)
    a_prev = jnp.where(row == 0, a_before, pltpu.roll(a, 1, axis=0))
    a_next = jnp.where(row == tile - 1, a_after, pltpu.roll(a, tile - 1, axis=0))
    cw = convw_ref[...]
    conv = zb * (cw[0:1, :] * a_prev + cw[1:2, :] * a + cw[2:3, :] * a_next)
    ya = (_rms(conv) * ggrp_ref[:, 0:CONV_W]).astype(_BF16)

    zf = z[:, 3 * CONV_W:3 * CONV_W + FOURIER_W].astype(_BF16)
    f2 = jnp.dot(zf, dft_ref[...], preferred_element_type=_F32)
    f2 = f2.astype(_BF16)
    fc_ref[...] = f2[:, :FOURIER_W]
    fs_ref[...] = f2[:, FOURIER_W:]

    @pl.when(t >= nt // 2)
    def _():
        zf_after = jnp.dot(h_after, win_ref[:, 3 * CONV_W:3 * CONV_W + FOURIER_W],
                           preferred_element_type=_F32).astype(_BF16)
        first = jnp.dot(zf_after, dft_ref[...], preferred_element_type=_F32)[0:1, :]
        first = jnp.where(t < nt - 1, first, 0.0)
        nb = tile // CHUNK
        row_in_group = lax.broadcasted_iota(jnp.int32, (CHUNK, 2 * FOURIER_W), 0)
        for a in range(nb):
            src = f2[(nb - 1 - a) * CHUNK:(nb - a) * CHUNK, :]
            m = jnp.dot(flip_ref[...], src, preferred_element_type=_F32)
            head = first if a == 0 else f2[(nb - a) * CHUNK:(nb - a) * CHUNK + 1, :].astype(_F32)
            m = jnp.where(row_in_group == 0, head, m).astype(_BF16)
            fcm_ref[a * CHUNK:(a + 1) * CHUNK, :] = m[:, :FOURIER_W]
            fsm_ref[a * CHUNK:(a + 1) * CHUNK, :] = m[:, FOURIER_W:]

    off = 3 * CONV_W + FOURIER_W
    u = jax.nn.gelu(z[:, off:off + SGU_W])
    v = (_rms(jax.nn.gelu(z[:, off + SGU_W:off + 2 * SGU_W])) * gsgu_ref[...]).astype(_BF16)
    lane = lax.broadcasted_iota(jnp.int32, (CHUNK, LANES), 1)
    first_head = lane < HEAD_DIM
    chunks = []
    for c in range(tile // CHUNK):
        pairs = []
        for k in range(SGU_W // LANES):
            vk = v[c * CHUNK:(c + 1) * CHUNK, k * LANES:(k + 1) * LANES]
            r0 = jnp.dot(wsp_ref[2 * k], vk, preferred_element_type=_F32)
            r1 = jnp.dot(wsp_ref[2 * k + 1], vk, preferred_element_type=_F32)
            pairs.append(jnp.where(first_head, r0, r1))
        chunks.append(jnp.concatenate(pairs, axis=1) + bsp_ref[...])
    sv = jnp.concatenate(chunks, axis=0)
    yc = (_rms(u * sv) * ggrp_ref[:, CONV_W + FOURIER_W:]).astype(_BF16)
    yac_ref[0] = jnp.concatenate([ya, yc], axis=1)


def _mix_in(x, mod_l, gmix, win_bf, convw, gsgu, wsp_bf, bsp_rows, ggrp, dft64, tile):
    b, s, d = x.shape[0], x.shape[1], D_MODEL
    nt = s // tile
    assert nt % 2 == 0
    pos = jnp.arange(CHUNK, dtype=jnp.int32)
    flip = (pos[:, None] + pos[None, :] == CHUNK).astype(_BF16)
    hb = tile // SUBLANES
    last_hb = s // SUBLANES - 1
    const2 = lambda i, t: (0, 0)
    return pl.pallas_call(
        functools.partial(_mix_in_kernel, tile=tile),
        grid=(b, nt),
        in_specs=[
            pl.BlockSpec((1, tile, d), lambda i, t: (i, t, 0)),
            pl.BlockSpec((1, SUBLANES, d), lambda i, t: (i, jnp.maximum(t * hb - 1, 0), 0)),
            pl.BlockSpec((1, SUBLANES, d), lambda i, t: (i, jnp.minimum((t + 1) * hb, last_hb), 0)),
            pl.BlockSpec((1, 1, N_MOD * d), lambda i, t: (i, 0, 0)),
            pl.BlockSpec((1, d), const2),
            pl.BlockSpec((d, IN_W), const2),
            pl.BlockSpec((3, CONV_W), const2),
            pl.BlockSpec((1, SGU_W), const2),
            pl.BlockSpec((SGU_HEADS, CHUNK, CHUNK), lambda i, t: (0, 0, 0)),
            pl.BlockSpec((CHUNK, SGU_W), const2),
            pl.BlockSpec((1, d), const2),
            pl.BlockSpec((FOURIER_W, 2 * FOURIER_W), const2),
            pl.BlockSpec((CHUNK, CHUNK), const2),
        ],
        out_specs=[
            pl.BlockSpec((1, tile, CONV_W + SGU_W), lambda i, t: (i, t, 0)),
            pl.BlockSpec((tile, FOURIER_W), lambda i, t: (t, i)),
            pl.BlockSpec((tile, FOURIER_W), lambda i, t: (t, i)),
            pl.BlockSpec((tile, FOURIER_W), lambda i, t: (jnp.minimum(nt - 1 - t, nt // 2 - 1), i)),
            pl.BlockSpec((tile, FOURIER_W), lambda i, t: (jnp.minimum(nt - 1 - t, nt // 2 - 1), i)),
        ],
        out_shape=[
            jax.ShapeDtypeStruct((b, s, CONV_W + SGU_W), _BF16),
            jax.ShapeDtypeStruct((s, b * FOURIER_W), _BF16),
            jax.ShapeDtypeStruct((s, b * FOURIER_W), _BF16),
            jax.ShapeDtypeStruct((s // 2, b * FOURIER_W), _BF16),
            jax.ShapeDtypeStruct((s // 2, b * FOURIER_W), _BF16),
        ],
        compiler_params=_params(("arbitrary", "arbitrary")),
        name="mix_in",
    )(x, x, x, mod_l, gmix, win_bf, convw, gsgu, wsp_bf, bsp_rows, ggrp, dft64, flip)


def _seq_dft_kernel(wc_ref, ws_ref, fc_ref, fs_ref, fcm_ref, fsm_ref, mid_ref, o_ref, *, tm):
    gc = (fc_ref[...].astype(_F32) + fcm_ref[...].astype(_F32)).astype(_BF16)
    gs = (fs_ref[...].astype(_F32) - fsm_ref[...].astype(_F32)).astype(_BF16)
    part = (jnp.dot(wc_ref[...], gc, preferred_element_type=_F32)
            + jnp.dot(ws_ref[...], gs, preferred_element_type=_F32))

    @pl.when(pl.program_id(2) == 0)
    def _():
        k = pl.program_id(0) * tm + lax.broadcasted_iota(jnp.int32, (tm, 1), 0)
        sign = (1 - 2 * (k & 1)).astype(_F32)
        o_ref[...] = part + sign * mid_ref[0:1, :].astype(_F32)

    @pl.when(pl.program_id(2) > 0)
    def _():
        o_ref[...] += part


def _seq_dft(wc, wsn, fc, fs, fcm, fsm):
    s = wc.shape[0]
    n = fc.shape[1]
    half = s // 2
    tm = min(1024, s)
    tn = min(1024, n)
    tk = min(1024, half)
    fblock = pl.BlockSpec((tk, tn), lambda i, j, k: (k, j))
    return pl.pallas_call(
        functools.partial(_seq_dft_kernel, tm=tm),
        grid=(s // tm, n // tn, half // tk),
        in_specs=[
            pl.BlockSpec((tm, tk), lambda i, j, k: (i, k)),
            pl.BlockSpec((tm, tk), lambda i, j, k: (i, k)),
            fblock, fblock, fblock, fblock,
            pl.BlockSpec((SUBLANES, tn), lambda i, j, k: (0, j)),
        ],
        out_specs=pl.BlockSpec((tm, tn), lambda i, j, k: (i, j)),
        out_shape=jax.ShapeDtypeStruct((s, n), _F32),
        compiler_params=_params(("arbitrary", "arbitrary", "arbitrary")),
        name="seq_dft",
    )(wc, wsn, fc, fs, fcm, fsm, fc[half:half + SUBLANES])


def _mix_out_kernel(x_ref, yac_ref, yf_ref, mod_ref, ggrp_ref, wout_ref, gffn_ref, wr_ref,
                    xh_ref, aff_ref, *, seq_scale):
    d = D_MODEL
    gate_m = mod_ref[0, :, 2 * d:3 * d]
    shift_f = mod_ref[0, :, 3 * d:4 * d]
    scale_f = mod_ref[0, :, 4 * d:5 * d]
    yf = (_rms(yf_ref[...] * seq_scale) * ggrp_ref[:, CONV_W:CONV_W + FOURIER_W]).astype(_BF16)
    yac = yac_ref[0]
    ycat = jnp.concatenate([yac[:, :CONV_W], yf, yac[:, CONV_W:]], axis=1)
    mix = jnp.dot(ycat, wout_ref[...], preferred_element_type=_F32)
    x1 = x_ref[0] + gate_m * mix
    h2 = _rms(x1) * (gffn_ref[...] * (1.0 + scale_f)) + shift_f
    xh_ref[0, :, :d] = x1
    xh_ref[0, :, d:] = h2
    h_hi = h2.astype(_BF16)
    h_lo = (h2 - h_hi.astype(_F32)).astype(_BF16)
    by_hi = jnp.dot(h_hi, wr_ref[...], preferred_element_type=_F32)
    logits = (by_hi[:, :N_EXPERTS] + (by_hi[:, N_EXPERTS:]
              + jnp.dot(h_lo, wr_ref[:, :N_EXPERTS], preferred_element_type=_F32)))
    e = jnp.exp(logits - jnp.max(logits, axis=-1, keepdims=True))
    aff_ref[0] = e / jnp.sum(e, axis=-1, keepdims=True)


def _mix_out(x, yac, yf, mod_l, ggrp, wout_bf, gffn, w_router, tile):
    b, s, d = x.shape[0], x.shape[1], D_MODEL
    const2 = lambda i, t: (0, 0)
    return pl.pallas_call(
        functools.partial(_mix_out_kernel, seq_scale=1.0 / math.sqrt(s)),
        grid=(b, s // tile),
        in_specs=[
            pl.BlockSpec((1, tile, d), lambda i, t: (i, t, 0)),
            pl.BlockSpec((1, tile, CONV_W + SGU_W), lambda i, t: (i, t, 0)),
            pl.BlockSpec((tile, FOURIER_W), lambda i, t: (t, i)),
            pl.BlockSpec((1, 1, N_MOD * d), lambda i, t: (i, 0, 0)),
            pl.BlockSpec((1, d), const2),
            pl.BlockSpec((d, d), const2),
            pl.BlockSpec((1, d), const2),
            pl.BlockSpec((d, 2 * N_EXPERTS), const2),
        ],
        out_specs=[
            pl.BlockSpec((1, tile, 2 * d), lambda i, t: (i, t, 0)),
            pl.BlockSpec((1, tile, N_EXPERTS), lambda i, t: (i, t, 0)),
        ],
        out_shape=[
            jax.ShapeDtypeStruct((b, s, 2 * d), _F32),
            jax.ShapeDtypeStruct((b, s, N_EXPERTS), _F32),
        ],
        compiler_params=_params(("arbitrary", "arbitrary")),
        name="mix_out",
    )(x, yac, yf, mod_l, ggrp, wout_bf, gffn, w_router)


def _cumsum_mats(rows):
    qi = lax.broadcasted_iota(jnp.int32, (LANES, LANES), 0)
    qj = lax.broadcasted_iota(jnp.int32, (LANES, LANES), 1)
    ri = lax.broadcasted_iota(jnp.int32, (rows, rows), 0)
    rj = lax.broadcasted_iota(jnp.int32, (rows, rows), 1)
    def ones_where(mask):
        return jnp.where(mask, 1.0, 0.0).astype(_BF16)

    lane_excl = ones_where(qi < qj)
    lane_incl = ones_where(qi <= qj)
    row_excl = ones_where(rj < ri)
    row_incl_t = ones_where(ri <= rj)
    return lane_excl, lane_incl, row_excl, row_incl_t


def _route_kernel(aff_ref, idx_ref, gate_ref, idxc_ref, *, cap, jb):
    a = aff_ref[0]
    rows = a.shape[0]
    lane_excl, lane_incl, row_excl, row_incl_t = _cumsum_mats(rows)

    def count(mask):
        return jnp.sum(jnp.where(mask, 1.0, 0.0)).astype(jnp.int32)

    def refine(thr_bits, shift, nbits):
        passing = jnp.int32(0)
        for m in range(1, 1 << nbits):
            cand_f = lax.bitcast_convert_type(thr_bits | (jnp.int32(m) << shift), _F32)
            passing += (count(a >= cand_f) >= cap).astype(jnp.int32)
        return thr_bits | (passing << shift)

    thr_bits = lax.fori_loop(0, 10, lambda i, t: refine(t, 27 - 3 * i, 3), jnp.int32(0))
    thr = lax.bitcast_convert_type(thr_bits, _F32)
    above = a > thr
    tied = a == thr
    need = cap - count(above)

    tied_b = jnp.where(tied, 1.0, 0.0).astype(_BF16)
    in_row = jnp.dot(tied_b, lane_excl, preferred_element_type=_F32)
    row_tot = jnp.dot(tied_b, jnp.ones((LANES, LANES), _BF16), preferred_element_type=_F32)
    before_row = jnp.dot(row_excl, row_tot.astype(_BF16), preferred_element_type=_F32)
    tie_rank = (in_row + before_row).astype(jnp.int32)
    sel = above | (tied & (tie_rank < need))
    sel_b = jnp.where(sel, 1.0, 0.0).astype(_BF16)

    row_cnt = lax.dot_general(jnp.ones((SUBLANES, LANES), _BF16), sel_b,
                              (((1,), (1,)), ((), ())), preferred_element_type=_F32)
    incl = jnp.dot(row_cnt.astype(_BF16), row_incl_t, preferred_element_type=_F32)[0:1, :]
    excl = incl - row_cnt[0:1, :]
    a_hi = a.astype(_BF16)
    a_mid = (a - a_hi.astype(_F32)).astype(_BF16)
    a_lo = (a - a_hi.astype(_F32) - a_mid.astype(_F32)).astype(_BF16)
    row_id = lax.broadcasted_iota(jnp.int32, (1, rows), 1).astype(_F32)
    lane_id = lax.broadcasted_iota(jnp.int32, (jb, LANES), 1).astype(_F32)
    diag = (lax.broadcasted_iota(jnp.int32, (LANES, LANES), 0)
            == lax.broadcasted_iota(jnp.int32, (LANES, LANES), 1))

    for c in range(cap // jb):
        j = (lax.broadcasted_iota(jnp.int32, (jb, 1), 0) + c * jb).astype(_F32)
        hit = (excl <= j) & (j < incl)
        hit_f = jnp.where(hit, 1.0, 0.0)
        r_of_j = jnp.sum(hit_f * row_id, axis=1, keepdims=True)
        target = j + 1.0 - jnp.sum(hit_f * excl, axis=1, keepdims=True)
        row_sel = jnp.dot(hit_f.astype(_BF16), sel_b, preferred_element_type=_F32)
        within = jnp.dot(row_sel.astype(_BF16), lane_incl, preferred_element_type=_F32)
        l_of_j = jnp.sum(jnp.where(within < target, 1.0, 0.0), axis=1, keepdims=True)
        hit_b = hit_f.astype(_BF16)
        a_row = (jnp.dot(hit_b, a_hi, preferred_element_type=_F32)
                 + jnp.dot(hit_b, a_mid, preferred_element_type=_F32)
                 + jnp.dot(hit_b, a_lo, preferred_element_type=_F32))
        gate = jnp.sum(jnp.where(lane_id == l_of_j, a_row, 0.0), axis=1, keepdims=True)
        token_rep = jnp.broadcast_to(r_of_j * LANES + l_of_j, (jb, LANES))
        idx_ref[0, c * jb:(c + 1) * jb, :] = token_rep.astype(jnp.int32)
        gate_ref[0, c * jb:(c + 1) * jb, :] = jnp.broadcast_to(gate, (jb, LANES))
        for g in range(jb // LANES):
            grp = jnp.where(diag, token_rep[g * LANES:(g + 1) * LANES, :], 0.0)
            out_row = c * (jb // LANES) + g
            idxc_ref[0, out_row:out_row + 1, :] = jnp.sum(grp, axis=0, keepdims=True).astype(jnp.int32)


def _route(aff_t, cap):
    e, rows, _ = aff_t.shape
    jb = min(512, cap)
    return pl.pallas_call(
        functools.partial(_route_kernel, cap=cap, jb=jb),
        grid=(e,),
        in_specs=[pl.BlockSpec((1, rows, LANES), lambda i: (i, 0, 0))],
        out_specs=[
            pl.BlockSpec((1, cap, LANES), lambda i: (i, 0, 0)),
            pl.BlockSpec((1, cap, LANES), lambda i: (i, 0, 0)),
            pl.BlockSpec((1, cap // LANES, LANES), lambda i: (i, 0, 0)),
        ],
        out_shape=[
            jax.ShapeDtypeStruct((e, cap, LANES), jnp.int32),
            jax.ShapeDtypeStruct((e, cap, LANES), _F32),
            jax.ShapeDtypeStruct((e, cap // LANES, LANES), jnp.int32),
        ],
        compiler_params=_params(("arbitrary",)),
        name="route",
    )(aff_t)


def _experts_kernel(idx_hbm, xin_hbm, tok_ref, gate_ref, gf_ref, wg_ref, wu_ref, wd_ref,
                    xh_hbm, idx_p, idx_a, idx_b, idx_n, idx_m, prev_first, buf_a, buf_b, wb_a, wb_b, sems,
                    *, rows, nsteps, seq_shift, nbatch, col_chunk):
    del xin_hbm
    d = D_MODEL
    e = pl.program_id(0)
    j = pl.program_id(1)
    step = e * nsteps + j
    last_step = pl.num_programs(0) * nsteps - 1
    blk_a = 2 * step
    blk_p = jnp.maximum(blk_a - 1, 0)
    last_blk = 2 * last_step + 1
    blk_n = jnp.minimum(blk_a + 2, last_blk)
    blk_m = jnp.minimum(blk_a + 3, last_blk)
    SEM_IDX, SEM_GA, SEM_GB, SEM_SA, SEM_SB, SEM_FIX = range(6)

    idx_copies = [pltpu.make_async_copy(idx_hbm.at[b], ref, sems.at[SEM_IDX])
                  for b, ref in ((blk_p, idx_p), (blk_a, idx_a), (blk_a + 1, idx_b), (blk_n, idx_n),
                                 (blk_m, idx_m))]
    for c in idx_copies:
        c.start()
    for c in idx_copies:
        c.wait()

    res_cols = pl.ds(0, d)
    moe_cols = pl.ds(d, d)

    def row_copy(idx_ref, r, buf, sem, kind):
        tok = idx_ref[0, r]
        if kind == "store":
            return pltpu.make_async_copy(buf.at[pl.ds(r, 1)], xh_hbm.at[pl.ds(tok, 1), res_cols], sems.at[sem])
        if kind == "fetch":
            return pltpu.make_async_copy(xh_hbm.at[pl.ds(tok, 1)], buf.at[pl.ds(r, 1)], sems.at[sem])
        cols = moe_cols if kind == "fetch_moe" else res_cols
        return pltpu.make_async_copy(xh_hbm.at[pl.ds(tok, 1), cols], buf.at[pl.ds(r, 1), cols], sems.at[sem])

    def start_all(idx_ref, buf, sem, kind):
        def body(r, carry):
            row_copy(idx_ref, r, buf, sem, kind).start()
            return carry
        lax.fori_loop(0, rows, body, 0)

    def wait_all(buf, sem, kind):
        all_rows = pl.ds(0, rows)
        if kind == "store":
            pltpu.make_async_copy(buf, xh_hbm.at[all_rows, res_cols], sems.at[sem]).wait()
        elif kind == "fetch":
            pltpu.make_async_copy(xh_hbm.at[all_rows], buf, sems.at[sem]).wait()
        else:
            cols = moe_cols if kind == "fetch_moe" else res_cols
            pltpu.make_async_copy(xh_hbm.at[all_rows, cols], buf.at[:, cols], sems.at[sem]).wait()

    def block(part, buf, sem_g, wb, streams, fetched=False):
        if not fetched:
            wait_all(buf, sem_g, "fetch")
        todo = [functools.partial(row_copy, stream[0], r, *stream[1:])
                for r in range(rows) for stream in streams]
        nch = D_MODEL // col_chunk
        per_call = -(-len(todo) // (3 * nch))

        def start_some():
            for k, make in enumerate(todo[:per_call]):
                make().start(priority=k % 2)
            del todo[:per_call]

        x = buf[:, d:].astype(_BF16)
        hid = []
        for c in range(nch):
            cs = slice(c * col_chunk, (c + 1) * col_chunk)
            hg = jnp.dot(x, wg_ref[0, 0, :, cs].astype(_BF16), preferred_element_type=_F32)
            start_some()
            hu = jnp.dot(x, wu_ref[0, 0, :, cs].astype(_BF16), preferred_element_type=_F32)
            start_some()
            hid.append((hg * jax.nn.sigmoid(hg) * hu).astype(_BF16))
        hid = jnp.concatenate(hid, axis=1)
        rs = slice(part * rows, (part + 1) * rows)
        seq_of_row = tok_ref[rs, :] >> seq_shift
        gate = gate_ref[rs, :]
        upd = []
        for c in range(nch):
            cs = slice(c * col_chunk, (c + 1) * col_chunk)
            y = jnp.dot(hid, wd_ref[0, 0, :, cs].astype(_BF16), preferred_element_type=_F32)
            start_some()
            seq_c = jnp.concatenate([seq_of_row] * (col_chunk // LANES), axis=1)
            gf = jnp.zeros_like(y)
            for b in range(nbatch):
                gf = jnp.where(seq_c == b, gf_ref[b:b + 1, cs], gf)
            upd.append(gf * (y * jnp.concatenate([gate] * (col_chunk // LANES), axis=1)))
        assert not todo
        wb[...] = buf[:, :d] + jnp.concatenate(upd, axis=1)

    fetch_b = (idx_b, buf_b, SEM_GB, "fetch")
    fetch_n = (idx_n, buf_a, SEM_GA, "fetch")
    fetch_n_moe = (idx_n, buf_a, SEM_GA, "fetch_moe")
    store_p = (idx_p, wb_b, SEM_SB, "store")
    store_a = (idx_a, wb_a, SEM_SA, "store")
    store_b = (idx_b, wb_b, SEM_SB, "store")

    @pl.when(step == 0)
    def _():
        start_all(idx_a, buf_a, SEM_GA, "fetch")
        prev_first[0] = 0

    @pl.when(step > 0)
    def _():
        wait_all(wb_a, SEM_SA, "store")

    def overlaps(first, last, other_first, other_last):
        return jnp.logical_not((other_last < first) | (last < other_first))

    serial = (step == 0) | overlaps(idx_a[0, 0], idx_b[0, rows - 1], prev_first[0], idx_p[0, rows - 1])
    next_serial = overlaps(idx_n[0, 0], idx_m[0, rows - 1], idx_a[0, 0], idx_b[0, rows - 1])
    prev_first[0] = idx_a[0, 0]

    @pl.when(serial)
    def _():
        @pl.when(step == 0)
        def _():
            wait_all(buf_a, SEM_GA, "fetch")

        @pl.when(step > 0)
        def _():
            start_all(*store_p)
            wait_all(wb_b, SEM_SB, "store")
            wait_all(buf_a, SEM_GA, "fetch_moe")
            start_all(idx_a, buf_a, SEM_FIX, "fetch_res")
            wait_all(buf_a, SEM_FIX, "fetch_res")
        block(0, buf_a, SEM_GA, wb_a, [fetch_b], fetched=True)

    @pl.when(jnp.logical_not(serial))
    def _():
        block(0, buf_a, SEM_GA, wb_a, [store_p, fetch_b])
        wait_all(wb_b, SEM_SB, "store")

    @pl.when(next_serial)
    def _():
        block(1, buf_b, SEM_GB, wb_b, [store_a, fetch_n_moe])

    @pl.when(jnp.logical_not(next_serial))
    def _():
        block(1, buf_b, SEM_GB, wb_b, [store_a, fetch_n])

    @pl.when(step == last_step)
    def _():
        wait_all(buf_a, SEM_GA, "fetch_moe")
        start_all(*store_b)
        wait_all(wb_a, SEM_SA, "store")
        wait_all(wb_b, SEM_SB, "store")


def _experts(idx_blocks, xh, tok_rep, gate_rep, gate_f, w_gate, w_up, w_down, layer, seq, rows):
    n, d = xh.shape[0], D_MODEL
    e = w_gate.shape[1]
    cap = tok_rep.shape[0] // e
    nsteps = cap // (2 * rows)
    nbatch = gate_f.shape[0]
    seq_shift = seq.bit_length() - 1
    assert 1 << seq_shift == seq and nsteps * 2 * rows == cap
    wspec = pl.BlockSpec((1, 1, d, d), lambda i, j: (layer, i, 0, 0))
    any_spec = pl.BlockSpec(memory_space=pl.ANY)
    return pl.pallas_call(
        functools.partial(_experts_kernel, rows=rows, nsteps=nsteps, seq_shift=seq_shift, nbatch=nbatch,
                          col_chunk=EXPERT_COL_CHUNK),
        grid=(e, nsteps),
        in_specs=[
            any_spec, any_spec,
            pl.BlockSpec((2 * rows, LANES), lambda i, j: (i * nsteps + j, 0)),
            pl.BlockSpec((2 * rows, LANES), lambda i, j: (i * nsteps + j, 0)),
            pl.BlockSpec((nbatch, d), lambda i, j: (0, 0)),
            wspec, wspec, wspec,
        ],
        out_specs=any_spec,
        out_shape=jax.ShapeDtypeStruct((n, 2 * d), _F32),
        scratch_shapes=[
            pltpu.SMEM((1, rows), jnp.int32),
            pltpu.SMEM((1, rows), jnp.int32),
            pltpu.SMEM((1, rows), jnp.int32),
            pltpu.SMEM((1, rows), jnp.int32),
            pltpu.SMEM((1, rows), jnp.int32),
            pltpu.SMEM((1,), jnp.int32),
            pltpu.VMEM((rows, 2 * d), _F32),
            pltpu.VMEM((rows, 2 * d), _F32),
            pltpu.VMEM((rows, d), _F32),
            pltpu.VMEM((rows, d), _F32),
            pltpu.SemaphoreType.DMA((6,)),
        ],
        input_output_aliases={1: 0},
        compiler_params=_params(("arbitrary", "arbitrary")),
        name="experts",
    )(idx_blocks, xh, tok_rep, gate_rep, gate_f, w_gate, w_up, w_down)


def _final_kernel(x_ref, g_ref, o_ref):
    o_ref[0] = _rms(x_ref[0]) * g_ref[...]


def _final_norm(x, g, tile):
    b, s, d = x.shape[0], x.shape[1], D_MODEL
    return pl.pallas_call(
        _final_kernel,
        grid=(b, s // tile),
        in_specs=[pl.BlockSpec((1, tile, d), lambda i, t: (i, t, 0)),
                  pl.BlockSpec((1, d), lambda i, t: (0, 0))],
        out_specs=pl.BlockSpec((1, tile, d), lambda i, t: (i, t, 0)),
        out_shape=jax.ShapeDtypeStruct((b, s, d), _F32),
        compiler_params=_params(("arbitrary", "arbitrary")),
        name="final_norm",
    )(x, g)


def _channel_dft_table():
    k = jnp.arange(HEAD_DIM, dtype=jnp.int32)
    ang = ((k[:, None] * k[None, :]) % HEAD_DIM).astype(_F32) * (2.0 * math.pi / HEAD_DIM)
    groups = FOURIER_W // HEAD_DIM
    eye = jnp.eye(groups, dtype=_F32)
    c = jnp.kron(eye, jnp.cos(ang)) / math.sqrt(HEAD_DIM)
    s = jnp.kron(eye, jnp.sin(ang)) / math.sqrt(HEAD_DIM)
    return jnp.concatenate([c, s], axis=1).astype(_BF16)


def _split_hi_lo(w):
    hi = w.astype(_BF16)
    lo = (w - hi.astype(_F32)).astype(_BF16)
    return jnp.concatenate([hi, lo], axis=-1)


def _seq_dft_tables(s):
    radix = 64
    assert s % radix == 0
    t = jnp.arange(s, dtype=jnp.int32)[None, :]
    k1 = jnp.arange(s // radix, dtype=jnp.int32)[:, None]
    k0 = jnp.arange(radix, dtype=jnp.int32)[:, None]
    ang_a = ((k1 * radix * t) % s).astype(_F32) * (2.0 * math.pi / s)
    ang_b = ((k0 * t) % s).astype(_F32) * (2.0 * math.pi / s)
    ca, sa = jnp.cos(ang_a)[:, None, :], jnp.sin(ang_a)[:, None, :]
    cb, sb = jnp.cos(ang_b)[None, :, :], jnp.sin(ang_b)[None, :, :]
    cos = (ca * cb - sa * sb).astype(_BF16).reshape(s, s)
    nsin = (-(sa * cb + ca * sb)).astype(_BF16).reshape(s, s)
    return cos, nsin


def _encoder(x, mod, p, dft64, seq_tables, tile, erows=None):
    b, s, d = x.shape
    n = b * s
    cap = CAPACITY_FACTOR * n // N_EXPERTS
    if erows is None:
        erows = min(EXPERT_MAX_ROWS, cap // 4)
    assert cap % (2 * erows) == 0 and cap // erows >= 2
    wc, wsn = seq_tables
    depth = p["w_in"].shape[0]
    for l in range(depth):
        mod_l = mod[l][:, None, :]
        yac, fc, fs, fcm, fsm = _mix_in(x, mod_l, p["g_mix"][l][None], p["w_in_bf"][l], p["conv_w"][l],
                                        p["g_sgu"][l][None], p["w_sp_bf"][l], p["b_sp_rows"][l],
                                        p["g_grp"][l][None], dft64, tile)
        yf = _seq_dft(wc, wsn, fc, fs, fcm, fsm)
        xh, aff = _mix_out(x, yac, yf, mod_l, p["g_grp"][l][None], p["w_out_bf"][l],
                           p["g_ffn"][l][None], p["w_router_hl"][l], tile)
        aff_t = aff.reshape(n, N_EXPERTS).T.reshape(N_EXPERTS, n // LANES, LANES)
        tok_rep, gate_rep, idx = _route(aff_t, cap)
        idx_blocks = idx.reshape(N_EXPERTS * cap // erows, 1, erows)
        gate_f = mod_l[:, 0, 5 * d:6 * d]
        x = _experts(idx_blocks, xh.reshape(n, 2 * d),
                     tok_rep.reshape(N_EXPERTS * cap, LANES), gate_rep.reshape(N_EXPERTS * cap, LANES),
                     gate_f, p["w_gate"], p["w_up"], p["w_down"], l, s, erows).reshape(b, s, 2 * d)
    return _final_norm(x, p["g_final"][None], tile)


def kernel(x_prompt, x_sample, c_prompt, c_sample, w_ada, b_ada, g_mix, w_in, conv_w, g_sgu, w_spatial,
           b_spatial, g_grp, w_out, g_ffn, w_router, w_gate, w_up, w_down, g_final):
    nb = x_prompt.shape[0]
    mod = _ada(jnp.concatenate([c_prompt, c_sample], axis=0), w_ada, b_ada)
    p = dict(
        g_mix=g_mix, conv_w=conv_w, g_sgu=g_sgu, g_grp=g_grp, g_ffn=g_ffn, w_router_hl=_split_hi_lo(w_router),
        w_gate=w_gate, w_up=w_up, w_down=w_down, g_final=g_final, w_in=w_in,
        w_in_bf=w_in.astype(_BF16), w_out_bf=w_out.astype(_BF16), w_sp_bf=w_spatial.astype(_BF16),
        b_sp_rows=jnp.repeat(jnp.swapaxes(b_spatial, 1, 2), HEAD_DIM, axis=2),
    )
    dft64 = _channel_dft_table()
    tile = min(SEQ_TILE, x_sample.shape[1])
    y_prompt = _encoder(x_prompt, mod[:, :nb], p, dft64, _seq_dft_tables(x_prompt.shape[1]), tile)
    y_sample = _encoder(x_sample, mod[:, nb:], p, dft64, _seq_dft_tables(x_sample.shape[1]), tile)
    return (y_prompt, y_sample)
```

```python
import functools
import math

import jax
import jax.numpy as jnp
from jax import lax
from jax.experimental import pallas as pl
from jax.experimental.pallas import tpu as pltpu

D_MODEL = 1024
HEAD_DIM = 64
CONV_W = 384
FOURIER_W = 256
SGU_W = 384
SGU_HEADS = SGU_W // HEAD_DIM
CHUNK = 128
IN_W = 3 * CONV_W + FOURIER_W + 2 * SGU_W
N_EXPERTS = 16
CAPACITY_FACTOR = 2
N_MOD = 6
EPS = 1e-6

LANES = 128
SUBLANES = 8
VMEM_LIMIT_BYTES = 52 * 1024 * 1024
MXU_COLS = 256
EXPERT_COL_CHUNK = MXU_COLS
EXPERT_MAX_ROWS = 512
SEQ_TILE = 1024

_HI = lax.Precision.HIGHEST
_BF16 = jnp.bfloat16
_F32 = jnp.float32


def _rms(x):
    return x * lax.rsqrt(jnp.mean(x * x, axis=-1, keepdims=True) + EPS)


def _params(sem, vmem=VMEM_LIMIT_BYTES):
    return pltpu.CompilerParams(dimension_semantics=sem, vmem_limit_bytes=vmem)


def _ada_kernel(c_ref, w_ref, b_ref, o_ref):
    c = c_ref[...]
    a = c * jax.nn.sigmoid(c)
    o_ref[0] = jnp.dot(a, w_ref[0], precision=_HI, preferred_element_type=_F32) + b_ref[0]


def _ada(c_all, w_ada, b_ada):
    depth, d, m = w_ada.shape
    nb = c_all.shape[0]
    tn = 1536
    return pl.pallas_call(
        _ada_kernel,
        grid=(depth, m // tn),
        in_specs=[
            pl.BlockSpec((nb, d), lambda l, j: (0, 0)),
            pl.BlockSpec((1, d, tn), lambda l, j: (l, 0, j)),
            pl.BlockSpec((1, 1, tn), lambda l, j: (l, 0, j)),
        ],
        out_specs=pl.BlockSpec((1, nb, tn), lambda l, j: (l, 0, j)),
        out_shape=jax.ShapeDtypeStruct((depth, nb, m), _F32),
        compiler_params=_params(("arbitrary", "arbitrary")),
        name="ada",
    )(c_all, w_ada, b_ada.reshape(depth, 1, m))


def _mix_in_kernel(x_ref, xp_ref, xn_ref, mod_ref, gmix_ref, win_ref, convw_ref, gsgu_ref,
                   wsp_ref, bsp_ref, ggrp_ref, dft_ref, flip_ref, yac_ref, fc_ref, fs_ref, fcm_ref, fsm_ref,
                   *, tile):
    t = pl.program_id(1)
    nt = pl.num_programs(1)
    shift = mod_ref[0, :, 0:D_MODEL]
    scale = mod_ref[0, :, D_MODEL:2 * D_MODEL]
    gain = gmix_ref[...] * (1.0 + scale)

    def modulated(x):
        return (_rms(x) * gain + shift).astype(_BF16)

    h = modulated(x_ref[0])
    z = jnp.dot(h, win_ref[...], preferred_element_type=_F32)

    def halo(hh):
        zh = jnp.dot(hh, win_ref[:, CONV_W:3 * CONV_W], preferred_element_type=_F32)
        return zh[:, :CONV_W] * zh[:, CONV_W:]

    h_after = modulated(xn_ref[0])
    a_before = jnp.where(t > 0, halo(modulated(xp_ref[0]))[SUBLANES - 1:SUBLANES, :], 0.0)
    a_after = jnp.where(t < nt - 1, halo(h_after)[0:1, :], 0.0)

    zb = z[:, 0:CONV_W]
    a = z[:, CONV_W:2 * CONV_W] * z[:, 2 * CONV_W:3 * CONV_W]
    row = lax.broadcasted_iota(jnp.int32, (tile, CONV_W), 0)
    a_prev = jnp.where(row == 0, a_before, pltpu.roll(a, 1, axis=0))
    a_next = jnp.where(row == tile - 1, a_after, pltpu.roll(a, tile - 1, axis=0))
    cw = convw_ref[...]
    conv = zb * (cw[0:1, :] * a_prev + cw[1:2, :] * a + cw[2:3, :] * a_next)
    ya = (_rms(conv) * ggrp_ref[:, 0:CONV_W]).astype(_BF16)

    zf = z[:, 3 * CONV_W:3 * CONV_W + FOURIER_W].astype(_BF16)
    f2 = jnp.dot(zf, dft_ref[...], preferred_element_type=_F32)
    f2 = f2.astype(_BF16)
    fc_ref[...] = f2[:, :FOURIER_W]
    fs_ref[...] = f2[:, FOURIER_W:]

    @pl.when(t >= nt // 2)
    def _():
        zf_after = jnp.dot(h_after, win_ref[:, 3 * CONV_W:3 * CONV_W + FOURIER_W],
                           preferred_element_type=_F32).astype(_BF16)
        first = jnp.dot(zf_after, dft_ref[...], preferred_element_type=_F32)[0:1, :]
        first = jnp.where(t < nt - 1, first, 0.0)
        nb = tile // CHUNK
        row_in_group = lax.broadcasted_iota(jnp.int32, (CHUNK, 2 * FOURIER_W), 0)
        for a in range(nb):
            src = f2[(nb - 1 - a) * CHUNK:(nb - a) * CHUNK, :]
            m = jnp.dot(flip_ref[...], src, preferred_element_type=_F32)
            head = first if a == 0 else f2[(nb - a) * CHUNK:(nb - a) * CHUNK + 1, :].astype(_F32)
            m = jnp.where(row_in_group == 0, head, m).astype(_BF16)
            fcm_ref[a * CHUNK:(a + 1) * CHUNK, :] = m[:, :FOURIER_W]
            fsm_ref[a * CHUNK:(a + 1) * CHUNK, :] = m[:, FOURIER_W:]

    off = 3 * CONV_W + FOURIER_W
    u = jax.nn.gelu(z[:, off:off + SGU_W])
    v = (_rms(jax.nn.gelu(z[:, off + SGU_W:off + 2 * SGU_W])) * gsgu_ref[...]).astype(_BF16)
    lane = lax.broadcasted_iota(jnp.int32, (CHUNK, LANES), 1)
    first_head = lane < HEAD_DIM
    chunks = []
    for c in range(tile // CHUNK):
        pairs = []
        for k in range(SGU_W // LANES):
            vk = v[c * CHUNK:(c + 1) * CHUNK, k * LANES:(k + 1) * LANES]
            r0 = jnp.dot(wsp_ref[2 * k], vk, preferred_element_type=_F32)
            r1 = jnp.dot(wsp_ref[2 * k + 1], vk, preferred_element_type=_F32)
            pairs.append(jnp.where(first_head, r0, r1))
        chunks.append(jnp.concatenate(pairs, axis=1) + bsp_ref[...])
    sv = jnp.concatenate(chunks, axis=0)
    yc = (_rms(u * sv) * ggrp_ref[:, CONV_W + FOURIER_W:]).astype(_BF16)
    yac_ref[0] = jnp.concatenate([ya, yc], axis=1)


def _mix_in(x, mod_l, gmix, win_bf, convw, gsgu, wsp_bf, bsp_rows, ggrp, dft64, tile):
    b, s, d = x.shape[0], x.shape[1], D_MODEL
    nt = s // tile
    assert nt % 2 == 0
    pos = jnp.arange(CHUNK, dtype=jnp.int32)
    flip = (pos[:, None] + pos[None, :] == CHUNK).astype(_BF16)
    hb = tile // SUBLANES
    last_hb = s // SUBLANES - 1
    const2 = lambda i, t: (0, 0)
    return pl.pallas_call(
        functools.partial(_mix_in_kernel, tile=tile),
        grid=(b, nt),
        in_specs=[
            pl.BlockSpec((1, tile, d), lambda i, t: (i, t, 0)),
            pl.BlockSpec((1, SUBLANES, d), lambda i, t: (i, jnp.maximum(t * hb - 1, 0), 0)),
            pl.BlockSpec((1, SUBLANES, d), lambda i, t: (i, jnp.minimum((t + 1) * hb, last_hb), 0)),
            pl.BlockSpec((1, 1, N_MOD * d), lambda i, t: (i, 0, 0)),
            pl.BlockSpec((1, d), const2),
            pl.BlockSpec((d, IN_W), const2),
            pl.BlockSpec((3, CONV_W), const2),
            pl.BlockSpec((1, SGU_W), const2),
            pl.BlockSpec((SGU_HEADS, CHUNK, CHUNK), lambda i, t: (0, 0, 0)),
            pl.BlockSpec((CHUNK, SGU_W), const2),
            pl.BlockSpec((1, d), const2),
            pl.BlockSpec((FOURIER_W, 2 * FOURIER_W), const2),
            pl.BlockSpec((CHUNK, CHUNK), const2),
        ],
        out_specs=[
            pl.BlockSpec((1, tile, CONV_W + SGU_W), lambda i, t: (i, t, 0)),
            pl.BlockSpec((tile, FOURIER_W), lambda i, t: (t, i)),
            pl.BlockSpec((tile, FOURIER_W), lambda i, t: (t, i)),
            pl.BlockSpec((tile, FOURIER_W), lambda i, t: (jnp.minimum(nt - 1 - t, nt // 2 - 1), i)),
            pl.BlockSpec((tile, FOURIER_W), lambda i, t: (jnp.minimum(nt - 1 - t, nt // 2 - 1), i)),
        ],
        out_shape=[
            jax.ShapeDtypeStruct((b, s, CONV_W + SGU_W), _BF16),
            jax.ShapeDtypeStruct((s, b * FOURIER_W), _BF16),
            jax.ShapeDtypeStruct((s, b * FOURIER_W), _BF16),
            jax.ShapeDtypeStruct((s // 2, b * FOURIER_W), _BF16),
            jax.ShapeDtypeStruct((s // 2, b * FOURIER_W), _BF16),
        ],
        compiler_params=_params(("arbitrary", "arbitrary")),
        name="mix_in",
    )(x, x, x, mod_l, gmix, win_bf, convw, gsgu, wsp_bf, bsp_rows, ggrp, dft64, flip)


def _seq_dft_kernel(wc_ref, ws_ref, fc_ref, fs_ref, fcm_ref, fsm_ref, mid_ref, o_ref, *, tm):
    gc = (fc_ref[...].astype(_F32) + fcm_ref[...].astype(_F32)).astype(_BF16)
    gs = (fs_ref[...].astype(_F32) - fsm_ref[...].astype(_F32)).astype(_BF16)
    part = (jnp.dot(wc_ref[...], gc, preferred_element_type=_F32)
            + jnp.dot(ws_ref[...], gs, preferred_element_type=_F32))

    @pl.when(pl.program_id(2) == 0)
    def _():
        k = pl.program_id(0) * tm + lax.broadcasted_iota(jnp.int32, (tm, 1), 0)
        sign = (1 - 2 * (k & 1)).astype(_F32)
        o_ref[...] = part + sign * mid_ref[0:1, :].astype(_F32)

    @pl.when(pl.program_id(2) > 0)
    def _():
        o_ref[...] += part


def _seq_dft(wc, wsn, fc, fs, fcm, fsm):
    s = wc.shape[0]
    n = fc.shape[1]
    half = s // 2
    tm = min(1024, s)
    tn = min(1024, n)
    tk = min(1024, half)
    fblock = pl.BlockSpec((tk, tn), lambda i, j, k: (k, j))
    return pl.pallas_call(
        functools.partial(_seq_dft_kernel, tm=tm),
        grid=(s // tm, n // tn, half // tk),
        in_specs=[
            pl.BlockSpec((tm, tk), lambda i, j, k: (i, k)),
            pl.BlockSpec((tm, tk), lambda i, j, k: (i, k)),
            fblock, fblock, fblock, fblock,
            pl.BlockSpec((SUBLANES, tn), lambda i, j, k: (0, j)),
        ],
        out_specs=pl.BlockSpec((tm, tn), lambda i, j, k: (i, j)),
        out_shape=jax.ShapeDtypeStruct((s, n), _F32),
        compiler_params=_params(("arbitrary", "arbitrary", "arbitrary")),
        name="seq_dft",
    )(wc, wsn, fc, fs, fcm, fsm, fc[half:half + SUBLANES])


def _mix_out_kernel(x_ref, yac_ref, yf_ref, mod_ref, ggrp_ref, wout_ref, gffn_ref, wr_ref,
                    xh_ref, aff_ref, *, seq_scale):
    d = D_MODEL
    gate_m = mod_ref[0, :, 2 * d:3 * d]
    shift_f = mod_ref[0, :, 3 * d:4 * d]
    scale_f = mod_ref[0, :, 4 * d:5 * d]
    yf = (_rms(yf_ref[...] * seq_scale) * ggrp_ref[:, CONV_W:CONV_W + FOURIER_W]).astype(_BF16)
    yac = yac_ref[0]
    ycat = jnp.concatenate([yac[:, :CONV_W], yf, yac[:, CONV_W:]], axis=1)
    mix = jnp.dot(ycat, wout_ref[...], preferred_element_type=_F32)
    x1 = x_ref[0] + gate_m * mix
    h2 = _rms(x1) * (gffn_ref[...] * (1.0 + scale_f)) + shift_f
    xh_ref[0, :, :d] = x1
    xh_ref[0, :, d:] = h2
    h_hi = h2.astype(_BF16)
    h_lo = (h2 - h_hi.astype(_F32)).astype(_BF16)
    by_hi = jnp.dot(h_hi, wr_ref[...], preferred_element_type=_F32)
    logits = (by_hi[:, :N_EXPERTS] + (by_hi[:, N_EXPERTS:]
              + jnp.dot(h_lo, wr_ref[:, :N_EXPERTS], preferred_element_type=_F32)))
    e = jnp.exp(logits - jnp.max(logits, axis=-1, keepdims=True))
    aff_ref[0] = e / jnp.sum(e, axis=-1, keepdims=True)


def _mix_out(x, yac, yf, mod_l, ggrp, wout_bf, gffn, w_router, tile):
    b, s, d = x.shape[0], x.shape[1], D_MODEL
    const2 = lambda i, t: (0, 0)
    return pl.pallas_call(
        functools.partial(_mix_out_kernel, seq_scale=1.0 / math.sqrt(s)),
        grid=(b, s // tile),
        in_specs=[
            pl.BlockSpec((1, tile, d), lambda i, t: (i, t, 0)),
            pl.BlockSpec((1, tile, CONV_W + SGU_W), lambda i, t: (i, t, 0)),
            pl.BlockSpec((tile, FOURIER_W), lambda i, t: (t, i)),
            pl.BlockSpec((1, 1, N_MOD * d), lambda i, t: (i, 0, 0)),
            pl.BlockSpec((1, d), const2),
            pl.BlockSpec((d, d), const2),
            pl.BlockSpec((1, d), const2),
            pl.BlockSpec((d, 2 * N_EXPERTS), const2),
        ],
        out_specs=[
            pl.BlockSpec((1, tile, 2 * d), lambda i, t: (i, t, 0)),
            pl.BlockSpec((1, tile, N_EXPERTS), lambda i, t: (i, t, 0)),
        ],
        out_shape=[
            jax.ShapeDtypeStruct((b, s, 2 * d), _F32),
            jax.ShapeDtypeStruct((b, s, N_EXPERTS), _F32),
        ],
        compiler_params=_params(("arbitrary", "arbitrary")),
        name="mix_out",
    )(x, yac, yf, mod_l, ggrp, wout_bf, gffn, w_router)


def _cumsum_mats(rows):
    qi = lax.broadcasted_iota(jnp.int32, (LANES, LANES), 0)
    qj = lax.broadcasted_iota(jnp.int32, (LANES, LANES), 1)
    ri = lax.broadcasted_iota(jnp.int32, (rows, rows), 0)
    rj = lax.broadcasted_iota(jnp.int32, (rows, rows), 1)
    def ones_where(mask):
        return jnp.where(mask, 1.0, 0.0).astype(_BF16)

    lane_excl = ones_where(qi < qj)
    lane_incl = ones_where(qi <= qj)
    row_excl = ones_where(rj < ri)
    row_incl_t = ones_where(ri <= rj)
    return lane_excl, lane_incl, row_excl, row_incl_t


def _route_kernel(aff_ref, idx_ref, gate_ref, idxc_ref, *, cap, jb):
    a = aff_ref[0]
    rows = a.shape[0]
    lane_excl, lane_incl, row_excl, row_incl_t = _cumsum_mats(rows)

    def count(mask):
        return jnp.sum(jnp.where(mask, 1.0, 0.0)).astype(jnp.int32)

    def refine(thr_bits, shift, nbits):
        passing = jnp.int32(0)
        for m in range(1, 1 << nbits):
            cand_f = lax.bitcast_convert_type(thr_bits | (jnp.int32(m) << shift), _F32)
            passing += (count(a >= cand_f) >= cap).astype(jnp.int32)
        return thr_bits | (passing << shift)

    thr_bits = lax.fori_loop(0, 10, lambda i, t: refine(t, 27 - 3 * i, 3), jnp.int32(0))
    thr = lax.bitcast_convert_type(thr_bits, _F32)
    above = a > thr
    tied = a == thr
    need = cap - count(above)

    tied_b = jnp.where(tied, 1.0, 0.0).astype(_BF16)
    in_row = jnp.dot(tied_b, lane_excl, preferred_element_type=_F32)
    row_tot = jnp.dot(tied_b, jnp.ones((LANES, LANES), _BF16), preferred_element_type=_F32)
    before_row = jnp.dot(row_excl, row_tot.astype(_BF16), preferred_element_type=_F32)
    tie_rank = (in_row + before_row).astype(jnp.int32)
    sel = above | (tied & (tie_rank < need))
    sel_b = jnp.where(sel, 1.0, 0.0).astype(_BF16)

    row_cnt = lax.dot_general(jnp.ones((SUBLANES, LANES), _BF16), sel_b,
                              (((1,), (1,)), ((), ())), preferred_element_type=_F32)
    incl = jnp.dot(row_cnt.astype(_BF16), row_incl_t, preferred_element_type=_F32)[0:1, :]
    excl = incl - row_cnt[0:1, :]
    a_hi = a.astype(_BF16)
    a_mid = (a - a_hi.astype(_F32)).astype(_BF16)
    a_lo = (a - a_hi.astype(_F32) - a_mid.astype(_F32)).astype(_BF16)
    row_id = lax.broadcasted_iota(jnp.int32, (1, rows), 1).astype(_F32)
    lane_id = lax.broadcasted_iota(jnp.int32, (jb, LANES), 1).astype(_F32)
    diag = (lax.broadcasted_iota(jnp.int32, (LANES, LANES), 0)
            == lax.broadcasted_iota(jnp.int32, (LANES, LANES), 1))

    for c in range(cap // jb):
        j = (lax.broadcasted_iota(jnp.int32, (jb, 1), 0) + c * jb).astype(_F32)
        hit = (excl <= j) & (j < incl)
        hit_f = jnp.where(hit, 1.0, 0.0)
        r_of_j = jnp.sum(hit_f * row_id, axis=1, keepdims=True)
        target = j + 1.0 - jnp.sum(hit_f * excl, axis=1, keepdims=True)
        row_sel = jnp.dot(hit_f.astype(_BF16), sel_b, preferred_element_type=_F32)
        within = jnp.dot(row_sel.astype(_BF16), lane_incl, preferred_element_type=_F32)
        l_of_j = jnp.sum(jnp.where(within < target, 1.0, 0.0), axis=1, keepdims=True)
        hit_b = hit_f.astype(_BF16)
        a_row = (jnp.dot(hit_b, a_hi, preferred_element_type=_F32)
                 + jnp.dot(hit_b, a_mid, preferred_element_type=_F32)
                 + jnp.dot(hit_b, a_lo, preferred_element_type=_F32))
        gate = jnp.sum(jnp.where(lane_id == l_of_j, a_row, 0.0), axis=1, keepdims=True)
        token_rep = jnp.broadcast_to(r_of_j * LANES + l_of_j, (jb, LANES))
        idx_ref[0, c * jb:(c + 1) * jb, :] = token_rep.astype(jnp.int32)
        gate_ref[0, c * jb:(c + 1) * jb, :] = jnp.broadcast_to(gate, (jb, LANES))
        for g in range(jb // LANES):
            grp = jnp.where(diag, token_rep[g * LANES:(g + 1) * LANES, :], 0.0)
            out_row = c * (jb // LANES) + g
            idxc_ref[0, out_row:out_row + 1, :] = jnp.sum(grp, axis=0, keepdims=True).astype(jnp.int32)


def _route(aff_t, cap):
    e, rows, _ = aff_t.shape
    jb = min(1024, cap)
    return pl.pallas_call(
        functools.partial(_route_kernel, cap=cap, jb=jb),
        grid=(e,),
        in_specs=[pl.BlockSpec((1, rows, LANES), lambda i: (i, 0, 0))],
        out_specs=[
            pl.BlockSpec((1, cap, LANES), lambda i: (i, 0, 0)),
            pl.BlockSpec((1, cap, LANES), lambda i: (i, 0, 0)),
            pl.BlockSpec((1, cap // LANES, LANES), lambda i: (i, 0, 0)),
        ],
        out_shape=[
            jax.ShapeDtypeStruct((e, cap, LANES), jnp.int32),
            jax.ShapeDtypeStruct((e, cap, LANES), _F32),
            jax.ShapeDtypeStruct((e, cap // LANES, LANES), jnp.int32),
        ],
        compiler_params=_params(("arbitrary",)),
        name="route",
    )(aff_t)


def _experts_kernel(idx_hbm, xin_hbm, tok_ref, gate_ref, gf_ref, wg_ref, wu_ref, wd_ref,
                    xh_hbm, idx_p, idx_a, idx_b, idx_n, idx_m, prev_first, buf_a, buf_b, wb_a, wb_b, sems,
                    *, rows, nsteps, seq_shift, nbatch, col_chunk):
    del xin_hbm
    d = D_MODEL
    e = pl.program_id(0)
    j = pl.program_id(1)
    step = e * nsteps + j
    last_step = pl.num_programs(0) * nsteps - 1
    blk_a = 2 * step
    blk_p = jnp.maximum(blk_a - 1, 0)
    last_blk = 2 * last_step + 1
    blk_n = jnp.minimum(blk_a + 2, last_blk)
    blk_m = jnp.minimum(blk_a + 3, last_blk)
    SEM_IDX, SEM_GA, SEM_GB, SEM_SA, SEM_SB, SEM_FIX = range(6)

    idx_copies = [pltpu.make_async_copy(idx_hbm.at[b], ref, sems.at[SEM_IDX])
                  for b, ref in ((blk_p, idx_p), (blk_a, idx_a), (blk_a + 1, idx_b), (blk_n, idx_n),
                                 (blk_m, idx_m))]
    for c in idx_copies:
        c.start()
    for c in idx_copies:
        c.wait()

    res_cols = pl.ds(0, d)
    moe_cols = pl.ds(d, d)

    def row_copy(idx_ref, r, buf, sem, kind):
        tok = idx_ref[0, r]
        if kind == "store":
            return pltpu.make_async_copy(buf.at[pl.ds(r, 1)], xh_hbm.at[pl.ds(tok, 1), res_cols], sems.at[sem])
        if kind == "fetch":
            return pltpu.make_async_copy(xh_hbm.at[pl.ds(tok, 1)], buf.at[pl.ds(r, 1)], sems.at[sem])
        cols = moe_cols if kind == "fetch_moe" else res_cols
        return pltpu.make_async_copy(xh_hbm.at[pl.ds(tok, 1), cols], buf.at[pl.ds(r, 1), cols], sems.at[sem])

    def start_all(idx_ref, buf, sem, kind):
        def body(r, carry):
            row_copy(idx_ref, r, buf, sem, kind).start()
            return carry
        lax.fori_loop(0, rows, body, 0)

    def wait_all(buf, sem, kind):
        all_rows = pl.ds(0, rows)
        if kind == "store":
            pltpu.make_async_copy(buf, xh_hbm.at[all_rows, res_cols], sems.at[sem]).wait()
        elif kind == "fetch":
            pltpu.make_async_copy(xh_hbm.at[all_rows], buf, sems.at[sem]).wait()
        else:
            cols = moe_cols if kind == "fetch_moe" else res_cols
            pltpu.make_async_copy(xh_hbm.at[all_rows, cols], buf.at[:, cols], sems.at[sem]).wait()

    def block(part, buf, sem_g, wb, streams, fetched=False):
        if not fetched:
            wait_all(buf, sem_g, "fetch")
        todo = [functools.partial(row_copy, stream[0], r, *stream[1:])
                for r in range(rows) for stream in streams]
        nch = D_MODEL // col_chunk
        per_call = -(-len(todo) // (3 * nch))

        def start_some():
            for k, make in enumerate(todo[:per_call]):
                make().start(priority=k % 2)
            del todo[:per_call]

        x = buf[:, d:].astype(_BF16)
        hid = []
        for c in range(nch):
            cs = slice(c * col_chunk, (c + 1) * col_chunk)
            hg = jnp.dot(x, wg_ref[0, 0, :, cs].astype(_BF16), preferred_element_type=_F32)
            start_some()
            hu = jnp.dot(x, wu_ref[0, 0, :, cs].astype(_BF16), preferred_element_type=_F32)
            start_some()
            hid.append((hg * jax.nn.sigmoid(hg) * hu).astype(_BF16))
        hid = jnp.concatenate(hid, axis=1)
        rs = slice(part * rows, (part + 1) * rows)
        seq_of_row = tok_ref[rs, :] >> seq_shift
        gate = gate_ref[rs, :]
        upd = []
        for c in range(nch):
            cs = slice(c * col_chunk, (c + 1) * col_chunk)
            y = jnp.dot(hid, wd_ref[0, 0, :, cs].astype(_BF16), preferred_element_type=_F32)
            start_some()
            seq_c = jnp.concatenate([seq_of_row] * (col_chunk // LANES), axis=1)
            gf = jnp.zeros_like(y)
            for b in range(nbatch):
                gf = jnp.where(seq_c == b, gf_ref[b:b + 1, cs], gf)
            upd.append(gf * (y * jnp.concatenate([gate] * (col_chunk // LANES), axis=1)))
        assert not todo
        wb[...] = buf[:, :d] + jnp.concatenate(upd, axis=1)

    fetch_b = (idx_b, buf_b, SEM_GB, "fetch")
    fetch_n = (idx_n, buf_a, SEM_GA, "fetch")
    fetch_n_moe = (idx_n, buf_a, SEM_GA, "fetch_moe")
    store_p = (idx_p, wb_b, SEM_SB, "store")
    store_a = (idx_a, wb_a, SEM_SA, "store")
    store_b = (idx_b, wb_b, SEM_SB, "store")

    @pl.when(step == 0)
    def _():
        start_all(idx_a, buf_a, SEM_GA, "fetch")
        prev_first[0] = 0

    @pl.when(step > 0)
    def _():
        wait_all(wb_a, SEM_SA, "store")

    def overlaps(first, last, other_first, other_last):
        return jnp.logical_not((other_last < first) | (last < other_first))

    serial = (step == 0) | overlaps(idx_a[0, 0], idx_b[0, rows - 1], prev_first[0], idx_p[0, rows - 1])
    next_serial = overlaps(idx_n[0, 0], idx_m[0, rows - 1], idx_a[0, 0], idx_b[0, rows - 1])
    prev_first[0] = idx_a[0, 0]

    @pl.when(serial)
    def _():
        @pl.when(step == 0)
        def _():
            wait_all(buf_a, SEM_GA, "fetch")

        @pl.when(step > 0)
        def _():
            start_all(*store_p)
            wait_all(wb_b, SEM_SB, "store")
            wait_all(buf_a, SEM_GA, "fetch_moe")
            start_all(idx_a, buf_a, SEM_FIX, "fetch_res")
            wait_all(buf_a, SEM_FIX, "fetch_res")
        block(0, buf_a, SEM_GA, wb_a, [fetch_b], fetched=True)

    @pl.when(jnp.logical_not(serial))
    def _():
        block(0, buf_a, SEM_GA, wb_a, [store_p, fetch_b])
        wait_all(wb_b, SEM_SB, "store")

    @pl.when(next_serial)
    def _():
        block(1, buf_b, SEM_GB, wb_b, [store_a, fetch_n_moe])

    @pl.when(jnp.logical_not(next_serial))
    def _():
        block(1, buf_b, SEM_GB, wb_b, [store_a, fetch_n])

    @pl.when(step == last_step)
    def _():
        wait_all(buf_a, SEM_GA, "fetch_moe")
        start_all(*store_b)
        wait_all(wb_a, SEM_SA, "store")
        wait_all(wb_b, SEM_SB, "store")


def _experts(idx_blocks, xh, tok_rep, gate_rep, gate_f, w_gate, w_up, w_down, layer, seq, rows):
    n, d = xh.shape[0], D_MODEL
    e = w_gate.shape[1]
    cap = tok_rep.shape[0] // e
    nsteps = cap // (2 * rows)
    nbatch = gate_f.shape[0]
    seq_shift = seq.bit_length() - 1
    assert 1 << seq_shift == seq and nsteps * 2 * rows == cap
    wspec = pl.BlockSpec((1, 1, d, d), lambda i, j: (layer, i, 0, 0))
    any_spec = pl.BlockSpec(memory_space=pl.ANY)
    return pl.pallas_call(
        functools.partial(_experts_kernel, rows=rows, nsteps=nsteps, seq_shift=seq_shift, nbatch=nbatch,
                          col_chunk=EXPERT_COL_CHUNK),
        grid=(e, nsteps),
        in_specs=[
            any_spec, any_spec,
            pl.BlockSpec((2 * rows, LANES), lambda i, j: (i * nsteps + j, 0)),
            pl.BlockSpec((2 * rows, LANES), lambda i, j: (i * nsteps + j, 0)),
            pl.BlockSpec((nbatch, d), lambda i, j: (0, 0)),
            wspec, wspec, wspec,
        ],
        out_specs=any_spec,
        out_shape=jax.ShapeDtypeStruct((n, 2 * d), _F32),
        scratch_shapes=[
            pltpu.SMEM((1, rows), jnp.int32),
            pltpu.SMEM((1, rows), jnp.int32),
            pltpu.SMEM((1, rows), jnp.int32),
            pltpu.SMEM((1, rows), jnp.int32),
            pltpu.SMEM((1, rows), jnp.int32),
            pltpu.SMEM((1,), jnp.int32),
            pltpu.VMEM((rows, 2 * d), _F32),
            pltpu.VMEM((rows, 2 * d), _F32),
            pltpu.VMEM((rows, d), _F32),
            pltpu.VMEM((rows, d), _F32),
            pltpu.SemaphoreType.DMA((6,)),
        ],
        input_output_aliases={1: 0},
        compiler_params=_params(("arbitrary", "arbitrary")),
        name="experts",
    )(idx_blocks, xh, tok_rep, gate_rep, gate_f, w_gate, w_up, w_down)


def _final_kernel(x_ref, g_ref, o_ref):
    o_ref[0] = _rms(x_ref[0]) * g_ref[...]


def _final_norm(x, g, tile):
    b, s, d = x.shape[0], x.shape[1], D_MODEL
    return pl.pallas_call(
        _final_kernel,
        grid=(b, s // tile),
        in_specs=[pl.BlockSpec((1, tile, d), lambda i, t: (i, t, 0)),
                  pl.BlockSpec((1, d), lambda i, t: (0, 0))],
        out_specs=pl.BlockSpec((1, tile, d), lambda i, t: (i, t, 0)),
        out_shape=jax.ShapeDtypeStruct((b, s, d), _F32),
        compiler_params=_params(("arbitrary", "arbitrary")),
        name="final_norm",
    )(x, g)


def _channel_dft_table():
    k = jnp.arange(HEAD_DIM, dtype=jnp.int32)
    ang = ((k[:, None] * k[None, :]) % HEAD_DIM).astype(_F32) * (2.0 * math.pi / HEAD_DIM)
    groups = FOURIER_W // HEAD_DIM
    eye = jnp.eye(groups, dtype=_F32)
    c = jnp.kron(eye, jnp.cos(ang)) / math.sqrt(HEAD_DIM)
    s = jnp.kron(eye, jnp.sin(ang)) / math.sqrt(HEAD_DIM)
    return jnp.concatenate([c, s], axis=1).astype(_BF16)


def _split_hi_lo(w):
    hi = w.astype(_BF16)
    lo = (w - hi.astype(_F32)).astype(_BF16)
    return jnp.concatenate([hi, lo], axis=-1)


def _seq_dft_tables(s):
    radix = 64
    assert s % radix == 0
    t = jnp.arange(s, dtype=jnp.int32)[None, :]
    k1 = jnp.arange(s // radix, dtype=jnp.int32)[:, None]
    k0 = jnp.arange(radix, dtype=jnp.int32)[:, None]
    ang_a = ((k1 * radix * t) % s).astype(_F32) * (2.0 * math.pi / s)
    ang_b = ((k0 * t) % s).astype(_F32) * (2.0 * math.pi / s)
    ca, sa = jnp.cos(ang_a)[:, None, :], jnp.sin(ang_a)[:, None, :]
    cb, sb = jnp.cos(ang_b)[None, :, :], jnp.sin(ang_b)[None, :, :]
    cos = (ca * cb - sa * sb).astype(_BF16).reshape(s, s)
    nsin = (-(sa * cb + ca * sb)).astype(_BF16).reshape(s, s)
    return cos, nsin


def _encoder(x, mod, p, dft64, seq_tables, tile, erows=None):
    b, s, d = x.shape
    n = b * s
    cap = CAPACITY_FACTOR * n // N_EXPERTS
    if erows is None:
        erows = min(EXPERT_MAX_ROWS, cap // 4)
    assert cap % (2 * erows) == 0 and cap // erows >= 2
    wc, wsn = seq_tables
    depth = p["w_in"].shape[0]
    for l in range(depth):
        mod_l = mod[l][:, None, :]
        yac, fc, fs, fcm, fsm = _mix_in(x, mod_l, p["g_mix"][l][None], p["w_in_bf"][l], p["conv_w"][l],
                                        p["g_sgu"][l][None], p["w_sp_bf"][l], p["b_sp_rows"][l],
                                        p["g_grp"][l][None], dft64, tile)
        yf = _seq_dft(wc, wsn, fc, fs, fcm, fsm)
        xh, aff = _mix_out(x, yac, yf, mod_l, p["g_grp"][l][None], p["w_out_bf"][l],
                           p["g_ffn"][l][None], p["w_router_hl"][l], tile)
        aff_t = aff.reshape(n, N_EXPERTS).T.reshape(N_EXPERTS, n // LANES, LANES)
        tok_rep, gate_rep, idx = _route(aff_t, cap)
        idx_blocks = idx.reshape(N_EXPERTS * cap // erows, 1, erows)
        gate_f = mod_l[:, 0, 5 * d:6 * d]
        x = _experts(idx_blocks, xh.reshape(n, 2 * d),
                     tok_rep.reshape(N_EXPERTS * cap, LANES), gate_rep.reshape(N_EXPERTS * cap, LANES),
                     gate_f, p["w_gate"], p["w_up"], p["w_down"], l, s, erows).reshape(b, s, 2 * d)
    return _final_norm(x, p["g_final"][None], tile)


def kernel(x_prompt, x_sample, c_prompt, c_sample, w_ada, b_ada, g_mix, w_in, conv_w, g_sgu, w_spatial,
           b_spatial, g_grp, w_out, g_ffn, w_router, w_gate, w_up, w_down, g_final):
    nb = x_prompt.shape[0]
    mod = _ada(jnp.concatenate([c_prompt, c_sample], axis=0), w_ada, b_ada)
    p = dict(
        g_mix=g_mix, conv_w=conv_w, g_sgu=g_sgu, g_grp=g_grp, g_ffn=g_ffn, w_router_hl=_split_hi_lo(w_router),
        w_gate=w_gate, w_up=w_up, w_down=w_down, g_final=g_final, w_in=w_in,
        w_in_bf=w_in.astype(_BF16), w_out_bf=w_out.astype(_BF16), w_sp_bf=w_spatial.astype(_BF16),
        b_sp_rows=jnp.repeat(jnp.swapaxes(b_spatial, 1, 2), HEAD_DIM, axis=2),
    )
    dft64 = _channel_dft_table()
    tile = min(SEQ_TILE, x_sample.shape[1])
    y_prompt = _encoder(x_prompt, mod[:, :nb], p, dft64, _seq_dft_tables(x_prompt.shape[1]), tile)
    y_sample = _encoder(x_sample, mod[:, nb:], p, dft64, _seq_dft_tables(x_sample.shape[1]), tile)
    return (y_prompt, y_sample)
```
